```python
import numpy as np
import jax
import jax.numpy as jnp
from jax import lax

D_MODEL = 1024
BATCH = 2
SEQ = 8192
DEPTH = 4

N_META = 16
BLOCK_Q = 128
D_MIX = D_MODEL
N_GROUPS = 4
GROUP_W = D_MIX // N_GROUPS
HEAD_DIM = 64
FOX_HEADS = GROUP_W // HEAD_DIM
CONV_WIDTH = 31
LRU_BLOCKS = 4
LRU_CONV_WIDTH = 4
LRU_C = 8.0
DSA_HEADS = GROUP_W // HEAD_DIM
DSA_LATENT = 128
IDX_HEADS = 8
IDX_DIM = 32
TOPK_MAX = 256
D_FF = 2560
RMS_EPS = 1e-6
LN_EPS = 1e-5

SPLIT_SIZES = (
    GROUP_W, GROUP_W, GROUP_W, FOX_HEADS,
    2 * GROUP_W,
    GROUP_W, GROUP_W,
    DSA_HEADS * HEAD_DIM, DSA_LATENT,
    IDX_HEADS * IDX_DIM, IDX_DIM, IDX_HEADS,
)
D_IN = sum(SPLIT_SIZES)

kernel_name = "hymba_fox_conformer_rglru_dsa_trunk"


def rms_norm(x, g):
    xf = x.astype(jnp.float32)
    y = xf * lax.rsqrt(jnp.mean(xf * xf, axis=-1, keepdims=True) + RMS_EPS)
    return (y * g.astype(jnp.float32)).astype(x.dtype)


def layer_norm(x, g, b):
    xf = x.astype(jnp.float32)
    mu = jnp.mean(xf, axis=-1, keepdims=True)
    var = jnp.mean(jnp.square(xf - mu), axis=-1, keepdims=True)
    y = (xf - mu) * lax.rsqrt(var + LN_EPS) * g.astype(jnp.float32) + b.astype(jnp.float32)
    return y.astype(x.dtype)


def swiglu(x, w_in, w_out):
    gate, up = jnp.split(x @ w_in, 2, axis=-1)
    return (jax.nn.silu(gate) * up) @ w_out


def causal_depthwise_conv(x, w, b):
    width = w.shape[0]
    y = lax.conv_general_dilated(
        x, w[:, None, :].astype(x.dtype), window_strides=(1,),
        padding=[(width - 1, 0)], dimension_numbers=("NWC", "WIO", "NWC"),
        feature_group_count=x.shape[-1])
    return y + b.astype(x.dtype)


def block_sweep(fn, per_query):
    b, t = per_query[0].shape[:2]
    n_blk = (t - N_META) // BLOCK_Q
    out_meta = fn(tuple(a[:, :N_META] for a in per_query), jnp.arange(N_META))
    real = tuple(a[:, N_META:].reshape((b, n_blk, BLOCK_Q) + a.shape[2:]).swapaxes(0, 1)
                 for a in per_query)
    pos = (N_META + jnp.arange(n_blk * BLOCK_Q)).reshape(n_blk, BLOCK_Q)
    out_real = lax.map(lambda qp: fn(qp[0], qp[1]), (real, pos))
    out_real = out_real.swapaxes(0, 1).reshape((b, n_blk * BLOCK_Q) + out_real.shape[3:])
    return jnp.concatenate([out_meta, out_real], axis=1)


def forgetting_attention(q, k, v, f_logit, b_f):
    log_f = jax.nn.log_sigmoid(f_logit.astype(jnp.float32) + b_f.astype(jnp.float32))
    cum = jnp.cumsum(log_f, axis=1)
    cum_k = cum.transpose(0, 2, 1)
    key_pos = jnp.arange(q.shape[1])
    scale = HEAD_DIM ** -0.5

    def attend(qa, q_pos):
        q_blk, cum_q = qa
        s = jnp.einsum("bqhd,bkhd->bhqk", q_blk, k).astype(jnp.float32) * scale
        s = s + cum_q.transpose(0, 2, 1)[..., None] - cum_k[:, :, None, :]
        s = jnp.where(key_pos[None, None, None, :] <= q_pos[None, None, :, None], s, -jnp.inf)
        p = jax.nn.softmax(s, axis=-1).astype(v.dtype)
        return jnp.einsum("bhqk,bkhd->bqhd", p, v)

    return block_sweep(attend, (q, cum))


def conformer_conv(u, dw_w, dw_b, ln_g, ln_b):
    a, g = jnp.split(u, 2, axis=-1)
    h = a * jax.nn.sigmoid(g)
    h = causal_depthwise_conv(h, dw_w, dw_b)
    h = layer_norm(h, ln_g, ln_b)
    return jax.nn.silu(h)


def rg_lru_branch(xb, gb, conv_w, conv_b, w_a, b_a, w_i, b_i, lam):
    xc = causal_depthwise_conv(xb, conv_w, conv_b)
    bsz, t, w = xc.shape
    xh = xc.reshape(bsz, t, LRU_BLOCKS, w // LRU_BLOCKS)
    r = jax.nn.sigmoid(jnp.einsum("btnc,ncd->btnd", xh, w_a).reshape(bsz, t, w) + b_a)
    i = jax.nn.sigmoid(jnp.einsum("btnc,ncd->btnd", xh, w_i).reshape(bsz, t, w) + b_i)
    log_a = LRU_C * r.astype(jnp.float32) * jax.nn.log_sigmoid(lam.astype(jnp.float32))
    a = jnp.exp(log_a)
    u = jnp.sqrt(-jnp.expm1(2.0 * log_a)) * (i.astype(jnp.float32) * xc.astype(jnp.float32))

    def combine(left, right):
        a1, b1 = left
        a2, b2 = right
        return a1 * a2, a2 * b1 + b2

    _, h = lax.associative_scan(combine, (a, u), axis=1)
    return h.astype(xb.dtype) * jax.nn.gelu(gb)


def dsa_attention(q, c_kv, q_idx, k_idx, w_idx, kv_g, w_uk, w_uv, k_ln_g, k_ln_b, topk):
    c = rms_norm(c_kv, kv_g)
    k_i = layer_norm(k_idx, k_ln_g, k_ln_b)
    q_lat = jnp.einsum("bthd,hcd->bthc", q, w_uk)
    w_h = w_idx.astype(jnp.float32) * (IDX_HEADS ** -0.5 * IDX_DIM ** -0.5)
    bsz, t = c.shape[:2]
    key_pos = jnp.arange(t)
    bidx = jnp.arange(bsz)[:, None, None]
    scale = HEAD_DIM ** -0.5

    def attend(qa, q_pos):
        ql, qi, wh = qa
        sc = jax.nn.relu(jnp.einsum("bqhd,bkd->bqhk", qi, k_i).astype(jnp.float32))
        sc = jnp.einsum("bqhk,bqh->bqk", sc, wh)
        sc = jnp.where(key_pos[None, None, :] <= q_pos[None, :, None], sc, -jnp.inf)
        _, sel = lax.top_k(sc, topk)
        c_sel = c[bidx, sel]
        s = jnp.einsum("bqhc,bqkc->bhqk", ql, c_sel).astype(jnp.float32) * scale
        s = jnp.where((sel <= q_pos[None, :, None])[:, None], s, -jnp.inf)
        p = jax.nn.softmax(s, axis=-1).astype(c.dtype)
        return jnp.einsum("bhqk,bqkc->bqhc", p, c_sel)

    o_lat = block_sweep(attend, (q_lat, q_idx, w_h))
    return jnp.einsum("bthc,hcd->bthd", o_lat, w_uv)


def setup_inputs(seed: int = 0) -> dict:
    key = jax.random.key(seed)
    ks = jax.random.split(key, 32)
    f32 = jnp.float32
    nrm = lambda k, shape, s: jax.random.normal(k, shape, f32) * s
    u = jax.random.uniform(ks[20], (DEPTH, GROUP_W), f32, 0.9, 0.999)
    base = u ** (1.0 / LRU_C)
    lam = jnp.log(base) - jnp.log1p(-base)
    return {
        "x": nrm(ks[0], (BATCH, SEQ, D_MODEL), 1.0),
        "meta_tokens": nrm(ks[1], (N_META, D_MODEL), 1.0),
        "norm_g": 1.0 + nrm(ks[2], (DEPTH, 6, D_MODEL), 0.05),
        "ffn_w_in": nrm(ks[3], (DEPTH, 2, D_MODEL, 2 * D_FF), D_MODEL ** -0.5),
        "ffn_w_out": nrm(ks[4], (DEPTH, 2, D_FF, D_MODEL), D_FF ** -0.5),
        "w_in": nrm(ks[5], (DEPTH, D_MODEL, D_IN), D_MODEL ** -0.5),
        "w_out": nrm(ks[6], (DEPTH, D_MIX, D_MODEL), D_MIX ** -0.5),
        "fox_b_f": 4.0 + nrm(ks[7], (DEPTH, FOX_HEADS), 0.5),
        "conv_dw_w": nrm(ks[8], (DEPTH, CONV_WIDTH, GROUP_W), CONV_WIDTH ** -0.5),
        "conv_dw_b": nrm(ks[9], (DEPTH, GROUP_W), 0.02),
        "conv_ln_g": 1.0 + nrm(ks[10], (DEPTH, GROUP_W), 0.05),
        "conv_ln_b": nrm(ks[11], (DEPTH, GROUP_W), 0.02),
        "lru_conv_w": nrm(ks[12], (DEPTH, LRU_CONV_WIDTH, GROUP_W), LRU_CONV_WIDTH ** -0.5),
        "lru_conv_b": nrm(ks[13], (DEPTH, GROUP_W), 0.02),
        "lru_w_a": nrm(ks[14], (DEPTH, LRU_BLOCKS, GROUP_W // LRU_BLOCKS, GROUP_W // LRU_BLOCKS), (GROUP_W // LRU_BLOCKS) ** -0.5),
        "lru_b_a": nrm(ks[15], (DEPTH, GROUP_W), 0.02),
        "lru_w_i": nrm(ks[16], (DEPTH, LRU_BLOCKS, GROUP_W // LRU_BLOCKS, GROUP_W // LRU_BLOCKS), (GROUP_W // LRU_BLOCKS) ** -0.5),
        "lru_b_i": nrm(ks[17], (DEPTH, GROUP_W), 0.02),
        "lru_lambda": lam,
        "dsa_kv_norm_g": 1.0 + nrm(ks[18], (DEPTH, DSA_LATENT), 0.05),
        "dsa_w_uk": nrm(ks[19], (DEPTH, DSA_HEADS, DSA_LATENT, HEAD_DIM), DSA_LATENT ** -0.5),
        "dsa_w_uv": nrm(ks[21], (DEPTH, DSA_HEADS, DSA_LATENT, HEAD_DIM), DSA_LATENT ** -0.5),
        "idx_k_ln_g": 1.0 + nrm(ks[22], (DEPTH, IDX_DIM), 0.05),
        "idx_k_ln_b": nrm(ks[23], (DEPTH, IDX_DIM), 0.02),
    }


def reference(x, meta_tokens, norm_g, ffn_w_in, ffn_w_out, w_in, w_out, fox_b_f,
              conv_dw_w, conv_dw_b, conv_ln_g, conv_ln_b,
              lru_conv_w, lru_conv_b, lru_w_a, lru_b_a, lru_w_i, lru_b_i, lru_lambda,
              dsa_kv_norm_g, dsa_w_uk, dsa_w_uv, idx_k_ln_g, idx_k_ln_b):
    bsz, seq, d = x.shape
    topk = min(TOPK_MAX, seq // 4)
    split_at = np.cumsum(SPLIT_SIZES)[:-1].tolist()
    meta = jnp.broadcast_to(meta_tokens[None].astype(x.dtype), (bsz, N_META, d))
    h = jnp.concatenate([meta, x], axis=1)
    t = h.shape[1]
    heads = lambda a, n: a.reshape(bsz, t, n, -1)

    for l in range(DEPTH):
        g = norm_g[l]
        h = h + 0.5 * rms_norm(swiglu(rms_norm(h, g[0]), ffn_w_in[l, 0], ffn_w_out[l, 0]), g[1])

        z = rms_norm(h, g[2]) @ w_in[l]
        fq, fk, fv, ff, cu, lx, lg, dq, dkv, iq, ik, iw = jnp.split(z, split_at, axis=-1)
        y_fox = forgetting_attention(heads(fq, FOX_HEADS), heads(fk, FOX_HEADS),
                                     heads(fv, FOX_HEADS), ff, fox_b_f[l]).reshape(bsz, t, GROUP_W)
        y_conv = conformer_conv(cu, conv_dw_w[l], conv_dw_b[l], conv_ln_g[l], conv_ln_b[l])
        y_lru = rg_lru_branch(lx, lg, lru_conv_w[l], lru_conv_b[l], lru_w_a[l], lru_b_a[l],
                              lru_w_i[l], lru_b_i[l], lru_lambda[l])
        y_dsa = dsa_attention(heads(dq, DSA_HEADS), dkv, heads(iq, IDX_HEADS), ik, iw,
                              dsa_kv_norm_g[l], dsa_w_uk[l], dsa_w_uv[l],
                              idx_k_ln_g[l], idx_k_ln_b[l], topk).reshape(bsz, t, GROUP_W)
        mix = jnp.concatenate([y_fox, y_conv, y_lru, y_dsa], axis=-1) @ w_out[l]
        h = h + rms_norm(mix, g[3])

        h = h + 0.5 * rms_norm(swiglu(rms_norm(h, g[4]), ffn_w_in[l, 1], ffn_w_out[l, 1]), g[5])

    return h[:, N_META:]
```

```python
import functools

import jax
import jax.numpy as jnp
import numpy as np
from jax import lax
from jax.experimental import pallas as pl
from jax.experimental.pallas import tpu as pltpu

N_META = 16
BLOCK = 128
PAD = BLOCK - N_META
KEY_STEP = 512
GROUP_W = 256
HEAD_DIM = 64
FOX_HEADS = 4
CONV_WIDTH = 31
CONV_HALO = 32
LRU_BLOCKS = 4
LRU_CONV_WIDTH = 4
LRU_HALO = 8
LRU_C = 8.0
DSA_HEADS = 4
DSA_LATENT = 128
IDX_HEADS = 8
IDX_DIM = 32
TOPK_MAX = 256
RMS_EPS = 1e-6
LN_EPS = 1e-5
SPLIT_SIZES = (GROUP_W, GROUP_W, GROUP_W, FOX_HEADS, 2 * GROUP_W, GROUP_W, GROUP_W,
               DSA_HEADS * HEAD_DIM, DSA_LATENT, IDX_HEADS * IDX_DIM, IDX_DIM, IDX_HEADS)

MXU_DTYPE = jnp.bfloat16
F32 = jnp.float32
NEG = -1e30
SCORE_MASKED = -3e38
VMEM_LIMIT = 56 * 1024 * 1024

_INPROJ_GROUPS = (("fq", 256), ("fk", 256), ("fv", 256), ("ff", 128), ("cu", 512), ("lx", 256),
                  ("lg", 256), ("dq", 256), ("dkv", 128), ("iq", 256), ("ik", 256), ("iw", 128))
_INPROJ_OFF = {}
_o = 0
for _n, _w in _INPROJ_GROUPS:
    _INPROJ_OFF[_n] = (_o, _w)
    _o += _w
INPROJ_COLS = _o


def _params(*sem):
    return pltpu.CompilerParams(dimension_semantics=sem, vmem_limit_bytes=VMEM_LIMIT)


def _rms(x, g):
    return x * lax.rsqrt(jnp.mean(x * x, axis=-1, keepdims=True) + RMS_EPS) * g


def _dot(a, b):
    return jnp.dot(a, b, preferred_element_type=F32)


def _dot_nt(a, b):
    return lax.dot_general(a, b, (((1,), (1,)), ((), ())), preferred_element_type=F32)


def _log_sigmoid(x):
    return jnp.minimum(x, 0.0) - jnp.log1p(jnp.exp(-jnp.abs(x)))


def _expm1(y):
    e = jnp.exp(y)
    regular = (e != 1.0) & (e > 0.0)
    r = (e - 1.0) * y / jnp.log(jnp.where(regular, e, 2.0))
    return jnp.where(regular, r, jnp.where(e > 0.0, y, -1.0))


def _row_tile(rows):
    return 640 if rows % 640 == 0 else BLOCK


def _ffn_kernel(h_ref, gpre_ref, gpost_ref, win_ref, wout_ref, o_ref, *, d_ff, chunk):
    x = h_ref[...]
    xn = _rms(x, gpre_ref[...]).astype(MXU_DTYPE)
    acc = jnp.zeros(x.shape, F32)
    for c in range(d_ff // chunk):
        gate = _dot(xn, win_ref[:, c * chunk:(c + 1) * chunk])
        up = _dot(xn, win_ref[:, d_ff + c * chunk:d_ff + (c + 1) * chunk])
        a = (gate * jax.nn.sigmoid(gate) * up).astype(MXU_DTYPE)
        acc = acc + _dot(a, wout_ref[c * chunk:(c + 1) * chunk, :])
    o_ref[...] = x + 0.5 * _rms(acc, gpost_ref[...])


def _ffn(h, g_pre, g_post, w_in, w_out):
    rows, d = h.shape
    d_ff = w_out.shape[0]
    tm = _row_tile(rows)
    chunk = 512 if d_ff % 512 == 0 else d_ff
    const = lambda i: (0, 0)
    return pl.pallas_call(
        functools.partial(_ffn_kernel, d_ff=d_ff, chunk=chunk),
        grid=(rows // tm,),
        in_specs=[pl.BlockSpec((tm, d), lambda i: (i, 0)),
                  pl.BlockSpec((1, d), const), pl.BlockSpec((1, d), const),
                  pl.BlockSpec(w_in.shape, const, pipeline_mode=pl.Buffered(1)),
                  pl.BlockSpec(w_out.shape, const, pipeline_mode=pl.Buffered(1))],
        out_specs=pl.BlockSpec((tm, d), lambda i: (i, 0)),
        out_shape=jax.ShapeDtypeStruct((rows, d), F32),
        compiler_params=_params("parallel"),
        name="ffn",
    )(h, g_pre, g_post, w_in, w_out)


def _inproj_kernel(h_ref, g_ref, w_ref, wuk_ref, kvg_ref, lng_ref, lnb_ref,
                   fq_ref, fk_ref, fv_ref, ff_ref, cu_ref, lx_ref, lg_ref,
                   ql_ref, c_ref, iq_ref, ki_ref, wh_ref):
    xn = _rms(h_ref[...], g_ref[...]).astype(MXU_DTYPE)

    def proj(name):
        lo, n = _INPROJ_OFF[name]
        return _dot(xn, w_ref[:, lo:lo + n])

    fq_ref[...] = proj("fq").astype(MXU_DTYPE)
    fk_ref[...] = proj("fk").astype(MXU_DTYPE)
    fv_ref[...] = proj("fv").astype(MXU_DTYPE)
    ff_ref[...] = proj("ff")
    cu_ref[...] = proj("cu")
    lx_ref[...] = proj("lx")
    lg_ref[...] = proj("lg")
    ql_ref[...] = _dot(proj("dq").astype(MXU_DTYPE), wuk_ref[...]).astype(MXU_DTYPE)
    c_ref[...] = _rms(proj("dkv"), kvg_ref[...]).astype(MXU_DTYPE)
    iq_ref[...] = proj("iq").astype(MXU_DTYPE)
    ik = proj("ik")
    mu = jnp.mean(ik, axis=-1, keepdims=True)
    var = jnp.mean(jnp.square(ik - mu), axis=-1, keepdims=True)
    ki_ref[...] = ((ik - mu) * lax.rsqrt(var + LN_EPS) * lng_ref[...] + lnb_ref[...]).astype(MXU_DTYPE)
    wh_ref[...] = proj("iw") * (IDX_HEADS ** -0.5 * IDX_DIM ** -0.5)


def _inproj(h, g, w, wuk, kvg, lng, lnb):
    rows, d = h.shape
    tm = _row_tile(rows)
    const = lambda i: (0, 0)
    row = lambda i: (i, 0)
    outs = (("fq", 256, MXU_DTYPE), ("fk", 256, MXU_DTYPE), ("fv", 256, MXU_DTYPE), ("ff", 128, F32),
            ("cu", 512, F32), ("lx", 256, F32), ("lg", 256, F32), ("ql", 512, MXU_DTYPE),
            ("c", 128, MXU_DTYPE), ("iq", 256, MXU_DTYPE), ("ki", 256, MXU_DTYPE), ("wh", 128, F32))
    res = pl.pallas_call(
        _inproj_kernel,
        grid=(rows // tm,),
        in_specs=[pl.BlockSpec((tm, d), row), pl.BlockSpec((1, d), const),
                  pl.BlockSpec(w.shape, const, pipeline_mode=pl.Buffered(1)),
                  pl.BlockSpec(wuk.shape, const),
                  pl.BlockSpec(kvg.shape, const), pl.BlockSpec(lng.shape, const), pl.BlockSpec(lnb.shape, const)],
        out_specs=[pl.BlockSpec((tm, n), row) for _, n, _ in outs],
        out_shape=[jax.ShapeDtypeStruct((rows, n), dt) for _, n, dt in outs],
        compiler_params=_params("parallel"),
        name="in_proj",
    )(h, g, w, wuk, kvg, lng, lnb)
    return dict(zip([n for n, _, _ in outs], res))


def _shift_rows(x, s, fill, rows):
    return jnp.where(rows >= s, pltpu.roll(x, s, axis=0), fill)


def _seqmix_kernel(ff_ref, cu_ref, lx_ref, lg_ref, bf_ref, dww_ref, dwb_ref, lng_ref, lnb_ref,
                   lcw_ref, lcb_ref, wa_ref, ba_ref, wi_ref, bi_ref, lam_ref,
                   yc_ref, yl_ref, cum_ref, cumt_ref,
                   glu_buf, lx_buf, h_carry, cum_carry):
    t = pl.program_id(1)

    @pl.when(t == 0)
    def _():
        glu_buf[...] = jnp.zeros(glu_buf.shape, F32)
        lx_buf[...] = jnp.zeros(lx_buf.shape, F32)
        h_carry[...] = jnp.zeros(h_carry.shape, F32)
        cum_carry[...] = jnp.zeros(cum_carry.shape, F32)

    rows = lax.broadcasted_iota(jnp.int32, (BLOCK, 1), 0)
    valid = (t * BLOCK + rows) >= PAD

    cu = cu_ref[0]
    glu = jnp.where(valid, cu[:, :GROUP_W] * jax.nn.sigmoid(cu[:, GROUP_W:]), 0.0)
    glu_buf[CONV_HALO:, :] = glu
    acc = jnp.zeros((BLOCK, GROUP_W), F32) + dwb_ref[...]
    for k in range(CONV_WIDTH):
        lo = CONV_HALO - (CONV_WIDTH - 1) + k
        acc = acc + dww_ref[k:k + 1, :] * glu_buf[lo:lo + BLOCK, :]
    glu_buf[:CONV_HALO, :] = glu_buf[BLOCK:, :]
    mu = jnp.mean(acc, axis=-1, keepdims=True)
    var = jnp.mean(jnp.square(acc - mu), axis=-1, keepdims=True)
    hc = (acc - mu) * lax.rsqrt(var + LN_EPS) * lng_ref[...] + lnb_ref[...]
    yc_ref[0] = (hc * jax.nn.sigmoid(hc)).astype(yc_ref.dtype)

    lx_buf[LRU_HALO:, :] = jnp.where(valid, lx_ref[0], 0.0)
    xc = jnp.zeros((BLOCK, GROUP_W), F32) + lcb_ref[...]
    for k in range(LRU_CONV_WIDTH):
        lo = LRU_HALO - (LRU_CONV_WIDTH - 1) + k
        xc = xc + lcw_ref[k:k + 1, :] * lx_buf[lo:lo + BLOCK, :]
    lx_buf[:LRU_HALO, :] = lx_buf[BLOCK:, :]
    xcm = xc.astype(MXU_DTYPE)
    r = jax.nn.sigmoid(_dot(xcm, wa_ref[...]) + ba_ref[...])
    gi = jax.nn.sigmoid(_dot(xcm, wi_ref[...]) + bi_ref[...])
    log_a = LRU_C * r * _log_sigmoid(lam_ref[...])
    a = jnp.exp(log_a)
    u = jnp.where(valid, jnp.sqrt(-_expm1(2.0 * log_a)) * (gi * xc), 0.0)
    s = 1
    while s < BLOCK:
        u = a * _shift_rows(u, s, 0.0, rows) + u
        a = a * _shift_rows(a, s, 1.0, rows)
        s *= 2
    hl = u + a * h_carry[0:1, :]
    h_carry[...] = jnp.broadcast_to(hl[BLOCK - 1:BLOCK, :], h_carry.shape)
    g = lg_ref[0]
    gelu = 0.5 * g * (1.0 + jnp.tanh(np.sqrt(2.0 / np.pi).astype(np.float32) * (g + 0.044715 * g * g * g)))
    yl_ref[0] = (hl * gelu).astype(yl_ref.dtype)

    cs = _log_sigmoid(ff_ref[0] + bf_ref[...])
    s = 1
    while s < BLOCK:
        cs = cs + _shift_rows(cs, s, 0.0, rows)
        s *= 2
    cs = cs + cum_carry[0:1, :]
    cum_carry[...] = jnp.broadcast_to(cs[BLOCK - 1:BLOCK, :], cum_carry.shape)
    cum_ref[0] = cs
    key_valid = (t * BLOCK + lax.broadcasted_iota(jnp.int32, (1, BLOCK), 1)) >= PAD
    cumt_ref[0] = jnp.where(key_valid, cs.T[:8, :], -NEG)


def _seqmix(z, bsz, tp, p):
    nblk = tp // BLOCK
    blk = lambda n: pl.BlockSpec((1, BLOCK, n), lambda b, t: (b, t, 0))
    const = lambda a: pl.BlockSpec(a.shape, lambda b, t: (0, 0))
    r3 = lambda a: a.reshape(bsz, tp, a.shape[-1])
    params = (p["fox_b_f"], p["conv_dw_w"], p["conv_dw_b"], p["conv_ln_g"], p["conv_ln_b"],
              p["lru_conv_w"], p["lru_conv_b"], p["lru_w_a"], p["lru_b_a"], p["lru_w_i"], p["lru_b_i"],
              p["lru_lambda"])
    return pl.pallas_call(
        _seqmix_kernel,
        grid=(bsz, nblk),
        in_specs=[blk(128), blk(512), blk(256), blk(256)] + [const(a) for a in params],
        out_specs=[blk(256), blk(256), blk(128), pl.BlockSpec((1, 8, BLOCK), lambda b, t: (b, 0, t))],
        out_shape=[jax.ShapeDtypeStruct((bsz, tp, GROUP_W), MXU_DTYPE),
                   jax.ShapeDtypeStruct((bsz, tp, GROUP_W), MXU_DTYPE),
                   jax.ShapeDtypeStruct((bsz, tp, 128), F32),
                   jax.ShapeDtypeStruct((bsz, 8, tp), F32)],
        scratch_shapes=[pltpu.VMEM((CONV_HALO + BLOCK, GROUP_W), F32),
                        pltpu.VMEM((LRU_HALO + BLOCK, GROUP_W), F32),
                        pltpu.VMEM((8, GROUP_W), F32),
                        pltpu.VMEM((8, 128), F32)],
        compiler_params=_params("parallel", "arbitrary"),
        name="seq_mix",
    )(r3(z["ff"]), r3(z["cu"]), r3(z["lx"]), r3(z["lg"]), *params)


def _fox_kernel(q_ref, cq_ref, k_ref, v_ref, ckt_ref, o_ref):
    qi = pl.program_id(1)
    q = q_ref[0]
    cq_all = cq_ref[0]
    lane = lax.broadcasted_iota(jnp.int32, (1, BLOCK), 1)
    lane_kb = lax.broadcasted_iota(jnp.int32, (1, KEY_STEP), 1)
    q_pos = qi * BLOCK + lax.broadcasted_iota(jnp.int32, (BLOCK, 1), 0)
    qm, cq = [], []
    for h in range(FOX_HEADS):
        pair, hh = divmod(h, 2)
        qp = q[:, pair * BLOCK:(pair + 1) * BLOCK]
        in_head = (lane >= hh * HEAD_DIM) & (lane < (hh + 1) * HEAD_DIM)
        qm.append(jnp.where(in_head, qp, jnp.zeros_like(qp)))
        cq.append(cq_all[:, h:h + 1])

    def step(j, carry, causal_mask):
        ks = pl.multiple_of(j * KEY_STEP, KEY_STEP)
        cols = [slice((h // 2) * BLOCK, (h // 2 + 1) * BLOCK) for h in range(FOX_HEADS)]
        scores = []
        for h in range(FOX_HEADS):
            s = _dot_nt(qm[h], k_ref[0, pl.ds(ks, KEY_STEP), cols[h]])
            s = s + cq[h] - ckt_ref[0, h:h + 1, pl.ds(ks, KEY_STEP)]
            if causal_mask:
                s = jnp.where(ks + lane_kb <= q_pos, s, NEG)
            scores.append(s)
        probs = []
        for h in range(FOX_HEADS):
            m, l, _ = carry[h]
            m_new = jnp.maximum(m, jnp.max(scores[h], axis=-1, keepdims=True))
            alpha = jnp.exp(m - m_new)
            pr = jnp.exp(scores[h] - m_new)
            probs.append((m_new, alpha, alpha * l + jnp.sum(pr, axis=-1, keepdims=True), pr.astype(MXU_DTYPE)))
        out = []
        for h in range(FOX_HEADS):
            m_new, alpha, l, pr = probs[h]
            acc = alpha * carry[h][2] + _dot(pr, v_ref[0, pl.ds(ks, KEY_STEP), cols[h]])
            out.append((m_new, l, acc))
        return tuple(out)

    init = tuple((jnp.full((BLOCK, 1), NEG, F32), jnp.zeros((BLOCK, 1), F32), jnp.zeros((BLOCK, BLOCK), F32))
                 for _ in range(FOX_HEADS))
    n_full = (qi * BLOCK) // KEY_STEP
    carry = lax.fori_loop(0, n_full, functools.partial(step, causal_mask=False), init)
    carry = step(n_full, carry, True)
    res = [acc / l for _, l, acc in carry]
    outs = [jnp.where(lane < HEAD_DIM, res[2 * pair], res[2 * pair + 1]) for pair in range(FOX_HEADS // 2)]
    o_ref[0] = jnp.concatenate(outs, axis=1).astype(o_ref.dtype)


def _fox(fq, fk, fv, cum, cumt):
    bsz, tp, _ = fq.shape
    nblk = tp // BLOCK
    qblk = lambda n: pl.BlockSpec((1, BLOCK, n), lambda b, i: (b, i, 0))
    full = lambda a: pl.BlockSpec((1,) + a.shape[1:], lambda b, i: (b, 0, 0))
    return pl.pallas_call(
        _fox_kernel,
        grid=(bsz, nblk),
        in_specs=[qblk(GROUP_W), qblk(128), full(fk), full(fv), full(cumt)],
        out_specs=qblk(GROUP_W),
        out_shape=jax.ShapeDtypeStruct((bsz, tp, GROUP_W), MXU_DTYPE),
        compiler_params=_params("parallel", "arbitrary"),
        name="fox",
    )(fq, cum, fk, fv, cumt)


def _dsa_kernel(iq_ref, wh_ref, ql_ref, ki_ref, c_ref, wuv_ref, o_ref, sc_ref, *, topk):
    qi = pl.program_id(1)
    nkb = (qi * BLOCK) // KEY_STEP + 1
    lane_kb = lax.broadcasted_iota(jnp.int32, (1, KEY_STEP), 1)
    q_pos = qi * BLOCK + lax.broadcasted_iota(jnp.int32, (BLOCK, 1), 0)
    n_valid = q_pos - PAD + 1

    iq = iq_ref[0]
    wh = wh_ref[0]
    lane_i = lax.broadcasted_iota(jnp.int32, (1, IDX_HEADS * IDX_DIM), 1)
    q_heads = [jnp.where((lane_i >= h * IDX_DIM) & (lane_i < (h + 1) * IDX_DIM), iq, jnp.zeros_like(iq))
               for h in range(IDX_HEADS)]

    def fold(op, w):
        return op(op(w[:, :BLOCK], w[:, BLOCK:2 * BLOCK]), op(w[:, 2 * BLOCK:3 * BLOCK], w[:, 3 * BLOCK:]))

    def score_body(j, amax):
        ks = pl.multiple_of(j * KEY_STEP, KEY_STEP)
        kib = ki_ref[0, pl.ds(ks, KEY_STEP), :]
        dots = [_dot_nt(q_heads[h], kib) for h in range(IDX_HEADS)]
        sc = jnp.zeros((BLOCK, KEY_STEP), F32)
        for h in range(IDX_HEADS):
            sc = sc + jnp.maximum(dots[h], 0.0) * wh[:, h:h + 1]
        k_pos = ks + lane_kb
        valid = (k_pos <= q_pos) & (k_pos >= PAD)
        sc_ref[j] = jnp.where(valid, sc, SCORE_MASKED)
        return jnp.maximum(amax, fold(jnp.maximum, jnp.where(valid, jnp.abs(sc), 0.0)))

    amax = lax.fori_loop(0, nkb, score_body, jnp.zeros((BLOCK, BLOCK), F32))
    bound = jnp.max(amax, axis=-1, keepdims=True) * 1.0001 + 1e-30

    def count(pred):
        def body(j, acc):
            return acc + fold(jnp.add, jnp.where(pred(sc_ref[j]), 1, 0))
        acc = lax.fori_loop(0, nkb, body, jnp.zeros((BLOCK, BLOCK), jnp.int32))
        return jnp.sum(acc, axis=-1, keepdims=True).astype(F32)

    kf = float(topk)
    take_all = n_valid <= topk
    c_ge0 = count(lambda s: s >= 0.0)
    c_gt0 = count(lambda s: s > 0.0)
    positive = c_gt0 >= kf
    zero_tie = (c_ge0 >= kf) & jnp.logical_not(positive)
    lo = jnp.where(positive | zero_tie, 0.0, -bound)
    c_lo = jnp.where(positive | zero_tie, c_ge0, n_valid.astype(F32))
    hi = jnp.where(positive, bound, 0.0)
    c_hi = jnp.where(positive, 0.0, jnp.where(zero_tie, c_gt0, c_ge0))
    done = jnp.where(take_all | zero_tie | (c_lo == kf), 1, 0)

    def search_cond(state):
        return state[-1] > 0

    def search_body(state):
        lo, hi, c_lo, c_hi, done, _ = state
        mid = 0.5 * lo + 0.5 * hi
        inside = (mid > lo) & (mid < hi)
        c = count(lambda s: s >= mid)
        active = (done == 0) & inside
        up = active & (c >= kf)
        down = active & (c < kf)
        lo, c_lo = jnp.where(up, mid, lo), jnp.where(up, c, c_lo)
        hi, c_hi = jnp.where(down, mid, hi), jnp.where(down, c, c_hi)
        done = jnp.where((done > 0) | jnp.logical_not(inside) | (c_lo == kf), 1, 0)
        return lo, hi, c_lo, c_hi, done, jnp.sum(1 - done)

    thr, _, c_thr, c_above, _, _ = lax.while_loop(
        search_cond, search_body, (lo, hi, c_lo, c_hi, done, jnp.sum(1 - done)))
    thr = jnp.where(take_all, 0.5 * SCORE_MASKED, thr)
    tied = jnp.logical_not(take_all) & (c_thr > kf)
    any_tied = jnp.max(jnp.where(tied, 1, 0)) > 0

    ql = ql_ref[0]
    q_lat = [ql[:, h * DSA_LATENT:(h + 1) * DSA_LATENT] for h in range(DSA_HEADS)]

    def attend(select):
        def body(j, carry):
            heads, run = carry
            ks = pl.multiple_of(j * KEY_STEP, KEY_STEP)
            cb = c_ref[0, pl.ds(ks, KEY_STEP), :]
            sel, run = select(sc_ref[j], run)
            scores = [jnp.where(sel, _dot_nt(q_lat[h], cb), NEG) for h in range(DSA_HEADS)]
            probs = []
            for h in range(DSA_HEADS):
                m, l, _ = heads[h]
                m_new = jnp.maximum(m, jnp.max(scores[h], axis=-1, keepdims=True))
                alpha = jnp.exp(m - m_new)
                pr = jnp.exp(scores[h] - m_new)
                probs.append((m_new, alpha, alpha * l + jnp.sum(pr, axis=-1, keepdims=True), pr.astype(MXU_DTYPE)))
            out = tuple((m_new, l, alpha * heads[h][2] + _dot(pr, cb))
                        for h, (m_new, alpha, l, pr) in enumerate(probs))
            return out, run

        init = (tuple((jnp.full((BLOCK, 1), NEG, F32), jnp.zeros((BLOCK, 1), F32),
                       jnp.zeros((BLOCK, DSA_LATENT), F32)) for _ in range(DSA_HEADS)),
                jnp.zeros((BLOCK, 1), F32))
        heads, _ = lax.fori_loop(0, nkb, body, init)
        o_lat = jnp.concatenate([acc / jnp.where(l > 0.0, l, 1.0) for _, l, acc in heads], axis=1)
        o_ref[0] = _dot(o_lat.astype(MXU_DTYPE), wuv_ref[...]).astype(o_ref.dtype)

    @pl.when(jnp.logical_not(any_tied))
    def _():
        attend(lambda s, run: (s >= thr, run))

    @pl.when(any_tied)
    def _():
        quota = jnp.where(tied, kf - c_above, 2.0 ** 30)
        r_i = lax.broadcasted_iota(jnp.int32, (KEY_STEP, KEY_STEP), 0)
        c_i = lax.broadcasted_iota(jnp.int32, (KEY_STEP, KEY_STEP), 1)
        tri = jnp.where(r_i <= c_i, 1.0, 0.0).astype(MXU_DTYPE)

        def select(s, run):
            eqf = jnp.where(s == thr, 1.0, 0.0)
            prefix = _dot(eqf.astype(MXU_DTYPE), tri) + run
            within = jnp.where(prefix <= quota, eqf, 0.0)
            sel = (jnp.where(s > thr, 1.0, 0.0) + within) > 0.5
            return sel, run + jnp.sum(eqf, axis=-1, keepdims=True)

        attend(select)


def _dsa(iq, wh, ql, ki, c, wuv, topk):
    bsz, tp, _ = iq.shape
    nblk = tp // BLOCK
    qblk = lambda n: pl.BlockSpec((1, BLOCK, n), lambda b, i: (b, i, 0))
    full = lambda a: pl.BlockSpec((1,) + a.shape[1:], lambda b, i: (b, 0, 0))
    return pl.pallas_call(
        functools.partial(_dsa_kernel, topk=topk),
        grid=(bsz, nblk),
        in_specs=[qblk(256), qblk(128), qblk(512), full(ki), full(c), pl.BlockSpec(wuv.shape, lambda b, i: (0, 0))],
        out_specs=qblk(GROUP_W),
        out_shape=jax.ShapeDtypeStruct((bsz, tp, GROUP_W), MXU_DTYPE),
        scratch_shapes=[pltpu.VMEM((ki.shape[1] // KEY_STEP, BLOCK, KEY_STEP), F32)],
        compiler_params=_params("parallel", "arbitrary"),
        name="dsa",
    )(iq, wh, ql, ki, c, wuv)


def _outproj_kernel(h_ref, yf_ref, yc_ref, yl_ref, yd_ref, w_ref, g_ref, o_ref, *, tm, tp):
    mix = _dot(yf_ref[...], w_ref[0:GROUP_W, :])
    mix = mix + _dot(yc_ref[...], w_ref[GROUP_W:2 * GROUP_W, :])
    mix = mix + _dot(yl_ref[...], w_ref[2 * GROUP_W:3 * GROUP_W, :])
    mix = mix + _dot(yd_ref[...], w_ref[3 * GROUP_W:4 * GROUP_W, :])
    row = (pl.program_id(0) * tm) % tp + lax.broadcasted_iota(jnp.int32, (tm, 1), 0)
    o_ref[...] = jnp.where(row >= PAD, h_ref[...] + _rms(mix, g_ref[...]), 0.0)


def _outproj(h, yf, yc, yl, yd, w, g, tp):
    rows, d = h.shape
    tm = _row_tile(rows)
    row = lambda i: (i, 0)
    const = lambda i: (0, 0)
    return pl.pallas_call(
        functools.partial(_outproj_kernel, tm=tm, tp=tp),
        grid=(rows // tm,),
        in_specs=[pl.BlockSpec((tm, d), row)] + [pl.BlockSpec((tm, GROUP_W), row)] * 4
                 + [pl.BlockSpec(w.shape, const), pl.BlockSpec((1, d), const)],
        out_specs=pl.BlockSpec((tm, d), row),
        out_shape=jax.ShapeDtypeStruct((rows, d), F32),
        compiler_params=_params("parallel"),
        name="out_proj",
    )(h, yf, yc, yl, yd, w, g)


def _pack_w_in(w_in):
    offs = np.cumsum((0,) + SPLIT_SIZES)
    fq, fk, fv, ff, cu, lx, lg, dq, dkv, iq, ik, iw = (w_in[..., offs[i]:offs[i + 1]] for i in range(12))
    padc = lambda a, n: jnp.pad(a, ((0, 0), (0, 0), (0, n - a.shape[-1])))
    cols = [fq * HEAD_DIM ** -0.5, fk, fv, padc(ff, 128), cu, lx, lg, dq, dkv, iq,
            jnp.tile(ik, (1, 1, IDX_HEADS)), padc(iw, 128)]
    return jnp.concatenate(cols, axis=-1).astype(MXU_DTYPE)


def _block_diag(w):
    depth, n, a, b = w.shape
    eye = jnp.eye(n, dtype=w.dtype)
    return jnp.einsum("lnab,nm->lnamb", w, eye).reshape(depth, n * a, n * b)


def _lane_pad(a, n):
    return jnp.pad(a, [(0, 0)] * (a.ndim - 1) + [(0, n - a.shape[-1])])


def kernel(x, meta_tokens, norm_g, ffn_w_in, ffn_w_out, w_in, w_out, fox_b_f, conv_dw_w, conv_dw_b, conv_ln_g,
           conv_ln_b, lru_conv_w, lru_conv_b, lru_w_a, lru_b_a, lru_w_i, lru_b_i, lru_lambda, dsa_kv_norm_g,
           dsa_w_uk, dsa_w_uv, idx_k_ln_g, idx_k_ln_b):
    bsz, seq, d = x.shape
    depth = norm_g.shape[0]
    assert seq % BLOCK == 0 and d % 128 == 0
    topk = min(TOPK_MAX, seq // 4)
    tp = PAD + N_META + seq
    rows = bsz * tp

    ffn_w_in_m = ffn_w_in.astype(MXU_DTYPE)
    ffn_w_out_m = ffn_w_out.astype(MXU_DTYPE)
    w_in_m = _pack_w_in(w_in)
    w_out_m = w_out.astype(MXU_DTYPE)
    wuk_m = (_block_diag(dsa_w_uk.transpose(0, 1, 3, 2)) * HEAD_DIM ** -0.5).astype(MXU_DTYPE)
    wuv_m = _block_diag(dsa_w_uv).astype(MXU_DTYPE)
    wa_m = _block_diag(lru_w_a).astype(MXU_DTYPE)
    wi_m = _block_diag(lru_w_i).astype(MXU_DTYPE)
    row2 = lambda a: a[:, None, :]
    dww = jnp.pad(conv_dw_w, ((0, 0), (0, CONV_HALO - CONV_WIDTH), (0, 0)))
    lcw = jnp.pad(lru_conv_w, ((0, 0), (0, LRU_HALO - LRU_CONV_WIDTH), (0, 0)))
    ln_g8 = row2(jnp.tile(idx_k_ln_g, (1, IDX_HEADS)))
    ln_b8 = row2(jnp.tile(idx_k_ln_b, (1, IDX_HEADS)))

    meta = jnp.broadcast_to(meta_tokens[None].astype(x.dtype), (bsz, N_META, d))
    h = jnp.concatenate([jnp.zeros((bsz, PAD, d), x.dtype), meta, x], axis=1).reshape(rows, d)

    for l in range(depth):
        g = norm_g[l][:, None, :]
        h = _ffn(h, g[0], g[1], ffn_w_in_m[l, 0], ffn_w_out_m[l, 0])
        z = _inproj(h, g[2], w_in_m[l], wuk_m[l], row2(dsa_kv_norm_g)[l], ln_g8[l], ln_b8[l])
        seq_params = {
            "fox_b_f": _lane_pad(fox_b_f[l][None], 128), "conv_dw_w": dww[l], "conv_dw_b": row2(conv_dw_b)[l],
            "conv_ln_g": row2(conv_ln_g)[l], "conv_ln_b": row2(conv_ln_b)[l], "lru_conv_w": lcw[l],
            "lru_conv_b": row2(lru_conv_b)[l], "lru_w_a": wa_m[l], "lru_b_a": row2(lru_b_a)[l],
            "lru_w_i": wi_m[l], "lru_b_i": row2(lru_b_i)[l], "lru_lambda": row2(lru_lambda)[l]}
        y_conv, y_lru, cum, cumt = _seqmix(z, bsz, tp, seq_params)
        r3 = lambda a: a.reshape(bsz, tp, a.shape[-1])
        tpk = -(-tp // KEY_STEP) * KEY_STEP
        keys = lambda a: jnp.pad(r3(a), ((0, 0), (0, tpk - tp), (0, 0)))
        cumt = jnp.pad(cumt, ((0, 0), (0, 0), (0, tpk - tp)))
        y_fox = _fox(r3(z["fq"]), keys(z["fk"]), keys(z["fv"]), cum, cumt)
        y_dsa = _dsa(r3(z["iq"]), r3(z["wh"]), r3(z["ql"]), keys(z["ki"]), keys(z["c"]), wuv_m[l], topk)
        r2 = lambda a: a.reshape(rows, a.shape[-1])
        h = _outproj(h, r2(y_fox), r2(y_conv), r2(y_lru), r2(y_dsa), w_out_m[l], g[3], tp)
        h = _ffn(h, g[4], g[5], ffn_w_in_m[l, 1], ffn_w_out_m[l, 1])

    return h.reshape(bsz, tp, d)[:, PAD + N_META:]
```

```python
import functools

import jax
import jax.numpy as jnp
import numpy as np
from jax import lax
from jax.experimental import pallas as pl
from jax.experimental.pallas import tpu as pltpu

N_META = 16
BLOCK = 128
PAD = BLOCK - N_META
KEY_STEP = 512
GROUP_W = 256
HEAD_DIM = 64
FOX_HEADS = 4
CONV_WIDTH = 31
CONV_HALO = 32
LRU_BLOCKS = 4
LRU_CONV_WIDTH = 4
LRU_HALO = 8
LRU_C = 8.0
DSA_HEADS = 4
DSA_LATENT = 128
IDX_HEADS = 8
IDX_DIM = 32
TOPK_MAX = 256
RMS_EPS = 1e-6
LN_EPS = 1e-5
SPLIT_SIZES = (GROUP_W, GROUP_W, GROUP_W, FOX_HEADS, 2 * GROUP_W, GROUP_W, GROUP_W,
               DSA_HEADS * HEAD_DIM, DSA_LATENT, IDX_HEADS * IDX_DIM, IDX_DIM, IDX_HEADS)

MXU_DTYPE = jnp.bfloat16
F32 = jnp.float32
NEG = -1e30
SCORE_MASKED = -3e38
VMEM_LIMIT = 56 * 1024 * 1024

_INPROJ_GROUPS = (("fq", 256), ("fk", 256), ("fv", 256), ("ff", 128), ("cu", 512), ("lx", 256),
                  ("lg", 256), ("dq", 256), ("dkv", 128), ("iq", 256), ("ik", 256), ("iw", 128))
_INPROJ_OFF = {}
_o = 0
for _n, _w in _INPROJ_GROUPS:
    _INPROJ_OFF[_n] = (_o, _w)
    _o += _w
INPROJ_COLS = _o


def _params(*sem):
    return pltpu.CompilerParams(dimension_semantics=sem, vmem_limit_bytes=VMEM_LIMIT)


def _rms(x, g):
    return x * lax.rsqrt(jnp.mean(x * x, axis=-1, keepdims=True) + RMS_EPS) * g


def _dot(a, b):
    return jnp.dot(a, b, preferred_element_type=F32)


def _dot_nt(a, b):
    return lax.dot_general(a, b, (((1,), (1,)), ((), ())), preferred_element_type=F32)


def _log_sigmoid(x):
    return jnp.minimum(x, 0.0) - jnp.log1p(jnp.exp(-jnp.abs(x)))


def _expm1(y):
    e = jnp.exp(y)
    regular = (e != 1.0) & (e > 0.0)
    r = (e - 1.0) * y / jnp.log(jnp.where(regular, e, 2.0))
    return jnp.where(regular, r, jnp.where(e > 0.0, y, -1.0))


def _row_tile(rows):
    return 640 if rows % 640 == 0 else BLOCK


def _ffn_kernel(h_ref, gpre_ref, gpost_ref, win_ref, wout_ref, o_ref, *, d_ff, chunk):
    x = h_ref[...]
    xn = _rms(x, gpre_ref[...]).astype(MXU_DTYPE)
    acc = jnp.zeros(x.shape, F32)
    for c in range(d_ff // chunk):
        gate = _dot(xn, win_ref[:, c * chunk:(c + 1) * chunk])
        up = _dot(xn, win_ref[:, d_ff + c * chunk:d_ff + (c + 1) * chunk])
        a = (gate * jax.nn.sigmoid(gate) * up).astype(MXU_DTYPE)
        acc = acc + _dot(a, wout_ref[c * chunk:(c + 1) * chunk, :])
    o_ref[...] = x + 0.5 * _rms(acc, gpost_ref[...])


def _ffn(h, g_pre, g_post, w_in, w_out):
    rows, d = h.shape
    d_ff = w_out.shape[0]
    tm = _row_tile(rows)
    chunk = 512 if d_ff % 512 == 0 else d_ff
    const = lambda i: (0, 0)
    return pl.pallas_call(
        functools.partial(_ffn_kernel, d_ff=d_ff, chunk=chunk),
        grid=(rows // tm,),
        in_specs=[pl.BlockSpec((tm, d), lambda i: (i, 0)),
                  pl.BlockSpec((1, d), const), pl.BlockSpec((1, d), const),
                  pl.BlockSpec(w_in.shape, const, pipeline_mode=pl.Buffered(1)),
                  pl.BlockSpec(w_out.shape, const, pipeline_mode=pl.Buffered(1))],
        out_specs=pl.BlockSpec((tm, d), lambda i: (i, 0)),
        out_shape=jax.ShapeDtypeStruct((rows, d), F32),
        compiler_params=_params("parallel"),
        name="ffn",
    )(h, g_pre, g_post, w_in, w_out)


def _inproj_kernel(h_ref, g_ref, w_ref, wuk_ref, kvg_ref, lng_ref, lnb_ref,
                   fq_ref, fk_ref, fv_ref, ff_ref, cu_ref, lx_ref, lg_ref,
                   ql_ref, c_ref, iq_ref, ki_ref, wh_ref):
    xn = _rms(h_ref[...], g_ref[...]).astype(MXU_DTYPE)

    def proj(name):
        lo, n = _INPROJ_OFF[name]
        return _dot(xn, w_ref[:, lo:lo + n])

    fq_ref[...] = proj("fq").astype(MXU_DTYPE)
    fk_ref[...] = proj("fk").astype(MXU_DTYPE)
    fv_ref[...] = proj("fv").astype(MXU_DTYPE)
    ff_ref[...] = proj("ff")
    cu_ref[...] = proj("cu")
    lx_ref[...] = proj("lx")
    lg_ref[...] = proj("lg")
    ql_ref[...] = _dot(proj("dq").astype(MXU_DTYPE), wuk_ref[...]).astype(MXU_DTYPE)
    c_ref[...] = _rms(proj("dkv"), kvg_ref[...]).astype(MXU_DTYPE)
    iq_ref[...] = proj("iq").astype(MXU_DTYPE)
    ik = proj("ik")
    mu = jnp.mean(ik, axis=-1, keepdims=True)
    var = jnp.mean(jnp.square(ik - mu), axis=-1, keepdims=True)
    ki_ref[...] = ((ik - mu) * lax.rsqrt(var + LN_EPS) * lng_ref[...] + lnb_ref[...]).astype(MXU_DTYPE)
    wh_ref[...] = proj("iw") * (IDX_HEADS ** -0.5 * IDX_DIM ** -0.5)


def _inproj(h, g, w, wuk, kvg, lng, lnb):
    rows, d = h.shape
    tm = _row_tile(rows)
    const = lambda i: (0, 0)
    row = lambda i: (i, 0)
    outs = (("fq", 256, MXU_DTYPE), ("fk", 256, MXU_DTYPE), ("fv", 256, MXU_DTYPE), ("ff", 128, F32),
            ("cu", 512, F32), ("lx", 256, F32), ("lg", 256, F32), ("ql", 512, MXU_DTYPE),
            ("c", 128, MXU_DTYPE), ("iq", 256, MXU_DTYPE), ("ki", 256, MXU_DTYPE), ("wh", 128, F32))
    res = pl.pallas_call(
        _inproj_kernel,
        grid=(rows // tm,),
        in_specs=[pl.BlockSpec((tm, d), row), pl.BlockSpec((1, d), const),
                  pl.BlockSpec(w.shape, const, pipeline_mode=pl.Buffered(1)),
                  pl.BlockSpec(wuk.shape, const),
                  pl.BlockSpec(kvg.shape, const), pl.BlockSpec(lng.shape, const), pl.BlockSpec(lnb.shape, const)],
        out_specs=[pl.BlockSpec((tm, n), row) for _, n, _ in outs],
        out_shape=[jax.ShapeDtypeStruct((rows, n), dt) for _, n, dt in outs],
        compiler_params=_params("parallel"),
        name="in_proj",
    )(h, g, w, wuk, kvg, lng, lnb)
    return dict(zip([n for n, _, _ in outs], res))


def _shift_rows(x, s, fill, rows):
    return jnp.where(rows >= s, pltpu.roll(x, s, axis=0), fill)


def _seqmix_kernel(ff_ref, cu_ref, lx_ref, lg_ref, bf_ref, dww_ref, dwb_ref, lng_ref, lnb_ref,
                   lcw_ref, lcb_ref, wa_ref, ba_ref, wi_ref, bi_ref, lam_ref,
                   yc_ref, yl_ref, cum_ref, cumt_ref,
                   glu_buf, lx_buf, h_carry, cum_carry):
    t = pl.program_id(1)

    @pl.when(t == 0)
    def _():
        glu_buf[...] = jnp.zeros(glu_buf.shape, F32)
        lx_buf[...] = jnp.zeros(lx_buf.shape, F32)
        h_carry[...] = jnp.zeros(h_carry.shape, F32)
        cum_carry[...] = jnp.zeros(cum_carry.shape, F32)

    rows = lax.broadcasted_iota(jnp.int32, (BLOCK, 1), 0)
    valid = (t * BLOCK + rows) >= PAD

    cu = cu_ref[0]
    glu = jnp.where(valid, cu[:, :GROUP_W] * jax.nn.sigmoid(cu[:, GROUP_W:]), 0.0)
    glu_buf[CONV_HALO:, :] = glu
    acc = jnp.zeros((BLOCK, GROUP_W), F32) + dwb_ref[...]
    for k in range(CONV_WIDTH):
        lo = CONV_HALO - (CONV_WIDTH - 1) + k
        acc = acc + dww_ref[k:k + 1, :] * glu_buf[lo:lo + BLOCK, :]
    glu_buf[:CONV_HALO, :] = glu_buf[BLOCK:, :]
    mu = jnp.mean(acc, axis=-1, keepdims=True)
    var = jnp.mean(jnp.square(acc - mu), axis=-1, keepdims=True)
    hc = (acc - mu) * lax.rsqrt(var + LN_EPS) * lng_ref[...] + lnb_ref[...]
    yc_ref[0] = (hc * jax.nn.sigmoid(hc)).astype(yc_ref.dtype)

    lx_buf[LRU_HALO:, :] = jnp.where(valid, lx_ref[0], 0.0)
    xc = jnp.zeros((BLOCK, GROUP_W), F32) + lcb_ref[...]
    for k in range(LRU_CONV_WIDTH):
        lo = LRU_HALO - (LRU_CONV_WIDTH - 1) + k
        xc = xc + lcw_ref[k:k + 1, :] * lx_buf[lo:lo + BLOCK, :]
    lx_buf[:LRU_HALO, :] = lx_buf[BLOCK:, :]
    xcm = xc.astype(MXU_DTYPE)
    r = jax.nn.sigmoid(_dot(xcm, wa_ref[...]) + ba_ref[...])
    gi = jax.nn.sigmoid(_dot(xcm, wi_ref[...]) + bi_ref[...])
    log_a = LRU_C * r * _log_sigmoid(lam_ref[...])
    a = jnp.exp(log_a)
    u = jnp.where(valid, jnp.sqrt(-_expm1(2.0 * log_a)) * (gi * xc), 0.0)
    s = 1
    while s < BLOCK:
        u = a * _shift_rows(u, s, 0.0, rows) + u
        a = a * _shift_rows(a, s, 1.0, rows)
        s *= 2
    hl = u + a * h_carry[0:1, :]
    h_carry[...] = jnp.broadcast_to(hl[BLOCK - 1:BLOCK, :], h_carry.shape)
    g = lg_ref[0]
    gelu = 0.5 * g * (1.0 + jnp.tanh(np.sqrt(2.0 / np.pi).astype(np.float32) * (g + 0.044715 * g * g * g)))
    yl_ref[0] = (hl * gelu).astype(yl_ref.dtype)

    cs = _log_sigmoid(ff_ref[0] + bf_ref[...])
    s = 1
    while s < BLOCK:
        cs = cs + _shift_rows(cs, s, 0.0, rows)
        s *= 2
    cs = cs + cum_carry[0:1, :]
    cum_carry[...] = jnp.broadcast_to(cs[BLOCK - 1:BLOCK, :], cum_carry.shape)
    cum_ref[0] = cs
    key_valid = (t * BLOCK + lax.broadcasted_iota(jnp.int32, (1, BLOCK), 1)) >= PAD
    cumt_ref[0] = jnp.where(key_valid, cs.T[:8, :], -NEG)


def _seqmix(z, bsz, tp, p):
    nblk = tp // BLOCK
    blk = lambda n: pl.BlockSpec((1, BLOCK, n), lambda b, t: (b, t, 0))
    const = lambda a: pl.BlockSpec(a.shape, lambda b, t: (0, 0))
    r3 = lambda a: a.reshape(bsz, tp, a.shape[-1])
    params = (p["fox_b_f"], p["conv_dw_w"], p["conv_dw_b"], p["conv_ln_g"], p["conv_ln_b"],
              p["lru_conv_w"], p["lru_conv_b"], p["lru_w_a"], p["lru_b_a"], p["lru_w_i"], p["lru_b_i"],
              p["lru_lambda"])
    return pl.pallas_call(
        _seqmix_kernel,
        grid=(bsz, nblk),
        in_specs=[blk(128), blk(512), blk(256), blk(256)] + [const(a) for a in params],
        out_specs=[blk(256), blk(256), blk(128), pl.BlockSpec((1, 8, BLOCK), lambda b, t: (b, 0, t))],
        out_shape=[jax.ShapeDtypeStruct((bsz, tp, GROUP_W), MXU_DTYPE),
                   jax.ShapeDtypeStruct((bsz, tp, GROUP_W), MXU_DTYPE),
                   jax.ShapeDtypeStruct((bsz, tp, 128), F32),
                   jax.ShapeDtypeStruct((bsz, 8, tp), F32)],
        scratch_shapes=[pltpu.VMEM((CONV_HALO + BLOCK, GROUP_W), F32),
                        pltpu.VMEM((LRU_HALO + BLOCK, GROUP_W), F32),
                        pltpu.VMEM((8, GROUP_W), F32),
                        pltpu.VMEM((8, 128), F32)],
        compiler_params=_params("parallel", "arbitrary"),
        name="seq_mix",
    )(r3(z["ff"]), r3(z["cu"]), r3(z["lx"]), r3(z["lg"]), *params)


def _fox_kernel(q_ref, cq_ref, k_ref, v_ref, ckt_ref, o_ref):
    qi = pl.program_id(1)
    q = q_ref[0]
    cq_all = cq_ref[0]
    lane = lax.broadcasted_iota(jnp.int32, (1, BLOCK), 1)
    lane_kb = lax.broadcasted_iota(jnp.int32, (1, KEY_STEP), 1)
    q_pos = qi * BLOCK + lax.broadcasted_iota(jnp.int32, (BLOCK, 1), 0)
    qm, cq = [], []
    for h in range(FOX_HEADS):
        pair, hh = divmod(h, 2)
        qp = q[:, pair * BLOCK:(pair + 1) * BLOCK]
        in_head = (lane >= hh * HEAD_DIM) & (lane < (hh + 1) * HEAD_DIM)
        qm.append(jnp.where(in_head, qp, jnp.zeros_like(qp)))
        cq.append(cq_all[:, h:h + 1])
    qm_pair = [jnp.concatenate(qm[2 * p:2 * p + 2], axis=0) for p in range(FOX_HEADS // 2)]

    cols = [slice((h // 2) * BLOCK, (h // 2 + 1) * BLOCK) for h in range(FOX_HEADS)]

    def qk(j):
        ks = pl.multiple_of(j * KEY_STEP, KEY_STEP)
        return tuple(_dot_nt(qm_pair[p], k_ref[0, pl.ds(ks, KEY_STEP), cols[2 * p]]) for p in range(FOX_HEADS // 2))

    def update(j, pair_scores, carry, causal_mask):
        ks = pl.multiple_of(j * KEY_STEP, KEY_STEP)
        stats, probs = [], []
        for h in range(FOX_HEADS):
            m, l, _ = carry[h]
            s = pair_scores[h // 2][(h % 2) * BLOCK:(h % 2 + 1) * BLOCK]
            s = s + cq[h] - ckt_ref[0, h:h + 1, pl.ds(ks, KEY_STEP)]
            if causal_mask:
                s = jnp.where(ks + lane_kb <= q_pos, s, NEG)
            m_new = jnp.maximum(m, jnp.max(s, axis=-1, keepdims=True))
            alpha = jnp.exp(m - m_new)
            pr = jnp.exp(s - m_new)
            stats.append((m_new, alpha, alpha * l + jnp.sum(pr, axis=-1, keepdims=True)))
            probs.append(pr.astype(MXU_DTYPE))
        out = []
        for p in range(FOX_HEADS // 2):
            pv = _dot(jnp.concatenate(probs[2 * p:2 * p + 2], axis=0), v_ref[0, pl.ds(ks, KEY_STEP), cols[2 * p]])
            for h in (2 * p, 2 * p + 1):
                m_new, alpha, l = stats[h]
                out.append((m_new, l, alpha * carry[h][2] + pv[(h % 2) * BLOCK:(h % 2 + 1) * BLOCK]))
        return tuple(out)

    init = tuple((jnp.full((BLOCK, 1), NEG, F32), jnp.zeros((BLOCK, 1), F32), jnp.zeros((BLOCK, BLOCK), F32))
                 for _ in range(FOX_HEADS))
    n_full = (qi * BLOCK) // KEY_STEP

    carry = lax.fori_loop(0, n_full, lambda j, carry: update(j, qk(j), carry, False), init)
    carry = update(n_full, qk(n_full), carry, True)
    res = [acc / l for _, l, acc in carry]
    outs = [jnp.where(lane < HEAD_DIM, res[2 * pair], res[2 * pair + 1]) for pair in range(FOX_HEADS // 2)]
    o_ref[0] = jnp.concatenate(outs, axis=1).astype(o_ref.dtype)


def _fox(fq, fk, fv, cum, cumt):
    bsz, tp, _ = fq.shape
    nblk = tp // BLOCK
    qblk = lambda n: pl.BlockSpec((1, BLOCK, n), lambda b, i: (b, i, 0))
    full = lambda a: pl.BlockSpec((1,) + a.shape[1:], lambda b, i: (b, 0, 0))
    return pl.pallas_call(
        _fox_kernel,
        grid=(bsz, nblk),
        in_specs=[qblk(GROUP_W), qblk(128), full(fk), full(fv), full(cumt)],
        out_specs=qblk(GROUP_W),
        out_shape=jax.ShapeDtypeStruct((bsz, tp, GROUP_W), MXU_DTYPE),
        compiler_params=_params("parallel", "arbitrary"),
        name="fox",
    )(fq, cum, fk, fv, cumt)


def _dsa_kernel(iq_ref, wh_ref, ql_ref, ki_ref, c_ref, wuv_ref, o_ref, sc_ref, *, topk):
    qi = pl.program_id(1)
    nkb = (qi * BLOCK) // KEY_STEP + 1
    lane_kb = lax.broadcasted_iota(jnp.int32, (1, KEY_STEP), 1)
    q_pos = qi * BLOCK + lax.broadcasted_iota(jnp.int32, (BLOCK, 1), 0)
    n_valid = q_pos - PAD + 1

    iq = iq_ref[0]
    wh = wh_ref[0]
    lane_i = lax.broadcasted_iota(jnp.int32, (1, IDX_HEADS * IDX_DIM), 1)
    q_heads = jnp.concatenate(
        [jnp.where((lane_i >= h * IDX_DIM) & (lane_i < (h + 1) * IDX_DIM), iq, jnp.zeros_like(iq))
         for h in range(IDX_HEADS)], axis=0)

    def fold(op, w):
        return op(op(w[:, :BLOCK], w[:, BLOCK:2 * BLOCK]), op(w[:, 2 * BLOCK:3 * BLOCK], w[:, 3 * BLOCK:]))

    def score_body(j, amax):
        ks = pl.multiple_of(j * KEY_STEP, KEY_STEP)
        kib = ki_ref[0, pl.ds(ks, KEY_STEP), :]
        dots = _dot_nt(q_heads, kib)
        sc = jnp.zeros((BLOCK, KEY_STEP), F32)
        for h in range(IDX_HEADS):
            sc = sc + jnp.maximum(dots[h * BLOCK:(h + 1) * BLOCK], 0.0) * wh[:, h:h + 1]
        k_pos = ks + lane_kb
        valid = (k_pos <= q_pos) & (k_pos >= PAD)
        sc_ref[j] = jnp.where(valid, sc, SCORE_MASKED)
        return jnp.maximum(amax, fold(jnp.maximum, jnp.where(valid, jnp.abs(sc), 0.0)))

    amax = lax.fori_loop(0, nkb, score_body, jnp.zeros((BLOCK, BLOCK), F32))
    bound = jnp.max(amax, axis=-1, keepdims=True) * 1.0001 + 1e-30

    @pl.when(nkb % 2 == 1)
    def _():
        sc_ref[nkb] = jnp.full((BLOCK, KEY_STEP), SCORE_MASKED, F32)

    def count(*preds):
        def body(jj, accs):
            tiles = (sc_ref[2 * jj], sc_ref[2 * jj + 1])
            return tuple(tuple(acc + fold(jnp.add, jnp.where(pred(t), 1, 0)) for acc, t in zip(pair, tiles))
                         for pair, pred in zip(accs, preds))
        zeros = jnp.zeros((BLOCK, BLOCK), jnp.int32)
        accs = lax.fori_loop(0, (nkb + 1) // 2, body, tuple((zeros, zeros) for _ in preds))
        return [jnp.sum(a + b, axis=-1, keepdims=True).astype(F32) for a, b in accs]

    kf = float(topk)
    take_all = n_valid <= topk
    c_ge0, c_gt0 = count(lambda s: s >= 0.0, lambda s: s > 0.0)
    positive = c_gt0 >= kf
    zero_tie = (c_ge0 >= kf) & jnp.logical_not(positive)
    lo = jnp.where(positive | zero_tie, 0.0, -bound)
    c_lo = jnp.where(positive | zero_tie, c_ge0, n_valid.astype(F32))
    hi = jnp.where(positive, bound, 0.0)
    c_hi = jnp.where(positive, 0.0, jnp.where(zero_tie, c_gt0, c_ge0))
    done = jnp.where(take_all | zero_tie | (c_lo == kf), 1, 0)

    def search_cond(state):
        return state[-1] > 0

    def search_body(state):
        lo, hi, c_lo, c_hi, done, it, _ = state
        mid = 0.5 * lo + 0.5 * hi
        inside = (mid > lo) & (mid < hi)
        log_lo = jnp.log(c_lo)
        frac = (log_lo - np.log(kf)) / (log_lo - jnp.log(jnp.maximum(c_hi, 0.5)))
        probe = lo + (hi - lo) * jnp.clip(frac, 0.02, 0.98)
        use_probe = ((jnp.zeros_like(done) + it % 3) != 2) & (probe > lo) & (probe < hi)
        probe = jnp.where(use_probe, probe, mid)
        c, = count(lambda s: s >= probe)
        active = (done == 0) & inside
        up = active & (c >= kf)
        down = active & (c < kf)
        lo, c_lo = jnp.where(up, probe, lo), jnp.where(up, c, c_lo)
        hi, c_hi = jnp.where(down, probe, hi), jnp.where(down, c, c_hi)
        done = jnp.where((done > 0) | jnp.logical_not(inside) | (c_lo == kf), 1, 0)
        return lo, hi, c_lo, c_hi, done, it + 1, jnp.sum(1 - done)

    thr, _, c_thr, c_above, _, _, _ = lax.while_loop(
        search_cond, search_body, (lo, hi, c_lo, c_hi, done, jnp.int32(0), jnp.sum(1 - done)))
    thr = jnp.where(take_all, 0.5 * SCORE_MASKED, thr)
    tied = jnp.logical_not(take_all) & (c_thr > kf)
    any_tied = jnp.max(jnp.where(tied, 1, 0)) > 0

    ql = ql_ref[0]
    q_stack = jnp.concatenate([ql[:, h * DSA_LATENT:(h + 1) * DSA_LATENT] for h in range(DSA_HEADS)], axis=0)

    def attend(select):
        def body(j, carry):
            heads, run = carry
            ks = pl.multiple_of(j * KEY_STEP, KEY_STEP)
            cb = c_ref[0, pl.ds(ks, KEY_STEP), :]
            sel, run = select(sc_ref[j], run)
            s_all = _dot_nt(q_stack, cb)
            stats, probs = [], []
            for h in range(DSA_HEADS):
                m, l, _ = heads[h]
                s = jnp.where(sel, s_all[h * BLOCK:(h + 1) * BLOCK], NEG)
                m_new = jnp.maximum(m, jnp.max(s, axis=-1, keepdims=True))
                alpha = jnp.exp(m - m_new)
                pr = jnp.exp(s - m_new)
                stats.append((m_new, alpha, alpha * l + jnp.sum(pr, axis=-1, keepdims=True)))
                probs.append(pr.astype(MXU_DTYPE))
            pv = _dot(jnp.concatenate(probs, axis=0), cb)
            out = tuple((m_new, l, alpha * heads[h][2] + pv[h * BLOCK:(h + 1) * BLOCK])
                        for h, (m_new, alpha, l) in enumerate(stats))
            return out, run

        init = (tuple((jnp.full((BLOCK, 1), NEG, F32), jnp.zeros((BLOCK, 1), F32),
                       jnp.zeros((BLOCK, DSA_LATENT), F32)) for _ in range(DSA_HEADS)),
                jnp.zeros((BLOCK, 1), F32))
        heads, _ = lax.fori_loop(0, nkb, body, init)
        o_lat = jnp.concatenate([acc / jnp.where(l > 0.0, l, 1.0) for _, l, acc in heads], axis=1)
        o_ref[0] = _dot(o_lat.astype(MXU_DTYPE), wuv_ref[...]).astype(o_ref.dtype)

    @pl.when(jnp.logical_not(any_tied))
    def _():
        attend(lambda s, run: (s >= thr, run))

    @pl.when(any_tied)
    def _():
        quota = jnp.where(tied, kf - c_above, 2.0 ** 30)
        r_i = lax.broadcasted_iota(jnp.int32, (KEY_STEP, KEY_STEP), 0)
        c_i = lax.broadcasted_iota(jnp.int32, (KEY_STEP, KEY_STEP), 1)
        tri = jnp.where(r_i <= c_i, 1.0, 0.0).astype(MXU_DTYPE)

        def select(s, run):
            eqf = jnp.where(s == thr, 1.0, 0.0)
            prefix = _dot(eqf.astype(MXU_DTYPE), tri) + run
            within = jnp.where(prefix <= quota, eqf, 0.0)
            sel = (jnp.where(s > thr, 1.0, 0.0) + within) > 0.5
            return sel, run + jnp.sum(eqf, axis=-1, keepdims=True)

        attend(select)


def _dsa(iq, wh, ql, ki, c, wuv, topk):
    bsz, tp, _ = iq.shape
    nblk = tp // BLOCK
    qblk = lambda n: pl.BlockSpec((1, BLOCK, n), lambda b, i: (b, i, 0))
    full = lambda a: pl.BlockSpec((1,) + a.shape[1:], lambda b, i: (b, 0, 0))
    return pl.pallas_call(
        functools.partial(_dsa_kernel, topk=topk),
        grid=(bsz, nblk),
        in_specs=[qblk(256), qblk(128), qblk(512), full(ki), full(c), pl.BlockSpec(wuv.shape, lambda b, i: (0, 0))],
        out_specs=qblk(GROUP_W),
        out_shape=jax.ShapeDtypeStruct((bsz, tp, GROUP_W), MXU_DTYPE),
        scratch_shapes=[pltpu.VMEM((ki.shape[1] // KEY_STEP + 1, BLOCK, KEY_STEP), F32)],
        compiler_params=_params("parallel", "arbitrary"),
        name="dsa",
    )(iq, wh, ql, ki, c, wuv)


def _outproj_kernel(h_ref, yf_ref, yc_ref, yl_ref, yd_ref, w_ref, g_ref, o_ref, *, tm, tp):
    mix = _dot(yf_ref[...], w_ref[0:GROUP_W, :])
    mix = mix + _dot(yc_ref[...], w_ref[GROUP_W:2 * GROUP_W, :])
    mix = mix + _dot(yl_ref[...], w_ref[2 * GROUP_W:3 * GROUP_W, :])
    mix = mix + _dot(yd_ref[...], w_ref[3 * GROUP_W:4 * GROUP_W, :])
    row = (pl.program_id(0) * tm) % tp + lax.broadcasted_iota(jnp.int32, (tm, 1), 0)
    o_ref[...] = jnp.where(row >= PAD, h_ref[...] + _rms(mix, g_ref[...]), 0.0)


def _outproj(h, yf, yc, yl, yd, w, g, tp):
    rows, d = h.shape
    tm = _row_tile(rows)
    row = lambda i: (i, 0)
    const = lambda i: (0, 0)
    return pl.pallas_call(
        functools.partial(_outproj_kernel, tm=tm, tp=tp),
        grid=(rows // tm,),
        in_specs=[pl.BlockSpec((tm, d), row)] + [pl.BlockSpec((tm, GROUP_W), row)] * 4
                 + [pl.BlockSpec(w.shape, const), pl.BlockSpec((1, d), const)],
        out_specs=pl.BlockSpec((tm, d), row),
        out_shape=jax.ShapeDtypeStruct((rows, d), F32),
        compiler_params=_params("parallel"),
        name="out_proj",
    )(h, yf, yc, yl, yd, w, g)


def _pack_w_in(w_in):
    offs = np.cumsum((0,) + SPLIT_SIZES)
    fq, fk, fv, ff, cu, lx, lg, dq, dkv, iq, ik, iw = (w_in[..., offs[i]:offs[i + 1]] for i in range(12))
    padc = lambda a, n: jnp.pad(a, ((0, 0), (0, 0), (0, n - a.shape[-1])))
    cols = [fq * HEAD_DIM ** -0.5, fk, fv, padc(ff, 128), cu, lx, lg, dq, dkv, iq,
            jnp.tile(ik, (1, 1, IDX_HEADS)), padc(iw, 128)]
    return jnp.concatenate(cols, axis=-1).astype(MXU_DTYPE)


def _block_diag(w):
    depth, n, a, b = w.shape
    eye = jnp.eye(n, dtype=w.dtype)
    return jnp.einsum("lnab,nm->lnamb", w, eye).reshape(depth, n * a, n * b)


def _lane_pad(a, n):
    return jnp.pad(a, [(0, 0)] * (a.ndim - 1) + [(0, n - a.shape[-1])])


def kernel(x, meta_tokens, norm_g, ffn_w_in, ffn_w_out, w_in, w_out, fox_b_f, conv_dw_w, conv_dw_b, conv_ln_g,
           conv_ln_b, lru_conv_w, lru_conv_b, lru_w_a, lru_b_a, lru_w_i, lru_b_i, lru_lambda, dsa_kv_norm_g,
           dsa_w_uk, dsa_w_uv, idx_k_ln_g, idx_k_ln_b):
    bsz, seq, d = x.shape
    depth = norm_g.shape[0]
    assert seq % BLOCK == 0 and d % 128 == 0
    topk = min(TOPK_MAX, seq // 4)
    tp = PAD + N_META + seq
    rows = bsz * tp

    ffn_w_in_m = ffn_w_in.astype(MXU_DTYPE)
    ffn_w_out_m = ffn_w_out.astype(MXU_DTYPE)
    w_in_m = _pack_w_in(w_in)
    w_out_m = w_out.astype(MXU_DTYPE)
    wuk_m = (_block_diag(dsa_w_uk.transpose(0, 1, 3, 2)) * HEAD_DIM ** -0.5).astype(MXU_DTYPE)
    wuv_m = _block_diag(dsa_w_uv).astype(MXU_DTYPE)
    wa_m = _block_diag(lru_w_a).astype(MXU_DTYPE)
    wi_m = _block_diag(lru_w_i).astype(MXU_DTYPE)
    row2 = lambda a: a[:, None, :]
    dww = jnp.pad(conv_dw_w, ((0, 0), (0, CONV_HALO - CONV_WIDTH), (0, 0)))
    lcw = jnp.pad(lru_conv_w, ((0, 0), (0, LRU_HALO - LRU_CONV_WIDTH), (0, 0)))
    ln_g8 = row2(jnp.tile(idx_k_ln_g, (1, IDX_HEADS)))
    ln_b8 = row2(jnp.tile(idx_k_ln_b, (1, IDX_HEADS)))

    meta = jnp.broadcast_to(meta_tokens[None].astype(x.dtype), (bsz, N_META, d))
    h = jnp.concatenate([jnp.zeros((bsz, PAD, d), x.dtype), meta, x], axis=1).reshape(rows, d)

    for l in range(depth):
        g = norm_g[l][:, None, :]
        h = _ffn(h, g[0], g[1], ffn_w_in_m[l, 0], ffn_w_out_m[l, 0])
        z = _inproj(h, g[2], w_in_m[l], wuk_m[l], row2(dsa_kv_norm_g)[l], ln_g8[l], ln_b8[l])
        seq_params = {
            "fox_b_f": _lane_pad(fox_b_f[l][None], 128), "conv_dw_w": dww[l], "conv_dw_b": row2(conv_dw_b)[l],
            "conv_ln_g": row2(conv_ln_g)[l], "conv_ln_b": row2(conv_ln_b)[l], "lru_conv_w": lcw[l],
            "lru_conv_b": row2(lru_conv_b)[l], "lru_w_a": wa_m[l], "lru_b_a": row2(lru_b_a)[l],
            "lru_w_i": wi_m[l], "lru_b_i": row2(lru_b_i)[l], "lru_lambda": row2(lru_lambda)[l]}
        y_conv, y_lru, cum, cumt = _seqmix(z, bsz, tp, seq_params)
        r3 = lambda a: a.reshape(bsz, tp, a.shape[-1])
        tpk = -(-tp // KEY_STEP) * KEY_STEP
        keys = lambda a: jnp.pad(r3(a), ((0, 0), (0, tpk - tp), (0, 0)))
        cumt = jnp.pad(cumt, ((0, 0), (0, 0), (0, tpk - tp)))
        y_fox = _fox(r3(z["fq"]), keys(z["fk"]), keys(z["fv"]), cum, cumt)
        y_dsa = _dsa(r3(z["iq"]), r3(z["wh"]), r3(z["ql"]), keys(z["ki"]), keys(z["c"]), wuv_m[l], topk)
        r2 = lambda a: a.reshape(rows, a.shape[-1])
        h = _outproj(h, r2(y_fox), r2(y_conv), r2(y_lru), r2(y_dsa), w_out_m[l], g[3], tp)
        h = _ffn(h, g[4], g[5], ffn_w_in_m[l, 1], ffn_w_out_m[l, 1])

    return h.reshape(bsz, tp, d)[:, PAD + N_META:]
```

```python
import functools

import jax
import jax.numpy as jnp
import numpy as np
from jax import lax
from jax.experimental import pallas as pl
from jax.experimental.pallas import tpu as pltpu

N_META = 16
BLOCK = 128
PAD = BLOCK - N_META
KEY_STEP = 512
GROUP_W = 256
HEAD_DIM = 64
FOX_HEADS = 4
CONV_WIDTH = 31
CONV_HALO = 32
LRU_BLOCKS = 4
LRU_CONV_WIDTH = 4
LRU_HALO = 8
LRU_C = 8.0
DSA_HEADS = 4
DSA_LATENT = 128
IDX_HEADS = 8
IDX_DIM = 32
TOPK_MAX = 256
RMS_EPS = 1e-6
LN_EPS = 1e-5
SPLIT_SIZES = (GROUP_W, GROUP_W, GROUP_W, FOX_HEADS, 2 * GROUP_W, GROUP_W, GROUP_W,
               DSA_HEADS * HEAD_DIM, DSA_LATENT, IDX_HEADS * IDX_DIM, IDX_DIM, IDX_HEADS)

MXU_DTYPE = jnp.bfloat16
F32 = jnp.float32
NEG = -1e30
SCORE_MASKED = -3e38
VMEM_LIMIT = 56 * 1024 * 1024

_INPROJ_GROUPS = (("fq", 256), ("fk", 256), ("fv", 256), ("ff", 128), ("cu", 512), ("lx", 256),
                  ("lg", 256), ("dq", 256), ("dkv", 128), ("iq", 256), ("ik", 256), ("iw", 128))
_INPROJ_OFF = {}
_o = 0
for _n, _w in _INPROJ_GROUPS:
    _INPROJ_OFF[_n] = (_o, _w)
    _o += _w
INPROJ_COLS = _o


def _params(*sem):
    return pltpu.CompilerParams(dimension_semantics=sem, vmem_limit_bytes=VMEM_LIMIT)


def _rms(x, g):
    return x * lax.rsqrt(jnp.mean(x * x, axis=-1, keepdims=True) + RMS_EPS) * g


def _dot(a, b):
    return jnp.dot(a, b, preferred_element_type=F32)


def _dot_nt(a, b):
    return lax.dot_general(a, b, (((1,), (1,)), ((), ())), preferred_element_type=F32)


def _log_sigmoid(x):
    return jnp.minimum(x, 0.0) - jnp.log1p(jnp.exp(-jnp.abs(x)))


def _expm1(y):
    e = jnp.exp(y)
    regular = (e != 1.0) & (e > 0.0)
    r = (e - 1.0) * y / jnp.log(jnp.where(regular, e, 2.0))
    return jnp.where(regular, r, jnp.where(e > 0.0, y, -1.0))


def _row_tile(rows):
    return 640 if rows % 640 == 0 else BLOCK


def _ffn_kernel(h_ref, gpre_ref, gpost_ref, win_ref, wout_ref, o_ref, *, d_ff, chunk):
    x = h_ref[...]
    xn = _rms(x, gpre_ref[...]).astype(MXU_DTYPE)
    acc = jnp.zeros(x.shape, F32)
    for c in range(d_ff // chunk):
        gate = _dot(xn, win_ref[:, c * chunk:(c + 1) * chunk])
        up = _dot(xn, win_ref[:, d_ff + c * chunk:d_ff + (c + 1) * chunk])
        a = (gate * jax.nn.sigmoid(gate) * up).astype(MXU_DTYPE)
        acc = acc + _dot(a, wout_ref[c * chunk:(c + 1) * chunk, :])
    o_ref[...] = x + 0.5 * _rms(acc, gpost_ref[...])


def _ffn(h, g_pre, g_post, w_in, w_out):
    rows, d = h.shape
    d_ff = w_out.shape[0]
    tm = _row_tile(rows)
    chunk = 512 if d_ff % 512 == 0 else d_ff
    const = lambda i: (0, 0)
    return pl.pallas_call(
        functools.partial(_ffn_kernel, d_ff=d_ff, chunk=chunk),
        grid=(rows // tm,),
        in_specs=[pl.BlockSpec((tm, d), lambda i: (i, 0)),
                  pl.BlockSpec((1, d), const), pl.BlockSpec((1, d), const),
                  pl.BlockSpec(w_in.shape, const, pipeline_mode=pl.Buffered(1)),
                  pl.BlockSpec(w_out.shape, const, pipeline_mode=pl.Buffered(1))],
        out_specs=pl.BlockSpec((tm, d), lambda i: (i, 0)),
        out_shape=jax.ShapeDtypeStruct((rows, d), F32),
        compiler_params=_params("parallel"),
        name="ffn",
    )(h, g_pre, g_post, w_in, w_out)


def _inproj_kernel(h_ref, g_ref, w_ref, wuk_ref, kvg_ref, lng_ref, lnb_ref,
                   fq_ref, fk_ref, fv_ref, ff_ref, cu_ref, lx_ref, lg_ref,
                   qlt_ref, c_ref, ct_ref, iqt_ref, ki_ref, wht_ref):
    xn = _rms(h_ref[...], g_ref[...]).astype(MXU_DTYPE)

    def proj(name):
        lo, n = _INPROJ_OFF[name]
        return _dot(xn, w_ref[:, lo:lo + n])

    fq_ref[...] = proj("fq").astype(MXU_DTYPE)
    fk_ref[...] = proj("fk").astype(MXU_DTYPE)
    fv_ref[...] = proj("fv").astype(MXU_DTYPE)
    ff_ref[...] = proj("ff")
    cu_ref[...] = proj("cu")
    lx_ref[...] = proj("lx")
    lg_ref[...] = proj("lg")
    qlt_ref[...] = _dot(proj("dq").astype(MXU_DTYPE), wuk_ref[...]).T.astype(MXU_DTYPE)
    c = _rms(proj("dkv"), kvg_ref[...])
    c_ref[...] = c.astype(MXU_DTYPE)
    ct_ref[...] = c.T.astype(MXU_DTYPE)
    iqt_ref[...] = proj("iq").T.astype(MXU_DTYPE)
    ik = proj("ik")
    mu = jnp.mean(ik, axis=-1, keepdims=True)
    var = jnp.mean(jnp.square(ik - mu), axis=-1, keepdims=True)
    ki_ref[...] = ((ik - mu) * lax.rsqrt(var + LN_EPS) * lng_ref[...] + lnb_ref[...]).astype(MXU_DTYPE)
    wht_ref[...] = (proj("iw") * (IDX_HEADS ** -0.5 * IDX_DIM ** -0.5)).T[:IDX_HEADS, :]


def _inproj(h, g, w, wuk, kvg, lng, lnb):
    rows, d = h.shape
    tm = _row_tile(rows)
    const = lambda i: (0, 0)
    row = lambda i: (i, 0)
    col = lambda i: (0, i)
    outs = (("fq", 256, MXU_DTYPE, False), ("fk", 256, MXU_DTYPE, False), ("fv", 256, MXU_DTYPE, False),
            ("ff", 128, F32, False), ("cu", 512, F32, False), ("lx", 256, F32, False), ("lg", 256, F32, False),
            ("qlT", 512, MXU_DTYPE, True), ("c", 128, MXU_DTYPE, False), ("cT", 128, MXU_DTYPE, True),
            ("iqT", 256, MXU_DTYPE, True), ("ki", 256, MXU_DTYPE, False), ("whT", IDX_HEADS, F32, True))
    res = pl.pallas_call(
        _inproj_kernel,
        grid=(rows // tm,),
        in_specs=[pl.BlockSpec((tm, d), row), pl.BlockSpec((1, d), const),
                  pl.BlockSpec(w.shape, const, pipeline_mode=pl.Buffered(1)),
                  pl.BlockSpec(wuk.shape, const),
                  pl.BlockSpec(kvg.shape, const), pl.BlockSpec(lng.shape, const), pl.BlockSpec(lnb.shape, const)],
        out_specs=[pl.BlockSpec((n, tm), col) if t else pl.BlockSpec((tm, n), row) for _, n, _, t in outs],
        out_shape=[jax.ShapeDtypeStruct((n, rows) if t else (rows, n), dt) for _, n, dt, t in outs],
        compiler_params=_params("parallel"),
        name="in_proj",
    )(h, g, w, wuk, kvg, lng, lnb)
    return dict(zip([n for n, _, _, _ in outs], res))


def _shift_rows(x, s, fill, rows):
    return jnp.where(rows >= s, pltpu.roll(x, s, axis=0), fill)


def _seqmix_kernel(ff_ref, cu_ref, lx_ref, lg_ref, bf_ref, dww_ref, dwb_ref, lng_ref, lnb_ref,
                   lcw_ref, lcb_ref, wa_ref, ba_ref, wi_ref, bi_ref, lam_ref,
                   yc_ref, yl_ref, cum_ref, cumt_ref,
                   glu_buf, lx_buf, h_carry, cum_carry):
    t = pl.program_id(1)

    @pl.when(t == 0)
    def _():
        glu_buf[...] = jnp.zeros(glu_buf.shape, F32)
        lx_buf[...] = jnp.zeros(lx_buf.shape, F32)
        h_carry[...] = jnp.zeros(h_carry.shape, F32)
        cum_carry[...] = jnp.zeros(cum_carry.shape, F32)

    rows = lax.broadcasted_iota(jnp.int32, (BLOCK, 1), 0)
    valid = (t * BLOCK + rows) >= PAD

    cu = cu_ref[0]
    glu = jnp.where(valid, cu[:, :GROUP_W] * jax.nn.sigmoid(cu[:, GROUP_W:]), 0.0)
    glu_buf[CONV_HALO:, :] = glu
    acc = jnp.zeros((BLOCK, GROUP_W), F32) + dwb_ref[...]
    for k in range(CONV_WIDTH):
        lo = CONV_HALO - (CONV_WIDTH - 1) + k
        acc = acc + dww_ref[k:k + 1, :] * glu_buf[lo:lo + BLOCK, :]
    glu_buf[:CONV_HALO, :] = glu_buf[BLOCK:, :]
    mu = jnp.mean(acc, axis=-1, keepdims=True)
    var = jnp.mean(jnp.square(acc - mu), axis=-1, keepdims=True)
    hc = (acc - mu) * lax.rsqrt(var + LN_EPS) * lng_ref[...] + lnb_ref[...]
    yc_ref[0] = (hc * jax.nn.sigmoid(hc)).astype(yc_ref.dtype)

    lx_buf[LRU_HALO:, :] = jnp.where(valid, lx_ref[0], 0.0)
    xc = jnp.zeros((BLOCK, GROUP_W), F32) + lcb_ref[...]
    for k in range(LRU_CONV_WIDTH):
        lo = LRU_HALO - (LRU_CONV_WIDTH - 1) + k
        xc = xc + lcw_ref[k:k + 1, :] * lx_buf[lo:lo + BLOCK, :]
    lx_buf[:LRU_HALO, :] = lx_buf[BLOCK:, :]
    xcm = xc.astype(MXU_DTYPE)
    r = jax.nn.sigmoid(_dot(xcm, wa_ref[...]) + ba_ref[...])
    gi = jax.nn.sigmoid(_dot(xcm, wi_ref[...]) + bi_ref[...])
    log_a = LRU_C * r * _log_sigmoid(lam_ref[...])
    a = jnp.exp(log_a)
    u = jnp.where(valid, jnp.sqrt(-_expm1(2.0 * log_a)) * (gi * xc), 0.0)
    s = 1
    while s < BLOCK:
        u = a * _shift_rows(u, s, 0.0, rows) + u
        a = a * _shift_rows(a, s, 1.0, rows)
        s *= 2
    hl = u + a * h_carry[0:1, :]
    h_carry[...] = jnp.broadcast_to(hl[BLOCK - 1:BLOCK, :], h_carry.shape)
    g = lg_ref[0]
    gelu = 0.5 * g * (1.0 + jnp.tanh(np.sqrt(2.0 / np.pi).astype(np.float32) * (g + 0.044715 * g * g * g)))
    yl_ref[0] = (hl * gelu).astype(yl_ref.dtype)

    cs = _log_sigmoid(ff_ref[0] + bf_ref[...])
    s = 1
    while s < BLOCK:
        cs = cs + _shift_rows(cs, s, 0.0, rows)
        s *= 2
    cs = cs + cum_carry[0:1, :]
    cum_carry[...] = jnp.broadcast_to(cs[BLOCK - 1:BLOCK, :], cum_carry.shape)
    cum_ref[0] = cs
    key_valid = (t * BLOCK + lax.broadcasted_iota(jnp.int32, (1, BLOCK), 1)) >= PAD
    cumt_ref[0] = jnp.where(key_valid, cs.T[:8, :], -NEG)


def _seqmix(z, bsz, tp, p):
    nblk = tp // BLOCK
    blk = lambda n: pl.BlockSpec((1, BLOCK, n), lambda b, t: (b, t, 0))
    const = lambda a: pl.BlockSpec(a.shape, lambda b, t: (0, 0))
    r3 = lambda a: a.reshape(bsz, tp, a.shape[-1])
    params = (p["fox_b_f"], p["conv_dw_w"], p["conv_dw_b"], p["conv_ln_g"], p["conv_ln_b"],
              p["lru_conv_w"], p["lru_conv_b"], p["lru_w_a"], p["lru_b_a"], p["lru_w_i"], p["lru_b_i"],
              p["lru_lambda"])
    return pl.pallas_call(
        _seqmix_kernel,
        grid=(bsz, nblk),
        in_specs=[blk(128), blk(512), blk(256), blk(256)] + [const(a) for a in params],
        out_specs=[blk(256), blk(256), blk(128), pl.BlockSpec((1, 8, BLOCK), lambda b, t: (b, 0, t))],
        out_shape=[jax.ShapeDtypeStruct((bsz, tp, GROUP_W), MXU_DTYPE),
                   jax.ShapeDtypeStruct((bsz, tp, GROUP_W), MXU_DTYPE),
                   jax.ShapeDtypeStruct((bsz, tp, 128), F32),
                   jax.ShapeDtypeStruct((bsz, 8, tp), F32)],
        scratch_shapes=[pltpu.VMEM((CONV_HALO + BLOCK, GROUP_W), F32),
                        pltpu.VMEM((LRU_HALO + BLOCK, GROUP_W), F32),
                        pltpu.VMEM((8, GROUP_W), F32),
                        pltpu.VMEM((8, 128), F32)],
        compiler_params=_params("parallel", "arbitrary"),
        name="seq_mix",
    )(r3(z["ff"]), r3(z["cu"]), r3(z["lx"]), r3(z["lg"]), *params)


def _fox_kernel(q_ref, cq_ref, k_ref, v_ref, ckt_ref, o_ref):
    qi = pl.program_id(1)
    q = q_ref[0]
    cq_all = cq_ref[0]
    lane = lax.broadcasted_iota(jnp.int32, (1, BLOCK), 1)
    lane_kb = lax.broadcasted_iota(jnp.int32, (1, KEY_STEP), 1)
    q_pos = qi * BLOCK + lax.broadcasted_iota(jnp.int32, (BLOCK, 1), 0)
    qm, cq = [], []
    for h in range(FOX_HEADS):
        pair, hh = divmod(h, 2)
        qp = q[:, pair * BLOCK:(pair + 1) * BLOCK]
        in_head = (lane >= hh * HEAD_DIM) & (lane < (hh + 1) * HEAD_DIM)
        qm.append(jnp.where(in_head, qp, jnp.zeros_like(qp)))
        cq.append(cq_all[:, h:h + 1])
    qm_pair = [jnp.concatenate(qm[2 * p:2 * p + 2], axis=0) for p in range(FOX_HEADS // 2)]

    cols = [slice((h // 2) * BLOCK, (h // 2 + 1) * BLOCK) for h in range(FOX_HEADS)]

    def qk(j):
        ks = pl.multiple_of(j * KEY_STEP, KEY_STEP)
        return tuple(_dot_nt(qm_pair[p], k_ref[0, pl.ds(ks, KEY_STEP), cols[2 * p]]) for p in range(FOX_HEADS // 2))

    def update(j, pair_scores, carry, causal_mask):
        ks = pl.multiple_of(j * KEY_STEP, KEY_STEP)
        out = []
        for p in range(FOX_HEADS // 2):
            stats, probs = [], []
            for h in (2 * p, 2 * p + 1):
                m, l, _ = carry[h]
                s = pair_scores[p][(h % 2) * BLOCK:(h % 2 + 1) * BLOCK]
                s = s + cq[h] - ckt_ref[0, h:h + 1, pl.ds(ks, KEY_STEP)]
                if causal_mask:
                    s = jnp.where(ks + lane_kb <= q_pos, s, NEG)
                m_new = jnp.maximum(m, jnp.max(s, axis=-1, keepdims=True))
                alpha = jnp.exp(m - m_new)
                pr = jnp.exp(s - m_new)
                stats.append((m_new, alpha, alpha * l + jnp.sum(pr, axis=-1, keepdims=True)))
                probs.append(pr.astype(MXU_DTYPE))
            pv = _dot(jnp.concatenate(probs, axis=0), v_ref[0, pl.ds(ks, KEY_STEP), cols[2 * p]])
            for i, (m_new, alpha, l) in enumerate(stats):
                out.append((m_new, l, alpha * carry[2 * p + i][2] + pv[i * BLOCK:(i + 1) * BLOCK]))
        return tuple(out)

    init = tuple((jnp.full((BLOCK, 1), NEG, F32), jnp.zeros((BLOCK, 1), F32), jnp.zeros((BLOCK, BLOCK), F32))
                 for _ in range(FOX_HEADS))
    n_full = (qi * BLOCK) // KEY_STEP

    carry = lax.fori_loop(0, n_full, lambda j, carry: update(j, qk(j), carry, False), init)
    carry = update(n_full, qk(n_full), carry, True)
    res = [acc / l for _, l, acc in carry]
    outs = [jnp.where(lane < HEAD_DIM, res[2 * pair], res[2 * pair + 1]) for pair in range(FOX_HEADS // 2)]
    o_ref[0] = jnp.concatenate(outs, axis=1).astype(o_ref.dtype)


def _fox(fq, fk, fv, cum, cumt):
    bsz, tp, _ = fq.shape
    nblk = tp // BLOCK
    qblk = lambda n: pl.BlockSpec((1, BLOCK, n), lambda b, i: (b, i, 0))
    full = lambda a: pl.BlockSpec((1,) + a.shape[1:], lambda b, i: (b, 0, 0))
    return pl.pallas_call(
        _fox_kernel,
        grid=(bsz, nblk),
        in_specs=[qblk(GROUP_W), qblk(128), full(fk), full(fv), full(cumt)],
        out_specs=qblk(GROUP_W),
        out_shape=jax.ShapeDtypeStruct((bsz, tp, GROUP_W), MXU_DTYPE),
        compiler_params=_params("parallel", "arbitrary"),
        name="fox",
    )(fq, cum, fk, fv, cumt)


def _dsa_kernel(iqt_ref, wht_ref, qlt_ref, ki_ref, c_ref, ct_ref, wuvt_ref, o_ref, sc_ref, *, topk):
    qi = pl.program_id(1)
    nkb = (qi * BLOCK) // KEY_STEP + 1
    k_off = lax.broadcasted_iota(jnp.int32, (KEY_STEP, 1), 0)
    q_pos = qi * BLOCK + lax.broadcasted_iota(jnp.int32, (1, BLOCK), 1)
    n_valid = q_pos - PAD + 1
    fold_rows = KEY_STEP // 8

    def fold(op, w):
        parts = [w[i * fold_rows:(i + 1) * fold_rows] for i in range(8)]
        return op(op(op(parts[0], parts[1]), op(parts[2], parts[3])), op(op(parts[4], parts[5]), op(parts[6], parts[7])))

    iqt = iqt_ref[...]
    wht = wht_ref[...]
    row_i = lax.broadcasted_iota(jnp.int32, (IDX_HEADS * IDX_DIM, 1), 0)
    q_heads = jnp.concatenate(
        [jnp.where((row_i >= h * IDX_DIM) & (row_i < (h + 1) * IDX_DIM), iqt, jnp.zeros_like(iqt))
         for h in range(IDX_HEADS)], axis=1)

    def score_body(j, amax):
        ks = pl.multiple_of(j * KEY_STEP, KEY_STEP)
        dots = _dot(ki_ref[0, pl.ds(ks, KEY_STEP), :], q_heads)
        sc = jnp.zeros((KEY_STEP, BLOCK), F32)
        for h in range(IDX_HEADS):
            sc = sc + jnp.maximum(dots[:, h * BLOCK:(h + 1) * BLOCK], 0.0) * wht[h:h + 1, :]
        k_pos = ks + k_off
        valid = (k_pos <= q_pos) & (k_pos >= PAD)
        sc_ref[j] = jnp.where(valid, sc, SCORE_MASKED)
        return jnp.maximum(amax, fold(jnp.maximum, jnp.where(valid, jnp.abs(sc), 0.0)))

    amax = lax.fori_loop(0, nkb, score_body, jnp.zeros((fold_rows, BLOCK), F32))
    bound = jnp.max(amax, axis=0, keepdims=True) * 1.0001 + 1e-30

    @pl.when(nkb % 2 == 1)
    def _():
        sc_ref[nkb] = jnp.full((KEY_STEP, BLOCK), SCORE_MASKED, F32)

    def count(*preds):
        def body(jj, accs):
            tiles = (sc_ref[2 * jj], sc_ref[2 * jj + 1])
            return tuple(tuple(acc + fold(jnp.add, jnp.where(pred(t), 1, 0)) for acc, t in zip(pair, tiles))
                         for pair, pred in zip(accs, preds))
        zeros = jnp.zeros((fold_rows, BLOCK), jnp.int32)
        accs = lax.fori_loop(0, (nkb + 1) // 2, body, tuple((zeros, zeros) for _ in preds))
        return [jnp.sum(a + b, axis=0, keepdims=True).astype(F32) for a, b in accs]

    kf = float(topk)
    take_all = n_valid <= topk
    c_ge0, c_gt0 = count(lambda s: s >= 0.0, lambda s: s > 0.0)
    positive = c_gt0 >= kf
    zero_tie = (c_ge0 >= kf) & jnp.logical_not(positive)
    lo = jnp.where(positive | zero_tie, 0.0, -bound)
    c_lo = jnp.where(positive | zero_tie, c_ge0, n_valid.astype(F32))
    hi = jnp.where(positive, bound, 0.0)
    c_hi = jnp.where(positive, 0.0, jnp.where(zero_tie, c_gt0, c_ge0))
    done = jnp.where(take_all | zero_tie | (c_lo == kf), 1, 0)

    def search_cond(state):
        return state[-1] > 0

    def search_body(state):
        lo, hi, c_lo, c_hi, done, it, _ = state
        mid = 0.5 * lo + 0.5 * hi
        inside = (mid > lo) & (mid < hi)
        log_lo = jnp.log(c_lo)
        frac = (log_lo - np.log(kf)) / (log_lo - jnp.log(jnp.maximum(c_hi, 0.5)))
        probe = lo + (hi - lo) * jnp.clip(frac, 0.02, 0.98)
        use_probe = ((jnp.zeros_like(done) + it % 3) != 2) & (probe > lo) & (probe < hi)
        probe = jnp.where(use_probe, probe, mid)
        c, = count(lambda s: s >= probe)
        active = (done == 0) & inside
        up = active & (c >= kf)
        down = active & (c < kf)
        lo, c_lo = jnp.where(up, probe, lo), jnp.where(up, c, c_lo)
        hi, c_hi = jnp.where(down, probe, hi), jnp.where(down, c, c_hi)
        done = jnp.where((done > 0) | jnp.logical_not(inside) | (c_lo == kf), 1, 0)
        return lo, hi, c_lo, c_hi, done, it + 1, jnp.sum(1 - done)

    thr, _, c_thr, c_above, _, _, _ = lax.while_loop(
        search_cond, search_body, (lo, hi, c_lo, c_hi, done, jnp.int32(0), jnp.sum(1 - done)))
    thr = jnp.where(take_all, 0.5 * SCORE_MASKED, thr)
    tied = jnp.logical_not(take_all) & (c_thr > kf)
    any_tied = jnp.max(jnp.where(tied, 1, 0)) > 0

    qlt = qlt_ref[...]
    q_lat = jnp.concatenate([qlt[h * DSA_LATENT:(h + 1) * DSA_LATENT] for h in range(DSA_HEADS)], axis=1)
    wide = DSA_HEADS * BLOCK

    def attend(select):
        def body(j, carry):
            m, l, acc, run = carry
            ks = pl.multiple_of(j * KEY_STEP, KEY_STEP)
            sel, run = select(sc_ref[j], run)
            s_all = _dot(c_ref[0, pl.ds(ks, KEY_STEP), :], q_lat)
            s_all = jnp.concatenate([jnp.where(sel, s_all[:, h * BLOCK:(h + 1) * BLOCK], NEG)
                                     for h in range(DSA_HEADS)], axis=1)
            m_new = jnp.maximum(m, jnp.max(s_all, axis=0, keepdims=True))
            alpha = jnp.exp(m - m_new)
            pr = jnp.exp(s_all - m_new)
            l = alpha * l + jnp.sum(pr, axis=0, keepdims=True)
            acc = alpha * acc + _dot(ct_ref[0, :, pl.ds(ks, KEY_STEP)], pr.astype(MXU_DTYPE))
            return m_new, l, acc, run

        init = (jnp.full((1, wide), NEG, F32), jnp.zeros((1, wide), F32), jnp.zeros((DSA_LATENT, wide), F32),
                jnp.zeros((1, BLOCK), F32))
        _, l, acc, _ = lax.fori_loop(0, nkb, body, init)
        o_lat = (acc / jnp.where(l > 0.0, l, 1.0)).astype(MXU_DTYPE)
        y_t = jnp.zeros((GROUP_W, BLOCK), F32)
        for h in range(DSA_HEADS):
            y_t = y_t + _dot(wuvt_ref[:, h * DSA_LATENT:(h + 1) * DSA_LATENT], o_lat[:, h * BLOCK:(h + 1) * BLOCK])
        o_ref[0] = y_t.T.astype(o_ref.dtype)

    @pl.when(jnp.logical_not(any_tied))
    def _():
        attend(lambda s, run: (s >= thr, run))

    @pl.when(any_tied)
    def _():
        quota = jnp.where(tied, kf - c_above, 2.0 ** 30)
        r_i = lax.broadcasted_iota(jnp.int32, (KEY_STEP, KEY_STEP), 0)
        c_i = lax.broadcasted_iota(jnp.int32, (KEY_STEP, KEY_STEP), 1)
        tri = jnp.where(c_i <= r_i, 1.0, 0.0).astype(MXU_DTYPE)

        def select(s, run):
            eqf = jnp.where(s == thr, 1.0, 0.0)
            prefix = _dot(tri, eqf.astype(MXU_DTYPE)) + run
            within = jnp.where(prefix <= quota, eqf, 0.0)
            sel = (jnp.where(s > thr, 1.0, 0.0) + within) > 0.5
            return sel, run + jnp.sum(eqf, axis=0, keepdims=True)

        attend(select)


def _dsa(iqt, wht, qlt, ki, c, ct, wuvt, topk):
    bsz, tpk, _ = ki.shape
    tp = iqt.shape[1] // bsz
    nblk = tp // BLOCK
    qcol = lambda n: pl.BlockSpec((n, BLOCK), lambda b, i: (0, b * nblk + i))
    full = lambda a: pl.BlockSpec((1,) + a.shape[1:], lambda b, i: (b, 0, 0))
    return pl.pallas_call(
        functools.partial(_dsa_kernel, topk=topk),
        grid=(bsz, nblk),
        in_specs=[qcol(iqt.shape[0]), qcol(wht.shape[0]), qcol(qlt.shape[0]), full(ki), full(c), full(ct),
                  pl.BlockSpec(wuvt.shape, lambda b, i: (0, 0))],
        out_specs=pl.BlockSpec((1, BLOCK, GROUP_W), lambda b, i: (b, i, 0)),
        out_shape=jax.ShapeDtypeStruct((bsz, tp, GROUP_W), MXU_DTYPE),
        scratch_shapes=[pltpu.VMEM((tpk // KEY_STEP + 1, KEY_STEP, BLOCK), F32)],
        compiler_params=_params("parallel", "arbitrary"),
        name="dsa",
    )(iqt, wht, qlt, ki, c, ct, wuvt)


def _outproj_kernel(h_ref, yf_ref, yc_ref, yl_ref, yd_ref, w_ref, g_ref, o_ref, *, tm, tp):
    mix = _dot(yf_ref[...], w_ref[0:GROUP_W, :])
    mix = mix + _dot(yc_ref[...], w_ref[GROUP_W:2 * GROUP_W, :])
    mix = mix + _dot(yl_ref[...], w_ref[2 * GROUP_W:3 * GROUP_W, :])
    mix = mix + _dot(yd_ref[...], w_ref[3 * GROUP_W:4 * GROUP_W, :])
    row = (pl.program_id(0) * tm) % tp + lax.broadcasted_iota(jnp.int32, (tm, 1), 0)
    o_ref[...] = jnp.where(row >= PAD, h_ref[...] + _rms(mix, g_ref[...]), 0.0)


def _outproj(h, yf, yc, yl, yd, w, g, tp):
    rows, d = h.shape
    tm = _row_tile(rows)
    row = lambda i: (i, 0)
    const = lambda i: (0, 0)
    return pl.pallas_call(
        functools.partial(_outproj_kernel, tm=tm, tp=tp),
        grid=(rows // tm,),
        in_specs=[pl.BlockSpec((tm, d), row)] + [pl.BlockSpec((tm, GROUP_W), row)] * 4
                 + [pl.BlockSpec(w.shape, const), pl.BlockSpec((1, d), const)],
        out_specs=pl.BlockSpec((tm, d), row),
        out_shape=jax.ShapeDtypeStruct((rows, d), F32),
        compiler_params=_params("parallel"),
        name="out_proj",
    )(h, yf, yc, yl, yd, w, g)


def _pack_w_in(w_in):
    offs = np.cumsum((0,) + SPLIT_SIZES)
    fq, fk, fv, ff, cu, lx, lg, dq, dkv, iq, ik, iw = (w_in[..., offs[i]:offs[i + 1]] for i in range(12))
    padc = lambda a, n: jnp.pad(a, ((0, 0), (0, 0), (0, n - a.shape[-1])))
    cols = [fq * HEAD_DIM ** -0.5, fk, fv, padc(ff, 128), cu, lx, lg, dq, dkv, iq,
            jnp.tile(ik, (1, 1, IDX_HEADS)), padc(iw, 128)]
    return jnp.concatenate(cols, axis=-1).astype(MXU_DTYPE)


def _block_diag(w):
    depth, n, a, b = w.shape
    eye = jnp.eye(n, dtype=w.dtype)
    return jnp.einsum("lnab,nm->lnamb", w, eye).reshape(depth, n * a, n * b)


def _lane_pad(a, n):
    return jnp.pad(a, [(0, 0)] * (a.ndim - 1) + [(0, n - a.shape[-1])])


def kernel(x, meta_tokens, norm_g, ffn_w_in, ffn_w_out, w_in, w_out, fox_b_f, conv_dw_w, conv_dw_b, conv_ln_g,
           conv_ln_b, lru_conv_w, lru_conv_b, lru_w_a, lru_b_a, lru_w_i, lru_b_i, lru_lambda, dsa_kv_norm_g,
           dsa_w_uk, dsa_w_uv, idx_k_ln_g, idx_k_ln_b):
    bsz, seq, d = x.shape
    depth = norm_g.shape[0]
    assert seq % BLOCK == 0 and d % 128 == 0
    topk = min(TOPK_MAX, seq // 4)
    tp = PAD + N_META + seq
    rows = bsz * tp

    ffn_w_in_m = ffn_w_in.astype(MXU_DTYPE)
    ffn_w_out_m = ffn_w_out.astype(MXU_DTYPE)
    w_in_m = _pack_w_in(w_in)
    w_out_m = w_out.astype(MXU_DTYPE)
    wuk_m = (_block_diag(dsa_w_uk.transpose(0, 1, 3, 2)) * HEAD_DIM ** -0.5).astype(MXU_DTYPE)
    wuvt_m = _block_diag(dsa_w_uv).transpose(0, 2, 1).astype(MXU_DTYPE)
    wa_m = _block_diag(lru_w_a).astype(MXU_DTYPE)
    wi_m = _block_diag(lru_w_i).astype(MXU_DTYPE)
    row2 = lambda a: a[:, None, :]
    dww = jnp.pad(conv_dw_w, ((0, 0), (0, CONV_HALO - CONV_WIDTH), (0, 0)))
    lcw = jnp.pad(lru_conv_w, ((0, 0), (0, LRU_HALO - LRU_CONV_WIDTH), (0, 0)))
    ln_g8 = row2(jnp.tile(idx_k_ln_g, (1, IDX_HEADS)))
    ln_b8 = row2(jnp.tile(idx_k_ln_b, (1, IDX_HEADS)))

    meta = jnp.broadcast_to(meta_tokens[None].astype(x.dtype), (bsz, N_META, d))
    h = jnp.concatenate([jnp.zeros((bsz, PAD, d), x.dtype), meta, x], axis=1).reshape(rows, d)

    for l in range(depth):
        g = norm_g[l][:, None, :]
        h = _ffn(h, g[0], g[1], ffn_w_in_m[l, 0], ffn_w_out_m[l, 0])
        z = _inproj(h, g[2], w_in_m[l], wuk_m[l], row2(dsa_kv_norm_g)[l], ln_g8[l], ln_b8[l])
        seq_params = {
            "fox_b_f": _lane_pad(fox_b_f[l][None], 128), "conv_dw_w": dww[l], "conv_dw_b": row2(conv_dw_b)[l],
            "conv_ln_g": row2(conv_ln_g)[l], "conv_ln_b": row2(conv_ln_b)[l], "lru_conv_w": lcw[l],
            "lru_conv_b": row2(lru_conv_b)[l], "lru_w_a": wa_m[l], "lru_b_a": row2(lru_b_a)[l],
            "lru_w_i": wi_m[l], "lru_b_i": row2(lru_b_i)[l], "lru_lambda": row2(lru_lambda)[l]}
        y_conv, y_lru, cum, cumt = _seqmix(z, bsz, tp, seq_params)
        r3 = lambda a: a.reshape(bsz, tp, a.shape[-1])
        tpk = -(-tp // KEY_STEP) * KEY_STEP
        keys = lambda a: jnp.pad(r3(a), ((0, 0), (0, tpk - tp), (0, 0)))
        cumt = jnp.pad(cumt, ((0, 0), (0, 0), (0, tpk - tp)))
        y_fox = _fox(r3(z["fq"]), keys(z["fk"]), keys(z["fv"]), cum, cumt)
        ct = jnp.pad(z["cT"].reshape(DSA_LATENT, bsz, tp).transpose(1, 0, 2), ((0, 0), (0, 0), (0, tpk - tp)))
        y_dsa = _dsa(z["iqT"], z["whT"], z["qlT"], keys(z["ki"]), keys(z["c"]), ct, wuvt_m[l], topk)
        r2 = lambda a: a.reshape(rows, a.shape[-1])
        h = _outproj(h, r2(y_fox), r2(y_conv), r2(y_lru), r2(y_dsa), w_out_m[l], g[3], tp)
        h = _ffn(h, g[4], g[5], ffn_w_in_m[l, 1], ffn_w_out_m[l, 1])

    return h.reshape(bsz, tp, d)[:, PAD + N_META:]
```

```python
import functools

import jax
import jax.numpy as jnp
import numpy as np
from jax import lax
from jax.experimental import pallas as pl
from jax.experimental.pallas import tpu as pltpu

N_META = 16
BLOCK = 128
PAD = BLOCK - N_META
KEY_STEP = 512
GROUP_W = 256
HEAD_DIM = 64
FOX_HEADS = 4
CONV_WIDTH = 31
CONV_HALO = 32
LRU_BLOCKS = 4
LRU_CONV_WIDTH = 4
LRU_HALO = 8
LRU_C = 8.0
DSA_HEADS = 4
DSA_LATENT = 128
IDX_HEADS = 8
IDX_DIM = 32
TOPK_MAX = 256
RMS_EPS = 1e-6
LN_EPS = 1e-5
SPLIT_SIZES = (GROUP_W, GROUP_W, GROUP_W, FOX_HEADS, 2 * GROUP_W, GROUP_W, GROUP_W,
               DSA_HEADS * HEAD_DIM, DSA_LATENT, IDX_HEADS * IDX_DIM, IDX_DIM, IDX_HEADS)

MXU_DTYPE = jnp.bfloat16
F32 = jnp.float32
NEG = -1e30
SCORE_MASKED = -3e38
VMEM_LIMIT = 56 * 1024 * 1024

_INPROJ_GROUPS = (("fq", 256), ("fk", 256), ("fv", 256), ("ff", 128), ("cu", 512), ("lx", 256),
                  ("lg", 256), ("dq", 256), ("dkv", 128), ("iq", 256), ("ik", 256), ("iw", 128))
_INPROJ_OFF = {}
_o = 0
for _n, _w in _INPROJ_GROUPS:
    _INPROJ_OFF[_n] = (_o, _w)
    _o += _w
INPROJ_COLS = _o


def _params(*sem):
    return pltpu.CompilerParams(dimension_semantics=sem, vmem_limit_bytes=VMEM_LIMIT)


def _rms(x, g):
    return x * lax.rsqrt(jnp.mean(x * x, axis=-1, keepdims=True) + RMS_EPS) * g


def _dot(a, b):
    return jnp.dot(a, b, preferred_element_type=F32)


def _log_sigmoid(x):
    return jnp.minimum(x, 0.0) - jnp.log1p(jnp.exp(-jnp.abs(x)))


def _expm1(y):
    e = jnp.exp(y)
    regular = (e != 1.0) & (e > 0.0)
    r = (e - 1.0) * y / jnp.log(jnp.where(regular, e, 2.0))
    return jnp.where(regular, r, jnp.where(e > 0.0, y, -1.0))


def _row_tile(rows):
    return 640 if rows % 640 == 0 else BLOCK


def _ffn_kernel(h_ref, gpre_ref, gpost_ref, win_ref, wout_ref, o_ref, *, d_ff, chunk):
    x = h_ref[...]
    xn = _rms(x, gpre_ref[...]).astype(MXU_DTYPE)
    acc = jnp.zeros(x.shape, F32)
    for c in range(d_ff // chunk):
        gate = _dot(xn, win_ref[:, c * chunk:(c + 1) * chunk])
        up = _dot(xn, win_ref[:, d_ff + c * chunk:d_ff + (c + 1) * chunk])
        a = (gate * jax.nn.sigmoid(gate) * up).astype(MXU_DTYPE)
        acc = acc + _dot(a, wout_ref[c * chunk:(c + 1) * chunk, :])
    o_ref[...] = x + 0.5 * _rms(acc, gpost_ref[...])


def _ffn(h, g_pre, g_post, w_in, w_out):
    rows, d = h.shape
    d_ff = w_out.shape[0]
    tm = _row_tile(rows)
    chunk = 512 if d_ff % 512 == 0 else d_ff
    const = lambda i: (0, 0)
    return pl.pallas_call(
        functools.partial(_ffn_kernel, d_ff=d_ff, chunk=chunk),
        grid=(rows // tm,),
        in_specs=[pl.BlockSpec((tm, d), lambda i: (i, 0)),
                  pl.BlockSpec((1, d), const), pl.BlockSpec((1, d), const),
                  pl.BlockSpec(w_in.shape, const, pipeline_mode=pl.Buffered(1)),
                  pl.BlockSpec(w_out.shape, const, pipeline_mode=pl.Buffered(1))],
        out_specs=pl.BlockSpec((tm, d), lambda i: (i, 0)),
        out_shape=jax.ShapeDtypeStruct((rows, d), F32),
        compiler_params=_params("parallel"),
        name="ffn",
    )(h, g_pre, g_post, w_in, w_out)


def _inproj_kernel(h_ref, g_ref, w_ref, wuk_ref, kvg_ref, lng_ref, lnb_ref,
                   fqt_ref, fk_ref, fvt_ref, ff_ref, cu_ref, lx_ref, lg_ref,
                   qlt_ref, c_ref, ct_ref, iqt_ref, ki_ref, wht_ref):
    xn = _rms(h_ref[...], g_ref[...]).astype(MXU_DTYPE)

    def proj(name):
        lo, n = _INPROJ_OFF[name]
        return _dot(xn, w_ref[:, lo:lo + n])

    fqt_ref[...] = proj("fq").T.astype(MXU_DTYPE)
    fk_ref[...] = proj("fk").astype(MXU_DTYPE)
    fvt_ref[...] = proj("fv").T.astype(MXU_DTYPE)
    ff_ref[...] = proj("ff")
    cu_ref[...] = proj("cu")
    lx_ref[...] = proj("lx")
    lg_ref[...] = proj("lg")
    qlt_ref[...] = _dot(proj("dq").astype(MXU_DTYPE), wuk_ref[...]).T.astype(MXU_DTYPE)
    c = _rms(proj("dkv"), kvg_ref[...])
    c_ref[...] = c.astype(MXU_DTYPE)
    ct_ref[...] = c.T.astype(MXU_DTYPE)
    iqt_ref[...] = proj("iq").T.astype(MXU_DTYPE)
    ik = proj("ik")
    mu = jnp.mean(ik, axis=-1, keepdims=True)
    var = jnp.mean(jnp.square(ik - mu), axis=-1, keepdims=True)
    ki_ref[...] = ((ik - mu) * lax.rsqrt(var + LN_EPS) * lng_ref[...] + lnb_ref[...]).astype(MXU_DTYPE)
    wht_ref[...] = (proj("iw") * (IDX_HEADS ** -0.5 * IDX_DIM ** -0.5)).T[:IDX_HEADS, :]


def _inproj(h, g, w, wuk, kvg, lng, lnb):
    rows, d = h.shape
    tm = _row_tile(rows)
    const = lambda i: (0, 0)
    row = lambda i: (i, 0)
    col = lambda i: (0, i)
    outs = (("fqT", 256, MXU_DTYPE, True), ("fk", 256, MXU_DTYPE, False), ("fvT", 256, MXU_DTYPE, True),
            ("ff", 128, F32, False), ("cu", 512, F32, False), ("lx", 256, F32, False), ("lg", 256, F32, False),
            ("qlT", 512, MXU_DTYPE, True), ("c", 128, MXU_DTYPE, False), ("cT", 128, MXU_DTYPE, True),
            ("iqT", 256, MXU_DTYPE, True), ("ki", 256, MXU_DTYPE, False), ("whT", IDX_HEADS, F32, True))
    res = pl.pallas_call(
        _inproj_kernel,
        grid=(rows // tm,),
        in_specs=[pl.BlockSpec((tm, d), row), pl.BlockSpec((1, d), const),
                  pl.BlockSpec(w.shape, const, pipeline_mode=pl.Buffered(1)),
                  pl.BlockSpec(wuk.shape, const),
                  pl.BlockSpec(kvg.shape, const), pl.BlockSpec(lng.shape, const), pl.BlockSpec(lnb.shape, const)],
        out_specs=[pl.BlockSpec((n, tm), col) if t else pl.BlockSpec((tm, n), row) for _, n, _, t in outs],
        out_shape=[jax.ShapeDtypeStruct((n, rows) if t else (rows, n), dt) for _, n, dt, t in outs],
        compiler_params=_params("parallel"),
        name="in_proj",
    )(h, g, w, wuk, kvg, lng, lnb)
    return dict(zip([n for n, _, _, _ in outs], res))


def _shift_rows(x, s, fill, rows):
    return jnp.where(rows >= s, pltpu.roll(x, s, axis=0), fill)


def _seqmix_kernel(ff_ref, cu_ref, lx_ref, lg_ref, bf_ref, dww_ref, dwb_ref, lng_ref, lnb_ref,
                   lcw_ref, lcb_ref, wa_ref, ba_ref, wi_ref, bi_ref, lam_ref,
                   yc_ref, yl_ref, cumt_ref, kx_ref,
                   glu_buf, lx_buf, h_carry, cum_carry):
    t = pl.program_id(1)

    @pl.when(t == 0)
    def _():
        glu_buf[...] = jnp.zeros(glu_buf.shape, F32)
        lx_buf[...] = jnp.zeros(lx_buf.shape, F32)
        h_carry[...] = jnp.zeros(h_carry.shape, F32)
        cum_carry[...] = jnp.zeros(cum_carry.shape, F32)

    rows = lax.broadcasted_iota(jnp.int32, (BLOCK, 1), 0)
    valid = (t * BLOCK + rows) >= PAD

    cu = cu_ref[0]
    glu = jnp.where(valid, cu[:, :GROUP_W] * jax.nn.sigmoid(cu[:, GROUP_W:]), 0.0)
    glu_buf[CONV_HALO:, :] = glu
    acc = jnp.zeros((BLOCK, GROUP_W), F32) + dwb_ref[...]
    for k in range(CONV_WIDTH):
        lo = CONV_HALO - (CONV_WIDTH - 1) + k
        acc = acc + dww_ref[k:k + 1, :] * glu_buf[lo:lo + BLOCK, :]
    glu_buf[:CONV_HALO, :] = glu_buf[BLOCK:, :]
    mu = jnp.mean(acc, axis=-1, keepdims=True)
    var = jnp.mean(jnp.square(acc - mu), axis=-1, keepdims=True)
    hc = (acc - mu) * lax.rsqrt(var + LN_EPS) * lng_ref[...] + lnb_ref[...]
    yc_ref[0] = (hc * jax.nn.sigmoid(hc)).astype(yc_ref.dtype)

    lx_buf[LRU_HALO:, :] = jnp.where(valid, lx_ref[0], 0.0)
    xc = jnp.zeros((BLOCK, GROUP_W), F32) + lcb_ref[...]
    for k in range(LRU_CONV_WIDTH):
        lo = LRU_HALO - (LRU_CONV_WIDTH - 1) + k
        xc = xc + lcw_ref[k:k + 1, :] * lx_buf[lo:lo + BLOCK, :]
    lx_buf[:LRU_HALO, :] = lx_buf[BLOCK:, :]
    xcm = xc.astype(MXU_DTYPE)
    r = jax.nn.sigmoid(_dot(xcm, wa_ref[...]) + ba_ref[...])
    gi = jax.nn.sigmoid(_dot(xcm, wi_ref[...]) + bi_ref[...])
    log_a = LRU_C * r * _log_sigmoid(lam_ref[...])
    a = jnp.exp(log_a)
    u = jnp.where(valid, jnp.sqrt(-_expm1(2.0 * log_a)) * (gi * xc), 0.0)
    s = 1
    while s < BLOCK:
        u = a * _shift_rows(u, s, 0.0, rows) + u
        a = a * _shift_rows(a, s, 1.0, rows)
        s *= 2
    hl = u + a * h_carry[0:1, :]
    h_carry[...] = jnp.broadcast_to(hl[BLOCK - 1:BLOCK, :], h_carry.shape)
    g = lg_ref[0]
    gelu = 0.5 * g * (1.0 + jnp.tanh(np.sqrt(2.0 / np.pi).astype(np.float32) * (g + 0.044715 * g * g * g)))
    yl_ref[0] = (hl * gelu).astype(yl_ref.dtype)

    cs = _log_sigmoid(ff_ref[0] + bf_ref[...])
    s = 1
    while s < BLOCK:
        cs = cs + _shift_rows(cs, s, 0.0, rows)
        s *= 2
    cs = cs + cum_carry[0:1, :]
    cum_carry[...] = jnp.broadcast_to(cs[BLOCK - 1:BLOCK, :], cum_carry.shape)
    cumt_ref[0] = cs.T[:8, :]
    lanes = lax.broadcasted_iota(jnp.int32, (1, BLOCK), 1)
    ck = jnp.where(lanes < FOX_HEADS, jnp.where(valid, cs, -NEG), 0.0)
    hi = ck.astype(jnp.bfloat16).astype(F32)
    mid = (ck - hi).astype(jnp.bfloat16).astype(F32)
    low = (ck - hi - mid).astype(jnp.bfloat16).astype(F32)
    kx_ref[0] = (hi + pltpu.roll(mid, FOX_HEADS, axis=1) + pltpu.roll(low, 2 * FOX_HEADS, axis=1)).astype(kx_ref.dtype)


def _seqmix(z, bsz, tp, p):
    nblk = tp // BLOCK
    blk = lambda n: pl.BlockSpec((1, BLOCK, n), lambda b, t: (b, t, 0))
    const = lambda a: pl.BlockSpec(a.shape, lambda b, t: (0, 0))
    r3 = lambda a: a.reshape(bsz, tp, a.shape[-1])
    params = (p["fox_b_f"], p["conv_dw_w"], p["conv_dw_b"], p["conv_ln_g"], p["conv_ln_b"],
              p["lru_conv_w"], p["lru_conv_b"], p["lru_w_a"], p["lru_b_a"], p["lru_w_i"], p["lru_b_i"],
              p["lru_lambda"])
    return pl.pallas_call(
        _seqmix_kernel,
        grid=(bsz, nblk),
        in_specs=[blk(128), blk(512), blk(256), blk(256)] + [const(a) for a in params],
        out_specs=[blk(256), blk(256), pl.BlockSpec((1, 8, BLOCK), lambda b, t: (b, 0, t)), blk(128)],
        out_shape=[jax.ShapeDtypeStruct((bsz, tp, GROUP_W), MXU_DTYPE),
                   jax.ShapeDtypeStruct((bsz, tp, GROUP_W), MXU_DTYPE),
                   jax.ShapeDtypeStruct((bsz, 8, tp), F32),
                   jax.ShapeDtypeStruct((bsz, tp, 128), jnp.bfloat16)],
        scratch_shapes=[pltpu.VMEM((CONV_HALO + BLOCK, GROUP_W), F32),
                        pltpu.VMEM((LRU_HALO + BLOCK, GROUP_W), F32),
                        pltpu.VMEM((8, GROUP_W), F32),
                        pltpu.VMEM((8, 128), F32)],
        compiler_params=_params("parallel", "arbitrary"),
        name="seq_mix",
    )(r3(z["ff"]), r3(z["cu"]), r3(z["lx"]), r3(z["lg"]), *params)


def _fox_kernel(qt_ref, cqt_ref, k_ref, kx_ref, vt_ref, o_ref, sa_ref, sb_ref):
    qi = pl.program_id(1)
    pairs = FOX_HEADS // 2
    two = 2 * BLOCK
    qt = qt_ref[...]
    cqt = cqt_ref[0]
    row = lax.broadcasted_iota(jnp.int32, (BLOCK, 1), 0)
    k_off = lax.broadcasted_iota(jnp.int32, (KEY_STEP, 1), 0)
    q_pos = qi * BLOCK + lax.broadcasted_iota(jnp.int32, (1, BLOCK), 1)
    q_pos2 = jnp.concatenate([q_pos, q_pos], axis=1)
    q_rhs, cq = [], []
    for p in range(pairs):
        qp = qt[p * BLOCK:(p + 1) * BLOCK]
        halves, marks = [], []
        for c in range(2):
            h = 2 * p + c
            halves.append(jnp.where((row >= c * HEAD_DIM) & (row < (c + 1) * HEAD_DIM), qp, jnp.zeros_like(qp)))
            is_piece = (row == h) | (row == FOX_HEADS + h) | (row == 2 * FOX_HEADS + h)
            marks.append(jnp.broadcast_to(jnp.where(is_piece, -1.0, 0.0), (BLOCK, BLOCK)))
        q_rhs.append(jnp.concatenate([jnp.concatenate(halves, axis=1),
                                      jnp.concatenate(marks, axis=1).astype(MXU_DTYPE)], axis=0))
        cq.append(jnp.concatenate([cqt[2 * p:2 * p + 1], cqt[2 * p + 1:2 * p + 2]], axis=1))

    def qk(j, buf):
        ks = pl.multiple_of(j * KEY_STEP, KEY_STEP)
        kxb = kx_ref[0, pl.ds(ks, KEY_STEP), :]
        for p in range(pairs):
            keys = jnp.concatenate([k_ref[0, pl.ds(ks, KEY_STEP), p * BLOCK:(p + 1) * BLOCK], kxb], axis=1)
            buf[p] = _dot(keys, q_rhs[p])

    def update(j, buf, carry, causal_mask):
        ks = pl.multiple_of(j * KEY_STEP, KEY_STEP)
        out = []
        for p in range(pairs):
            m, l, acc = carry[p]
            s = buf[p] + cq[p]
            if causal_mask:
                s = jnp.where(ks + k_off <= q_pos2, s, NEG)
            m_new = jnp.maximum(m, jnp.max(s, axis=0, keepdims=True))
            alpha = jnp.exp(m - m_new)
            pr = jnp.exp(s - m_new)
            l = alpha * l + jnp.sum(pr, axis=0, keepdims=True)
            acc = alpha * acc + _dot(vt_ref[0, p * BLOCK:(p + 1) * BLOCK, pl.ds(ks, KEY_STEP)], pr.astype(MXU_DTYPE))
            out.append((m_new, l, acc))
        return tuple(out)

    init = tuple((jnp.full((1, two), NEG, F32), jnp.zeros((1, two), F32), jnp.zeros((BLOCK, two), F32))
                 for _ in range(pairs))
    n_full = (qi * BLOCK) // KEY_STEP

    qk(0, sa_ref)

    def pair_of_steps(t, carry):
        j = 2 * t
        qk(j + 1, sb_ref)
        carry = update(j, sa_ref, carry, False)
        qk(j + 2, sa_ref)
        return update(j + 1, sb_ref, carry, False)

    carry = lax.fori_loop(0, n_full // 2, pair_of_steps, init)

    def tail_two(carry):
        qk(n_full, sb_ref)
        return update(n_full, sb_ref, update(n_full - 1, sa_ref, carry, True), True)

    carry = lax.cond(n_full % 2 == 1, tail_two, lambda carry: update(n_full, sa_ref, carry, True), carry)
    outs = []
    for p in range(pairs):
        _, l, acc = carry[p]
        o_t = acc / l
        outs.append(jnp.where(row < HEAD_DIM, o_t[:, :BLOCK], o_t[:, BLOCK:]).T)
    o_ref[0] = jnp.concatenate(outs, axis=1).astype(o_ref.dtype)


def _fox(fqt, fk, kx, fvt, cumt):
    bsz, _, tp = cumt.shape
    nblk = tp // BLOCK
    full = lambda a: pl.BlockSpec((1,) + a.shape[1:], lambda b, i: (b, 0, 0))
    return pl.pallas_call(
        _fox_kernel,
        grid=(bsz, nblk),
        in_specs=[pl.BlockSpec((GROUP_W, BLOCK), lambda b, i: (0, b * nblk + i)),
                  pl.BlockSpec((1, 8, BLOCK), lambda b, i: (b, 0, i)), full(fk), full(kx), full(fvt)],
        out_specs=pl.BlockSpec((1, BLOCK, GROUP_W), lambda b, i: (b, i, 0)),
        out_shape=jax.ShapeDtypeStruct((bsz, tp, GROUP_W), MXU_DTYPE),
        scratch_shapes=[pltpu.VMEM((FOX_HEADS // 2, KEY_STEP, 2 * BLOCK), F32)] * 2,
        compiler_params=_params("parallel", "arbitrary"),
        name="fox",
    )(fqt, cumt, fk, kx, fvt)


def _dsa_kernel(iqt_ref, wht_ref, qlt_ref, ki_ref, c_ref, ct_ref, wuvt_ref, o_ref,
                sc_ref, sa_ref, sb_ref, da_ref, db_ref, *, topk):
    qi = pl.program_id(1)
    nkb = (qi * BLOCK) // KEY_STEP + 1
    k_off = lax.broadcasted_iota(jnp.int32, (KEY_STEP, 1), 0)
    q_pos = qi * BLOCK + lax.broadcasted_iota(jnp.int32, (1, BLOCK), 1)
    n_valid = q_pos - PAD + 1
    fold_rows = KEY_STEP // 8

    def fold(op, w):
        parts = [w[i * fold_rows:(i + 1) * fold_rows] for i in range(8)]
        return op(op(op(parts[0], parts[1]), op(parts[2], parts[3])), op(op(parts[4], parts[5]), op(parts[6], parts[7])))

    iqt = iqt_ref[...]
    wht = wht_ref[...]
    row_i = lax.broadcasted_iota(jnp.int32, (IDX_HEADS * IDX_DIM, 1), 0)
    q_heads = jnp.concatenate(
        [jnp.where((row_i >= h * IDX_DIM) & (row_i < (h + 1) * IDX_DIM), iqt, jnp.zeros_like(iqt))
         for h in range(IDX_HEADS)], axis=1)

    def head_dots(j, buf):
        ks = pl.multiple_of(j * KEY_STEP, KEY_STEP)
        buf[...] = _dot(ki_ref[0, pl.ds(ks, KEY_STEP), :], q_heads)

    def score_step(j, buf, stats):
        amax, s1, s2 = stats
        sc = jnp.zeros((KEY_STEP, BLOCK), F32)
        for h in range(IDX_HEADS):
            sc = sc + jnp.maximum(buf[:, h * BLOCK:(h + 1) * BLOCK], 0.0) * wht[h:h + 1, :]
        k_pos = j * KEY_STEP + k_off
        valid = (k_pos <= q_pos) & (k_pos >= PAD)
        sc_ref[j] = jnp.where(valid, sc, SCORE_MASKED)
        vs = jnp.where(valid, sc, 0.0)
        return (jnp.maximum(amax, fold(jnp.maximum, jnp.abs(vs))), s1 + fold(jnp.add, vs), s2 + fold(jnp.add, vs * vs))

    last = nkb - 1
    head_dots(0, da_ref)

    def score_pair(t, stats):
        j = 2 * t
        head_dots(j + 1, db_ref)
        stats = score_step(j, da_ref, stats)
        head_dots(jnp.minimum(j + 2, last), da_ref)
        return score_step(j + 1, db_ref, stats)

    stats = lax.fori_loop(0, nkb // 2, score_pair, (jnp.zeros((fold_rows, BLOCK), F32),) * 3)
    amax, s1, s2 = lax.cond(nkb % 2 == 1, lambda st: score_step(last, da_ref, st), lambda st: st, stats)
    bound = jnp.max(amax, axis=0, keepdims=True) * 1.0001 + 1e-30

    @pl.when(nkb % 2 == 1)
    def _():
        sc_ref[nkb] = jnp.full((KEY_STEP, BLOCK), SCORE_MASKED, F32)

    def count(*preds):
        def body(jj, accs):
            tiles = (sc_ref[2 * jj], sc_ref[2 * jj + 1])
            return tuple(tuple(acc + fold(jnp.add, jnp.where(pred(t), 1, 0)) for acc, t in zip(pair, tiles))
                         for pair, pred in zip(accs, preds))
        zeros = jnp.zeros((fold_rows, BLOCK), jnp.int32)
        accs = lax.fori_loop(0, (nkb + 1) // 2, body, tuple((zeros, zeros) for _ in preds))
        return [jnp.sum(a + b, axis=0, keepdims=True).astype(F32) for a, b in accs]

    kf = float(topk)
    take_all = n_valid <= topk
    c_ge0, c_gt0 = count(lambda s: s >= 0.0, lambda s: s > 0.0)
    positive = c_gt0 >= kf
    zero_tie = (c_ge0 >= kf) & jnp.logical_not(positive)
    lo = jnp.where(positive | zero_tie, 0.0, -bound)
    c_lo = jnp.where(positive | zero_tie, c_ge0, n_valid.astype(F32))
    hi = jnp.where(positive, bound, 0.0)
    c_hi = jnp.where(positive, 0.0, jnp.where(zero_tie, c_gt0, c_ge0))
    done = jnp.where(take_all | zero_tie | (c_lo == kf), 1, 0)

    n_f = jnp.maximum(n_valid, 1).astype(F32)
    mean = jnp.sum(s1, axis=0, keepdims=True) / n_f
    std = jnp.sqrt(jnp.maximum(jnp.sum(s2, axis=0, keepdims=True) / n_f - mean * mean, 0.0))
    tail = jnp.clip(kf / n_f, 1e-6, 1.0 - 1e-6)
    upper = tail < 0.5
    t_q = jnp.sqrt(-2.0 * jnp.log(jnp.where(upper, tail, 1.0 - tail)))
    z_q = t_q - ((0.010328 * t_q + 0.802853) * t_q + 2.515517) / (((0.001308 * t_q + 0.189269) * t_q + 1.432788) * t_q + 1.0)
    guess = mean + jnp.where(upper, z_q, -z_q) * std

    def next_probe(lo, hi, c_lo, c_hi, it):
        mid = 0.5 * lo + 0.5 * hi
        inside = (mid > lo) & (mid < hi)
        log_lo = jnp.log(c_lo)
        frac = (log_lo - np.log(kf)) / (log_lo - jnp.log(jnp.maximum(c_hi, 0.5)))
        probe = lo + (hi - lo) * jnp.clip(frac, 0.02, 0.98)
        turn = jnp.zeros_like(done) + it
        probe = jnp.where(turn == 0, guess, probe)
        use_probe = (turn % 4 != 3) & (probe > lo) & (probe < hi)
        return jnp.where(use_probe, probe, mid), jnp.where(inside, 1, 0)

    def search_cond(state):
        return state[-1] > 0

    def search_body(state):
        lo, hi, c_lo, c_hi, done, probe, inside, it, _ = state
        c, = count(lambda s: s >= probe)
        active = (done == 0) & (inside > 0)
        up = active & (c >= kf)
        down = active & (c < kf)
        lo, c_lo = jnp.where(up, probe, lo), jnp.where(up, c, c_lo)
        hi, c_hi = jnp.where(down, probe, hi), jnp.where(down, c, c_hi)
        done = jnp.where((done > 0) | (inside == 0) | (c_lo == kf), 1, 0)
        probe, inside = next_probe(lo, hi, c_lo, c_hi, it + 1)
        return lo, hi, c_lo, c_hi, done, probe, inside, it + 1, jnp.sum(1 - done)

    probe0, inside0 = next_probe(lo, hi, c_lo, c_hi, jnp.int32(0))
    thr, _, c_thr, c_above = lax.while_loop(
        search_cond, search_body,
        (lo, hi, c_lo, c_hi, done, probe0, inside0, jnp.int32(0), jnp.sum(1 - done)))[:4]
    thr = jnp.where(take_all, 0.5 * SCORE_MASKED, thr)
    tied = jnp.logical_not(take_all) & (c_thr > kf)
    any_tied = jnp.max(jnp.where(tied, 1, 0)) > 0

    qlt = qlt_ref[...]
    q_lat = jnp.concatenate([qlt[h * DSA_LATENT:(h + 1) * DSA_LATENT] for h in range(DSA_HEADS)], axis=1)
    wide = DSA_HEADS * BLOCK

    def attend(select):
        def qk(j, buf):
            ks = pl.multiple_of(j * KEY_STEP, KEY_STEP)
            buf[...] = _dot(c_ref[0, pl.ds(ks, KEY_STEP), :], q_lat)

        def consume(j, buf, carry):
            m, l, acc, run = carry
            ks = pl.multiple_of(j * KEY_STEP, KEY_STEP)
            sel, run = select(sc_ref[j], run)
            s_all = buf[...]
            s_all = jnp.concatenate([jnp.where(sel, s_all[:, h * BLOCK:(h + 1) * BLOCK], NEG)
                                     for h in range(DSA_HEADS)], axis=1)
            m_new = jnp.maximum(m, jnp.max(s_all, axis=0, keepdims=True))
            alpha = jnp.exp(m - m_new)
            pr = jnp.exp(s_all - m_new)
            l = alpha * l + jnp.sum(pr, axis=0, keepdims=True)
            acc = alpha * acc + _dot(ct_ref[0, :, pl.ds(ks, KEY_STEP)], pr.astype(MXU_DTYPE))
            return m_new, l, acc, run

        qk(0, sa_ref)

        def pair(t, carry):
            j = 2 * t
            qk(j + 1, sb_ref)
            carry = consume(j, sa_ref, carry)
            qk(jnp.minimum(j + 2, last), sa_ref)
            return consume(j + 1, sb_ref, carry)

        init = (jnp.full((1, wide), NEG, F32), jnp.zeros((1, wide), F32), jnp.zeros((DSA_LATENT, wide), F32),
                jnp.zeros((1, BLOCK), F32))
        carry = lax.fori_loop(0, nkb // 2, pair, init)
        _, l, acc, _ = lax.cond(nkb % 2 == 1, lambda c: consume(last, sa_ref, c), lambda c: c, carry)
        o_lat = (acc / jnp.where(l > 0.0, l, 1.0)).astype(MXU_DTYPE)
        y_t = jnp.zeros((GROUP_W, BLOCK), F32)
        for h in range(DSA_HEADS):
            y_t = y_t + _dot(wuvt_ref[:, h * DSA_LATENT:(h + 1) * DSA_LATENT], o_lat[:, h * BLOCK:(h + 1) * BLOCK])
        o_ref[0] = y_t.T.astype(o_ref.dtype)

    @pl.when(jnp.logical_not(any_tied))
    def _():
        attend(lambda s, run: (s >= thr, run))

    @pl.when(any_tied)
    def _():
        quota = jnp.where(tied, kf - c_above, 2.0 ** 30)
        r_i = lax.broadcasted_iota(jnp.int32, (KEY_STEP, KEY_STEP), 0)
        c_i = lax.broadcasted_iota(jnp.int32, (KEY_STEP, KEY_STEP), 1)
        tri = jnp.where(c_i <= r_i, 1.0, 0.0).astype(MXU_DTYPE)

        def select(s, run):
            eqf = jnp.where(s == thr, 1.0, 0.0)
            prefix = _dot(tri, eqf.astype(MXU_DTYPE)) + run
            within = jnp.where(prefix <= quota, eqf, 0.0)
            sel = (jnp.where(s > thr, 1.0, 0.0) + within) > 0.5
            return sel, run + jnp.sum(eqf, axis=0, keepdims=True)

        attend(select)


def _dsa(iqt, wht, qlt, ki, c, ct, wuvt, topk):
    bsz, tpk, _ = ki.shape
    tp = iqt.shape[1] // bsz
    nblk = tp // BLOCK
    qcol = lambda n: pl.BlockSpec((n, BLOCK), lambda b, i: (0, b * nblk + i))
    full = lambda a: pl.BlockSpec((1,) + a.shape[1:], lambda b, i: (b, 0, 0))
    return pl.pallas_call(
        functools.partial(_dsa_kernel, topk=topk),
        grid=(bsz, nblk),
        in_specs=[qcol(iqt.shape[0]), qcol(wht.shape[0]), qcol(qlt.shape[0]), full(ki), full(c), full(ct),
                  pl.BlockSpec(wuvt.shape, lambda b, i: (0, 0))],
        out_specs=pl.BlockSpec((1, BLOCK, GROUP_W), lambda b, i: (b, i, 0)),
        out_shape=jax.ShapeDtypeStruct((bsz, tp, GROUP_W), MXU_DTYPE),
        scratch_shapes=[pltpu.VMEM((tpk // KEY_STEP + 1, KEY_STEP, BLOCK), F32),
                        pltpu.VMEM((KEY_STEP, DSA_HEADS * BLOCK), F32), pltpu.VMEM((KEY_STEP, DSA_HEADS * BLOCK), F32),
                        pltpu.VMEM((KEY_STEP, IDX_HEADS * BLOCK), F32), pltpu.VMEM((KEY_STEP, IDX_HEADS * BLOCK), F32)],
        compiler_params=_params("parallel", "arbitrary"),
        name="dsa",
    )(iqt, wht, qlt, ki, c, ct, wuvt)


def _outproj_kernel(h_ref, yf_ref, yc_ref, yl_ref, yd_ref, w_ref, g_ref, o_ref, *, tm, tp):
    mix = _dot(yf_ref[...], w_ref[0:GROUP_W, :])
    mix = mix + _dot(yc_ref[...], w_ref[GROUP_W:2 * GROUP_W, :])
    mix = mix + _dot(yl_ref[...], w_ref[2 * GROUP_W:3 * GROUP_W, :])
    mix = mix + _dot(yd_ref[...], w_ref[3 * GROUP_W:4 * GROUP_W, :])
    row = (pl.program_id(0) * tm) % tp + lax.broadcasted_iota(jnp.int32, (tm, 1), 0)
    o_ref[...] = jnp.where(row >= PAD, h_ref[...] + _rms(mix, g_ref[...]), 0.0)


def _outproj(h, yf, yc, yl, yd, w, g, tp):
    rows, d = h.shape
    tm = _row_tile(rows)
    row = lambda i: (i, 0)
    const = lambda i: (0, 0)
    return pl.pallas_call(
        functools.partial(_outproj_kernel, tm=tm, tp=tp),
        grid=(rows // tm,),
        in_specs=[pl.BlockSpec((tm, d), row)] + [pl.BlockSpec((tm, GROUP_W), row)] * 4
                 + [pl.BlockSpec(w.shape, const), pl.BlockSpec((1, d), const)],
        out_specs=pl.BlockSpec((tm, d), row),
        out_shape=jax.ShapeDtypeStruct((rows, d), F32),
        compiler_params=_params("parallel"),
        name="out_proj",
    )(h, yf, yc, yl, yd, w, g)


def _pack_w_in(w_in):
    offs = np.cumsum((0,) + SPLIT_SIZES)
    fq, fk, fv, ff, cu, lx, lg, dq, dkv, iq, ik, iw = (w_in[..., offs[i]:offs[i + 1]] for i in range(12))
    padc = lambda a, n: jnp.pad(a, ((0, 0), (0, 0), (0, n - a.shape[-1])))
    cols = [fq * HEAD_DIM ** -0.5, fk, fv, padc(ff, 128), cu, lx, lg, dq, dkv, iq,
            jnp.tile(ik, (1, 1, IDX_HEADS)), padc(iw, 128)]
    return jnp.concatenate(cols, axis=-1).astype(MXU_DTYPE)


def _block_diag(w):
    depth, n, a, b = w.shape
    eye = jnp.eye(n, dtype=w.dtype)
    return jnp.einsum("lnab,nm->lnamb", w, eye).reshape(depth, n * a, n * b)


def _lane_pad(a, n):
    return jnp.pad(a, [(0, 0)] * (a.ndim - 1) + [(0, n - a.shape[-1])])


def kernel(x, meta_tokens, norm_g, ffn_w_in, ffn_w_out, w_in, w_out, fox_b_f, conv_dw_w, conv_dw_b, conv_ln_g,
           conv_ln_b, lru_conv_w, lru_conv_b, lru_w_a, lru_b_a, lru_w_i, lru_b_i, lru_lambda, dsa_kv_norm_g,
           dsa_w_uk, dsa_w_uv, idx_k_ln_g, idx_k_ln_b):
    bsz, seq, d = x.shape
    depth = norm_g.shape[0]
    assert seq % BLOCK == 0 and d % 128 == 0
    topk = min(TOPK_MAX, seq // 4)
    tp = PAD + N_META + seq
    rows = bsz * tp

    ffn_w_in_m = ffn_w_in.astype(MXU_DTYPE)
    ffn_w_out_m = ffn_w_out.astype(MXU_DTYPE)
    w_in_m = _pack_w_in(w_in)
    w_out_m = w_out.astype(MXU_DTYPE)
    wuk_m = (_block_diag(dsa_w_uk.transpose(0, 1, 3, 2)) * HEAD_DIM ** -0.5).astype(MXU_DTYPE)
    wuvt_m = _block_diag(dsa_w_uv).transpose(0, 2, 1).astype(MXU_DTYPE)
    wa_m = _block_diag(lru_w_a).astype(MXU_DTYPE)
    wi_m = _block_diag(lru_w_i).astype(MXU_DTYPE)
    row2 = lambda a: a[:, None, :]
    dww = jnp.pad(conv_dw_w, ((0, 0), (0, CONV_HALO - CONV_WIDTH), (0, 0)))
    lcw = jnp.pad(lru_conv_w, ((0, 0), (0, LRU_HALO - LRU_CONV_WIDTH), (0, 0)))
    ln_g8 = row2(jnp.tile(idx_k_ln_g, (1, IDX_HEADS)))
    ln_b8 = row2(jnp.tile(idx_k_ln_b, (1, IDX_HEADS)))

    meta = jnp.broadcast_to(meta_tokens[None].astype(x.dtype), (bsz, N_META, d))
    h = jnp.concatenate([jnp.zeros((bsz, PAD, d), x.dtype), meta, x], axis=1).reshape(rows, d)

    for l in range(depth):
        g = norm_g[l][:, None, :]
        h = _ffn(h, g[0], g[1], ffn_w_in_m[l, 0], ffn_w_out_m[l, 0])
        z = _inproj(h, g[2], w_in_m[l], wuk_m[l], row2(dsa_kv_norm_g)[l], ln_g8[l], ln_b8[l])
        seq_params = {
            "fox_b_f": _lane_pad(fox_b_f[l][None], 128), "conv_dw_w": dww[l], "conv_dw_b": row2(conv_dw_b)[l],
            "conv_ln_g": row2(conv_ln_g)[l], "conv_ln_b": row2(conv_ln_b)[l], "lru_conv_w": lcw[l],
            "lru_conv_b": row2(lru_conv_b)[l], "lru_w_a": wa_m[l], "lru_b_a": row2(lru_b_a)[l],
            "lru_w_i": wi_m[l], "lru_b_i": row2(lru_b_i)[l], "lru_lambda": row2(lru_lambda)[l]}
        y_conv, y_lru, cumt, kx = _seqmix(z, bsz, tp, seq_params)
        r3 = lambda a: a.reshape(bsz, tp, a.shape[-1])
        tpk = -(-tp // KEY_STEP) * KEY_STEP
        keys = lambda a: jnp.pad(r3(a), ((0, 0), (0, tpk - tp), (0, 0)))
        keys_t = lambda a: jnp.pad(a.reshape(a.shape[0], bsz, tp).transpose(1, 0, 2), ((0, 0), (0, 0), (0, tpk - tp)))
        y_fox = _fox(z["fqT"], keys(z["fk"]), keys(kx), keys_t(z["fvT"]), cumt)
        y_dsa = _dsa(z["iqT"], z["whT"], z["qlT"], keys(z["ki"]), keys(z["c"]), keys_t(z["cT"]), wuvt_m[l], topk)
        r2 = lambda a: a.reshape(rows, a.shape[-1])
        h = _outproj(h, r2(y_fox), r2(y_conv), r2(y_lru), r2(y_dsa), w_out_m[l], g[3], tp)
        h = _ffn(h, g[4], g[5], ffn_w_in_m[l, 1], ffn_w_out_m[l, 1])

    return h.reshape(bsz, tp, d)[:, PAD + N_META:]
```

```python
import functools

import jax
import jax.numpy as jnp
import numpy as np
from jax import lax
from jax.experimental import pallas as pl
from jax.experimental.pallas import tpu as pltpu

N_META = 16
BLOCK = 128
PAD = BLOCK - N_META
KEY_STEP = 512
GROUP_W = 256
HEAD_DIM = 64
FOX_HEADS = 4
CONV_WIDTH = 31
CONV_HALO = 32
LRU_BLOCKS = 4
LRU_CONV_WIDTH = 4
LRU_HALO = 8
LRU_C = 8.0
DSA_HEADS = 4
DSA_LATENT = 128
IDX_HEADS = 8
IDX_DIM = 32
TOPK_MAX = 256
RMS_EPS = 1e-6
LN_EPS = 1e-5
SPLIT_SIZES = (GROUP_W, GROUP_W, GROUP_W, FOX_HEADS, 2 * GROUP_W, GROUP_W, GROUP_W,
               DSA_HEADS * HEAD_DIM, DSA_LATENT, IDX_HEADS * IDX_DIM, IDX_DIM, IDX_HEADS)

MXU_DTYPE = jnp.bfloat16
F32 = jnp.float32
NEG = -1e30
SCORE_MASKED = -3e38
SEARCH_WARMUP = 12
VMEM_LIMIT = 56 * 1024 * 1024

_INPROJ_GROUPS = (("fq", 256), ("fk", 256), ("fv", 256), ("ff", 128), ("cu", 512), ("lx", 256),
                  ("lg", 256), ("dq", 256), ("dkv", 128), ("iq", 256), ("ik", 256), ("iw", 128))
_INPROJ_OFF = {}
_o = 0
for _n, _w in _INPROJ_GROUPS:
    _INPROJ_OFF[_n] = (_o, _w)
    _o += _w
INPROJ_COLS = _o


def _params(*sem):
    return pltpu.CompilerParams(dimension_semantics=sem, vmem_limit_bytes=VMEM_LIMIT)


def _rms(x, g):
    return x * lax.rsqrt(jnp.mean(x * x, axis=-1, keepdims=True) + RMS_EPS) * g


def _dot(a, b):
    return jnp.dot(a, b, preferred_element_type=F32)


def _log_sigmoid(x):
    return jnp.minimum(x, 0.0) - jnp.log1p(jnp.exp(-jnp.abs(x)))


def _expm1(y):
    e = jnp.exp(y)
    regular = (e != 1.0) & (e > 0.0)
    r = (e - 1.0) * y / jnp.log(jnp.where(regular, e, 2.0))
    return jnp.where(regular, r, jnp.where(e > 0.0, y, -1.0))


def _row_tile(rows):
    return 640 if rows % 640 == 0 else BLOCK


def _ffn_kernel(h_ref, gpre_ref, gpost_ref, win_ref, wout_ref, o_ref, *, d_ff, chunk):
    x = h_ref[...]
    xn = _rms(x, gpre_ref[...]).astype(MXU_DTYPE)
    acc = jnp.zeros(x.shape, F32)
    for c in range(d_ff // chunk):
        gate = _dot(xn, win_ref[:, c * chunk:(c + 1) * chunk])
        up = _dot(xn, win_ref[:, d_ff + c * chunk:d_ff + (c + 1) * chunk])
        a = (gate * jax.nn.sigmoid(gate) * up).astype(MXU_DTYPE)
        acc = acc + _dot(a, wout_ref[c * chunk:(c + 1) * chunk, :])
    o_ref[...] = x + 0.5 * _rms(acc, gpost_ref[...])


def _ffn(h, g_pre, g_post, w_in, w_out):
    rows, d = h.shape
    d_ff = w_out.shape[0]
    tm = _row_tile(rows)
    chunk = 512 if d_ff % 512 == 0 else d_ff
    const = lambda i: (0, 0)
    return pl.pallas_call(
        functools.partial(_ffn_kernel, d_ff=d_ff, chunk=chunk),
        grid=(rows // tm,),
        in_specs=[pl.BlockSpec((tm, d), lambda i: (i, 0)),
                  pl.BlockSpec((1, d), const), pl.BlockSpec((1, d), const),
                  pl.BlockSpec(w_in.shape, const, pipeline_mode=pl.Buffered(1)),
                  pl.BlockSpec(w_out.shape, const, pipeline_mode=pl.Buffered(1))],
        out_specs=pl.BlockSpec((tm, d), lambda i: (i, 0)),
        out_shape=jax.ShapeDtypeStruct((rows, d), F32),
        compiler_params=_params("parallel"),
        name="ffn",
    )(h, g_pre, g_post, w_in, w_out)


def _inproj_kernel(h_ref, g_ref, w_ref, wuk_ref, kvg_ref, lng_ref, lnb_ref,
                   fqt_ref, fk_ref, fvt_ref, ff_ref, cu_ref, lx_ref, lg_ref,
                   qlt_ref, c_ref, ct_ref, iqt_ref, ki_ref, wht_ref):
    xn = _rms(h_ref[...], g_ref[...]).astype(MXU_DTYPE)

    def proj(name):
        lo, n = _INPROJ_OFF[name]
        return _dot(xn, w_ref[:, lo:lo + n])

    fqt_ref[...] = proj("fq").T.astype(MXU_DTYPE)
    fk_ref[...] = proj("fk").astype(MXU_DTYPE)
    fvt_ref[...] = proj("fv").T.astype(MXU_DTYPE)
    ff_ref[...] = proj("ff")
    cu_ref[...] = proj("cu")
    lx_ref[...] = proj("lx")
    lg_ref[...] = proj("lg")
    qlt_ref[...] = _dot(proj("dq").astype(MXU_DTYPE), wuk_ref[...]).T.astype(MXU_DTYPE)
    c = _rms(proj("dkv"), kvg_ref[...])
    c_ref[...] = c.astype(MXU_DTYPE)
    ct_ref[...] = c.T.astype(MXU_DTYPE)
    iqt_ref[...] = proj("iq").T.astype(MXU_DTYPE)
    ik = proj("ik")
    mu = jnp.mean(ik, axis=-1, keepdims=True)
    var = jnp.mean(jnp.square(ik - mu), axis=-1, keepdims=True)
    ki_ref[...] = ((ik - mu) * lax.rsqrt(var + LN_EPS) * lng_ref[...] + lnb_ref[...]).astype(MXU_DTYPE)
    wht_ref[...] = (proj("iw") * (IDX_HEADS ** -0.5 * IDX_DIM ** -0.5)).T[:IDX_HEADS, :]


def _inproj(h, g, w, wuk, kvg, lng, lnb):
    rows, d = h.shape
    tm = _row_tile(rows)
    const = lambda i: (0, 0)
    row = lambda i: (i, 0)
    col = lambda i: (0, i)
    outs = (("fqT", 256, MXU_DTYPE, True), ("fk", 256, MXU_DTYPE, False), ("fvT", 256, MXU_DTYPE, True),
            ("ff", 128, F32, False), ("cu", 512, F32, False), ("lx", 256, F32, False), ("lg", 256, F32, False),
            ("qlT", 512, MXU_DTYPE, True), ("c", 128, MXU_DTYPE, False), ("cT", 128, MXU_DTYPE, True),
            ("iqT", 256, MXU_DTYPE, True), ("ki", 256, MXU_DTYPE, False), ("whT", IDX_HEADS, F32, True))
    res = pl.pallas_call(
        _inproj_kernel,
        grid=(rows // tm,),
        in_specs=[pl.BlockSpec((tm, d), row), pl.BlockSpec((1, d), const),
                  pl.BlockSpec(w.shape, const, pipeline_mode=pl.Buffered(1)),
                  pl.BlockSpec(wuk.shape, const),
                  pl.BlockSpec(kvg.shape, const), pl.BlockSpec(lng.shape, const), pl.BlockSpec(lnb.shape, const)],
        out_specs=[pl.BlockSpec((n, tm), col) if t else pl.BlockSpec((tm, n), row) for _, n, _, t in outs],
        out_shape=[jax.ShapeDtypeStruct((n, rows) if t else (rows, n), dt) for _, n, dt, t in outs],
        compiler_params=_params("parallel"),
        name="in_proj",
    )(h, g, w, wuk, kvg, lng, lnb)
    return dict(zip([n for n, _, _, _ in outs], res))


def _shift_rows(x, s, fill, rows):
    return jnp.where(rows >= s, pltpu.roll(x, s, axis=0), fill)


def _seqmix_kernel(ff_ref, cu_ref, lx_ref, lg_ref, bf_ref, dww_ref, dwb_ref, lng_ref, lnb_ref,
                   lcw_ref, lcb_ref, wa_ref, ba_ref, wi_ref, bi_ref, lam_ref,
                   yc_ref, yl_ref, cumt_ref, kx_ref,
                   glu_buf, lx_buf, h_carry, cum_carry):
    t = pl.program_id(1)

    @pl.when(t == 0)
    def _():
        glu_buf[...] = jnp.zeros(glu_buf.shape, F32)
        lx_buf[...] = jnp.zeros(lx_buf.shape, F32)
        h_carry[...] = jnp.zeros(h_carry.shape, F32)
        cum_carry[...] = jnp.zeros(cum_carry.shape, F32)

    rows = lax.broadcasted_iota(jnp.int32, (BLOCK, 1), 0)
    valid = (t * BLOCK + rows) >= PAD

    cu = cu_ref[0]
    glu = jnp.where(valid, cu[:, :GROUP_W] * jax.nn.sigmoid(cu[:, GROUP_W:]), 0.0)
    glu_buf[CONV_HALO:, :] = glu
    acc = jnp.zeros((BLOCK, GROUP_W), F32) + dwb_ref[...]
    for k in range(CONV_WIDTH):
        lo = CONV_HALO - (CONV_WIDTH - 1) + k
        acc = acc + dww_ref[k:k + 1, :] * glu_buf[lo:lo + BLOCK, :]
    glu_buf[:CONV_HALO, :] = glu_buf[BLOCK:, :]
    mu = jnp.mean(acc, axis=-1, keepdims=True)
    var = jnp.mean(jnp.square(acc - mu), axis=-1, keepdims=True)
    hc = (acc - mu) * lax.rsqrt(var + LN_EPS) * lng_ref[...] + lnb_ref[...]
    yc_ref[0] = (hc * jax.nn.sigmoid(hc)).astype(yc_ref.dtype)

    lx_buf[LRU_HALO:, :] = jnp.where(valid, lx_ref[0], 0.0)
    xc = jnp.zeros((BLOCK, GROUP_W), F32) + lcb_ref[...]
    for k in range(LRU_CONV_WIDTH):
        lo = LRU_HALO - (LRU_CONV_WIDTH - 1) + k
        xc = xc + lcw_ref[k:k + 1, :] * lx_buf[lo:lo + BLOCK, :]
    lx_buf[:LRU_HALO, :] = lx_buf[BLOCK:, :]
    xcm = xc.astype(MXU_DTYPE)
    r = jax.nn.sigmoid(_dot(xcm, wa_ref[...]) + ba_ref[...])
    gi = jax.nn.sigmoid(_dot(xcm, wi_ref[...]) + bi_ref[...])
    log_a = LRU_C * r * _log_sigmoid(lam_ref[...])
    a = jnp.exp(log_a)
    u = jnp.where(valid, jnp.sqrt(-_expm1(2.0 * log_a)) * (gi * xc), 0.0)
    s = 1
    while s < BLOCK:
        u = a * _shift_rows(u, s, 0.0, rows) + u
        a = a * _shift_rows(a, s, 1.0, rows)
        s *= 2
    hl = u + a * h_carry[0:1, :]
    h_carry[...] = jnp.broadcast_to(hl[BLOCK - 1:BLOCK, :], h_carry.shape)
    g = lg_ref[0]
    gelu = 0.5 * g * (1.0 + jnp.tanh(np.sqrt(2.0 / np.pi).astype(np.float32) * (g + 0.044715 * g * g * g)))
    yl_ref[0] = (hl * gelu).astype(yl_ref.dtype)

    cs = _log_sigmoid(ff_ref[0] + bf_ref[...])
    s = 1
    while s < BLOCK:
        cs = cs + _shift_rows(cs, s, 0.0, rows)
        s *= 2
    cs = cs + cum_carry[0:1, :]
    cum_carry[...] = jnp.broadcast_to(cs[BLOCK - 1:BLOCK, :], cum_carry.shape)
    cumt_ref[0] = cs.T[:8, :]
    lanes = lax.broadcasted_iota(jnp.int32, (1, BLOCK), 1)
    ck = jnp.where(lanes < FOX_HEADS, jnp.where(valid, cs, -NEG), 0.0)
    hi = ck.astype(jnp.bfloat16).astype(F32)
    mid = (ck - hi).astype(jnp.bfloat16).astype(F32)
    low = (ck - hi - mid).astype(jnp.bfloat16).astype(F32)
    kx_ref[0] = (hi + pltpu.roll(mid, FOX_HEADS, axis=1) + pltpu.roll(low, 2 * FOX_HEADS, axis=1)).astype(kx_ref.dtype)


def _seqmix(z, bsz, tp, p):
    nblk = tp // BLOCK
    blk = lambda n: pl.BlockSpec((1, BLOCK, n), lambda b, t: (b, t, 0))
    const = lambda a: pl.BlockSpec(a.shape, lambda b, t: (0, 0))
    r3 = lambda a: a.reshape(bsz, tp, a.shape[-1])
    params = (p["fox_b_f"], p["conv_dw_w"], p["conv_dw_b"], p["conv_ln_g"], p["conv_ln_b"],
              p["lru_conv_w"], p["lru_conv_b"], p["lru_w_a"], p["lru_b_a"], p["lru_w_i"], p["lru_b_i"],
              p["lru_lambda"])
    return pl.pallas_call(
        _seqmix_kernel,
        grid=(bsz, nblk),
        in_specs=[blk(128), blk(512), blk(256), blk(256)] + [const(a) for a in params],
        out_specs=[blk(256), blk(256), pl.BlockSpec((1, 8, BLOCK), lambda b, t: (b, 0, t)), blk(128)],
        out_shape=[jax.ShapeDtypeStruct((bsz, tp, GROUP_W), MXU_DTYPE),
                   jax.ShapeDtypeStruct((bsz, tp, GROUP_W), MXU_DTYPE),
                   jax.ShapeDtypeStruct((bsz, 8, tp), F32),
                   jax.ShapeDtypeStruct((bsz, tp, 128), jnp.bfloat16)],
        scratch_shapes=[pltpu.VMEM((CONV_HALO + BLOCK, GROUP_W), F32),
                        pltpu.VMEM((LRU_HALO + BLOCK, GROUP_W), F32),
                        pltpu.VMEM((8, GROUP_W), F32),
                        pltpu.VMEM((8, 128), F32)],
        compiler_params=_params("parallel", "arbitrary"),
        name="seq_mix",
    )(r3(z["ff"]), r3(z["cu"]), r3(z["lx"]), r3(z["lg"]), *params)


def _fox_kernel(qt_ref, cqt_ref, k_ref, kx_ref, vt_ref, o_ref, sa_ref, sb_ref):
    qi = pl.program_id(1)
    pairs = FOX_HEADS // 2
    two = 2 * BLOCK
    qt = qt_ref[...]
    cqt = cqt_ref[0]
    row = lax.broadcasted_iota(jnp.int32, (BLOCK, 1), 0)
    k_off = lax.broadcasted_iota(jnp.int32, (KEY_STEP, 1), 0)
    q_pos = qi * BLOCK + lax.broadcasted_iota(jnp.int32, (1, BLOCK), 1)
    q_pos2 = jnp.concatenate([q_pos, q_pos], axis=1)
    q_rhs, cq = [], []
    for p in range(pairs):
        qp = qt[p * BLOCK:(p + 1) * BLOCK]
        halves, marks = [], []
        for c in range(2):
            h = 2 * p + c
            halves.append(jnp.where((row >= c * HEAD_DIM) & (row < (c + 1) * HEAD_DIM), qp, jnp.zeros_like(qp)))
            is_piece = (row == h) | (row == FOX_HEADS + h) | (row == 2 * FOX_HEADS + h)
            marks.append(jnp.broadcast_to(jnp.where(is_piece, -1.0, 0.0), (BLOCK, BLOCK)))
        q_rhs.append(jnp.concatenate([jnp.concatenate(halves, axis=1),
                                      jnp.concatenate(marks, axis=1).astype(MXU_DTYPE)], axis=0))
        cq.append(jnp.concatenate([cqt[2 * p:2 * p + 1], cqt[2 * p + 1:2 * p + 2]], axis=1))

    def qk(j, buf):
        ks = pl.multiple_of(j * KEY_STEP, KEY_STEP)
        kxb = kx_ref[0, pl.ds(ks, KEY_STEP), :]
        for p in range(pairs):
            keys = jnp.concatenate([k_ref[0, pl.ds(ks, KEY_STEP), p * BLOCK:(p + 1) * BLOCK], kxb], axis=1)
            buf[p] = _dot(keys, q_rhs[p])

    def update(j, buf, carry, causal_mask):
        ks = pl.multiple_of(j * KEY_STEP, KEY_STEP)
        out = []
        for p in range(pairs):
            m, l, acc = carry[p]
            s = buf[p] + cq[p]
            if causal_mask:
                s = jnp.where(ks + k_off <= q_pos2, s, NEG)
            m_new = jnp.maximum(m, jnp.max(s, axis=0, keepdims=True))
            alpha = jnp.exp(m - m_new)
            pr = jnp.exp(s - m_new)
            l = alpha * l + jnp.sum(pr, axis=0, keepdims=True)
            acc = alpha * acc + _dot(vt_ref[0, p * BLOCK:(p + 1) * BLOCK, pl.ds(ks, KEY_STEP)], pr.astype(MXU_DTYPE))
            out.append((m_new, l, acc))
        return tuple(out)

    init = tuple((jnp.full((1, two), NEG, F32), jnp.zeros((1, two), F32), jnp.zeros((BLOCK, two), F32))
                 for _ in range(pairs))
    n_full = (qi * BLOCK) // KEY_STEP

    qk(0, sa_ref)

    def pair_of_steps(t, carry):
        j = 2 * t
        qk(j + 1, sb_ref)
        carry = update(j, sa_ref, carry, False)
        qk(j + 2, sa_ref)
        return update(j + 1, sb_ref, carry, False)

    carry = lax.fori_loop(0, n_full // 2, pair_of_steps, init)

    def tail_two(carry):
        qk(n_full, sb_ref)
        return update(n_full, sb_ref, update(n_full - 1, sa_ref, carry, True), True)

    carry = lax.cond(n_full % 2 == 1, tail_two, lambda carry: update(n_full, sa_ref, carry, True), carry)
    outs = []
    for p in range(pairs):
        _, l, acc = carry[p]
        o_t = acc / l
        outs.append(jnp.where(row < HEAD_DIM, o_t[:, :BLOCK], o_t[:, BLOCK:]).T)
    o_ref[0] = jnp.concatenate(outs, axis=1).astype(o_ref.dtype)


def _fox(fqt, fk, kx, fvt, cumt):
    bsz, _, tp = cumt.shape
    nblk = tp // BLOCK
    full = lambda a: pl.BlockSpec((1,) + a.shape[1:], lambda b, i: (b, 0, 0))
    return pl.pallas_call(
        _fox_kernel,
        grid=(bsz, nblk),
        in_specs=[pl.BlockSpec((GROUP_W, BLOCK), lambda b, i: (0, b * nblk + i)),
                  pl.BlockSpec((1, 8, BLOCK), lambda b, i: (b, 0, i)), full(fk), full(kx), full(fvt)],
        out_specs=pl.BlockSpec((1, BLOCK, GROUP_W), lambda b, i: (b, i, 0)),
        out_shape=jax.ShapeDtypeStruct((bsz, tp, GROUP_W), MXU_DTYPE),
        scratch_shapes=[pltpu.VMEM((FOX_HEADS // 2, KEY_STEP, 2 * BLOCK), F32)] * 2,
        compiler_params=_params("parallel", "arbitrary"),
        name="fox",
    )(fqt, cumt, fk, kx, fvt)


def _dsa_kernel(iqt_ref, wht_ref, qlt_ref, ki_ref, c_ref, ct_ref, wuvt_ref, o_ref,
                sc_ref, sa_ref, sb_ref, da_ref, db_ref, *, topk):
    qi = pl.program_id(1)
    nkb = (qi * BLOCK) // KEY_STEP + 1
    k_off = lax.broadcasted_iota(jnp.int32, (KEY_STEP, 1), 0)
    q_pos = qi * BLOCK + lax.broadcasted_iota(jnp.int32, (1, BLOCK), 1)
    n_valid = q_pos - PAD + 1
    fold_rows = KEY_STEP // 8

    def fold(op, w):
        parts = [w[i * fold_rows:(i + 1) * fold_rows] for i in range(8)]
        return op(op(op(parts[0], parts[1]), op(parts[2], parts[3])), op(op(parts[4], parts[5]), op(parts[6], parts[7])))

    iqt = iqt_ref[...]
    wht = wht_ref[...]
    row_i = lax.broadcasted_iota(jnp.int32, (IDX_HEADS * IDX_DIM, 1), 0)
    q_heads = jnp.concatenate(
        [jnp.where((row_i >= h * IDX_DIM) & (row_i < (h + 1) * IDX_DIM), iqt, jnp.zeros_like(iqt))
         for h in range(IDX_HEADS)], axis=1)

    def head_dots(j, buf):
        ks = pl.multiple_of(j * KEY_STEP, KEY_STEP)
        buf[...] = _dot(ki_ref[0, pl.ds(ks, KEY_STEP), :], q_heads)

    def score_step(j, buf, stats):
        amax, s1, s2 = stats
        sc = jnp.zeros((KEY_STEP, BLOCK), F32)
        for h in range(IDX_HEADS):
            sc = sc + jnp.maximum(buf[:, h * BLOCK:(h + 1) * BLOCK], 0.0) * wht[h:h + 1, :]
        k_pos = j * KEY_STEP + k_off
        valid = (k_pos <= q_pos) & (k_pos >= PAD)
        sc_ref[j] = jnp.where(valid, sc, SCORE_MASKED)
        vs = jnp.where(valid, sc, 0.0)
        return (jnp.maximum(amax, fold(jnp.maximum, jnp.abs(vs))), s1 + fold(jnp.add, vs), s2 + fold(jnp.add, vs * vs))

    last = nkb - 1
    head_dots(0, da_ref)

    def score_pair(t, stats):
        j = 2 * t
        head_dots(j + 1, db_ref)
        stats = score_step(j, da_ref, stats)
        head_dots(jnp.minimum(j + 2, last), da_ref)
        return score_step(j + 1, db_ref, stats)

    stats = lax.fori_loop(0, nkb // 2, score_pair, (jnp.zeros((fold_rows, BLOCK), F32),) * 3)
    amax, s1, s2 = lax.cond(nkb % 2 == 1, lambda st: score_step(last, da_ref, st), lambda st: st, stats)
    bound = jnp.max(amax, axis=0, keepdims=True) * 1.0001 + 1e-30

    @pl.when(nkb % 2 == 1)
    def _():
        sc_ref[nkb] = jnp.full((KEY_STEP, BLOCK), SCORE_MASKED, F32)

    def count(*preds):
        def body(jj, accs):
            tiles = (sc_ref[2 * jj], sc_ref[2 * jj + 1])
            return tuple(tuple(acc + fold(jnp.add, jnp.where(pred(t), 1, 0)) for acc, t in zip(pair, tiles))
                         for pair, pred in zip(accs, preds))
        zeros = jnp.zeros((fold_rows, BLOCK), jnp.int32)
        accs = lax.fori_loop(0, (nkb + 1) // 2, body, tuple((zeros, zeros) for _ in preds))
        return [jnp.sum(a + b, axis=0, keepdims=True).astype(F32) for a, b in accs]

    kf = float(topk)
    take_all = n_valid <= topk
    c_ge0, c_gt0 = count(lambda s: s >= 0.0, lambda s: s > 0.0)
    positive = c_gt0 >= kf
    zero_tie = (c_ge0 >= kf) & jnp.logical_not(positive)
    lo = jnp.where(positive | zero_tie, 0.0, -bound)
    c_lo = jnp.where(positive | zero_tie, c_ge0, n_valid.astype(F32))
    hi = jnp.where(positive, bound, 0.0)
    c_hi = jnp.where(positive, 0.0, jnp.where(zero_tie, c_gt0, c_ge0))
    done = jnp.where(take_all | zero_tie | (c_lo == kf), 1, 0)

    n_f = jnp.maximum(n_valid, 1).astype(F32)
    mean = jnp.sum(s1, axis=0, keepdims=True) / n_f
    std = jnp.sqrt(jnp.maximum(jnp.sum(s2, axis=0, keepdims=True) / n_f - mean * mean, 0.0))
    tail = jnp.clip(kf / n_f, 1e-6, 1.0 - 1e-6)
    upper = tail < 0.5
    t_q = jnp.sqrt(-2.0 * jnp.log(jnp.where(upper, tail, 1.0 - tail)))
    z_q = t_q - ((0.010328 * t_q + 0.802853) * t_q + 2.515517) / (((0.001308 * t_q + 0.189269) * t_q + 1.432788) * t_q + 1.0)
    guess = mean + jnp.where(upper, z_q, -z_q) * std

    def next_probe(lo, hi, c_lo, c_hi, it):
        mid = 0.5 * lo + 0.5 * hi
        inside = (mid > lo) & (mid < hi)
        log_lo = jnp.log(c_lo)
        frac = (log_lo - np.log(kf)) / (log_lo - jnp.log(jnp.maximum(c_hi, 0.5)))
        probe = lo + (hi - lo) * jnp.clip(frac, 0.02, 0.98)
        turn = jnp.zeros_like(done) + it
        probe = jnp.where(turn == 0, guess, probe)
        use_probe = (turn % 4 != 3) & (probe > lo) & (probe < hi)
        return jnp.where(use_probe, probe, mid), jnp.where(inside, 1, 0)

    def search_step(state):
        lo, hi, c_lo, c_hi, done, probe, inside, it = state
        c, = count(lambda s: s >= probe)
        active = (done == 0) & (inside > 0)
        up = active & (c >= kf)
        down = active & (c < kf)
        lo, c_lo = jnp.where(up, probe, lo), jnp.where(up, c, c_lo)
        hi, c_hi = jnp.where(down, probe, hi), jnp.where(down, c, c_hi)
        done = jnp.where((done > 0) | (inside == 0) | (c_lo == kf), 1, 0)
        probe, inside = next_probe(lo, hi, c_lo, c_hi, it + 1)
        return lo, hi, c_lo, c_hi, done, probe, inside, it + 1

    def unfinished(state):
        return jnp.sum(1 - state[4])

    probe0, inside0 = next_probe(lo, hi, c_lo, c_hi, jnp.int32(0))
    state = (lo, hi, c_lo, c_hi, done, probe0, inside0, jnp.int32(0))
    state = lax.fori_loop(0, SEARCH_WARMUP, lambda _, st: search_step(st), state)

    def search_body(carry):
        state = search_step(carry[0])
        return state, unfinished(state)

    state, _ = lax.while_loop(lambda carry: carry[1] > 0, search_body, (state, unfinished(state)))
    thr, _, c_thr, c_above = state[:4]
    thr = jnp.where(take_all, 0.5 * SCORE_MASKED, thr)
    tied = jnp.logical_not(take_all) & (c_thr > kf)
    any_tied = jnp.max(jnp.where(tied, 1, 0)) > 0

    qlt = qlt_ref[...]
    q_lat = jnp.concatenate([qlt[h * DSA_LATENT:(h + 1) * DSA_LATENT] for h in range(DSA_HEADS)], axis=1)
    wide = DSA_HEADS * BLOCK

    def attend(select):
        def qk(j, buf):
            ks = pl.multiple_of(j * KEY_STEP, KEY_STEP)
            buf[...] = _dot(c_ref[0, pl.ds(ks, KEY_STEP), :], q_lat)

        def consume(j, buf, carry):
            m, l, acc, run = carry
            ks = pl.multiple_of(j * KEY_STEP, KEY_STEP)
            sel, run = select(sc_ref[j], run)
            s_all = buf[...]
            s_all = jnp.concatenate([jnp.where(sel, s_all[:, h * BLOCK:(h + 1) * BLOCK], NEG)
                                     for h in range(DSA_HEADS)], axis=1)
            m_new = jnp.maximum(m, jnp.max(s_all, axis=0, keepdims=True))
            alpha = jnp.exp(m - m_new)
            pr = jnp.exp(s_all - m_new)
            l = alpha * l + jnp.sum(pr, axis=0, keepdims=True)
            acc = alpha * acc + _dot(ct_ref[0, :, pl.ds(ks, KEY_STEP)], pr.astype(MXU_DTYPE))
            return m_new, l, acc, run

        qk(0, sa_ref)

        def pair(t, carry):
            j = 2 * t
            qk(j + 1, sb_ref)
            carry = consume(j, sa_ref, carry)
            qk(jnp.minimum(j + 2, last), sa_ref)
            return consume(j + 1, sb_ref, carry)

        init = (jnp.full((1, wide), NEG, F32), jnp.zeros((1, wide), F32), jnp.zeros((DSA_LATENT, wide), F32),
                jnp.zeros((1, BLOCK), F32))
        carry = lax.fori_loop(0, nkb // 2, pair, init)
        _, l, acc, _ = lax.cond(nkb % 2 == 1, lambda c: consume(last, sa_ref, c), lambda c: c, carry)
        o_lat = (acc / jnp.where(l > 0.0, l, 1.0)).astype(MXU_DTYPE)
        y_t = jnp.zeros((GROUP_W, BLOCK), F32)
        for h in range(DSA_HEADS):
            y_t = y_t + _dot(wuvt_ref[:, h * DSA_LATENT:(h + 1) * DSA_LATENT], o_lat[:, h * BLOCK:(h + 1) * BLOCK])
        o_ref[0] = y_t.T.astype(o_ref.dtype)

    @pl.when(jnp.logical_not(any_tied))
    def _():
        attend(lambda s, run: (s >= thr, run))

    @pl.when(any_tied)
    def _():
        quota = jnp.where(tied, kf - c_above, 2.0 ** 30)
        r_i = lax.broadcasted_iota(jnp.int32, (KEY_STEP, KEY_STEP), 0)
        c_i = lax.broadcasted_iota(jnp.int32, (KEY_STEP, KEY_STEP), 1)
        tri = jnp.where(c_i <= r_i, 1.0, 0.0).astype(MXU_DTYPE)

        def select(s, run):
            eqf = jnp.where(s == thr, 1.0, 0.0)
            prefix = _dot(tri, eqf.astype(MXU_DTYPE)) + run
            within = jnp.where(prefix <= quota, eqf, 0.0)
            sel = (jnp.where(s > thr, 1.0, 0.0) + within) > 0.5
            return sel, run + jnp.sum(eqf, axis=0, keepdims=True)

        attend(select)


def _dsa(iqt, wht, qlt, ki, c, ct, wuvt, topk):
    bsz, tpk, _ = ki.shape
    tp = iqt.shape[1] // bsz
    nblk = tp // BLOCK
    qcol = lambda n: pl.BlockSpec((n, BLOCK), lambda b, i: (0, b * nblk + i))
    full = lambda a: pl.BlockSpec((1,) + a.shape[1:], lambda b, i: (b, 0, 0))
    return pl.pallas_call(
        functools.partial(_dsa_kernel, topk=topk),
        grid=(bsz, nblk),
        in_specs=[qcol(iqt.shape[0]), qcol(wht.shape[0]), qcol(qlt.shape[0]), full(ki), full(c), full(ct),
                  pl.BlockSpec(wuvt.shape, lambda b, i: (0, 0))],
        out_specs=pl.BlockSpec((1, BLOCK, GROUP_W), lambda b, i: (b, i, 0)),
        out_shape=jax.ShapeDtypeStruct((bsz, tp, GROUP_W), MXU_DTYPE),
        scratch_shapes=[pltpu.VMEM((tpk // KEY_STEP + 1, KEY_STEP, BLOCK), F32),
                        pltpu.VMEM((KEY_STEP, DSA_HEADS * BLOCK), F32), pltpu.VMEM((KEY_STEP, DSA_HEADS * BLOCK), F32),
                        pltpu.VMEM((KEY_STEP, IDX_HEADS * BLOCK), F32), pltpu.VMEM((KEY_STEP, IDX_HEADS * BLOCK), F32)],
        compiler_params=_params("parallel", "arbitrary"),
        name="dsa",
    )(iqt, wht, qlt, ki, c, ct, wuvt)


def _outproj_kernel(h_ref, yf_ref, yc_ref, yl_ref, yd_ref, w_ref, g_ref, o_ref, *, tm, tp):
    mix = _dot(yf_ref[...], w_ref[0:GROUP_W, :])
    mix = mix + _dot(yc_ref[...], w_ref[GROUP_W:2 * GROUP_W, :])
    mix = mix + _dot(yl_ref[...], w_ref[2 * GROUP_W:3 * GROUP_W, :])
    mix = mix + _dot(yd_ref[...], w_ref[3 * GROUP_W:4 * GROUP_W, :])
    row = (pl.program_id(0) * tm) % tp + lax.broadcasted_iota(jnp.int32, (tm, 1), 0)
    o_ref[...] = jnp.where(row >= PAD, h_ref[...] + _rms(mix, g_ref[...]), 0.0)


def _outproj(h, yf, yc, yl, yd, w, g, tp):
    rows, d = h.shape
    tm = _row_tile(rows)
    row = lambda i: (i, 0)
    const = lambda i: (0, 0)
    return pl.pallas_call(
        functools.partial(_outproj_kernel, tm=tm, tp=tp),
        grid=(rows // tm,),
        in_specs=[pl.BlockSpec((tm, d), row)] + [pl.BlockSpec((tm, GROUP_W), row)] * 4
                 + [pl.BlockSpec(w.shape, const), pl.BlockSpec((1, d), const)],
        out_specs=pl.BlockSpec((tm, d), row),
        out_shape=jax.ShapeDtypeStruct((rows, d), F32),
        compiler_params=_params("parallel"),
        name="out_proj",
    )(h, yf, yc, yl, yd, w, g)


def _pack_w_in(w_in):
    offs = np.cumsum((0,) + SPLIT_SIZES)
    fq, fk, fv, ff, cu, lx, lg, dq, dkv, iq, ik, iw = (w_in[..., offs[i]:offs[i + 1]] for i in range(12))
    padc = lambda a, n: jnp.pad(a, ((0, 0), (0, 0), (0, n - a.shape[-1])))
    cols = [fq * HEAD_DIM ** -0.5, fk, fv, padc(ff, 128), cu, lx, lg, dq, dkv, iq,
            jnp.tile(ik, (1, 1, IDX_HEADS)), padc(iw, 128)]
    return jnp.concatenate(cols, axis=-1).astype(MXU_DTYPE)


def _block_diag(w):
    depth, n, a, b = w.shape
    eye = jnp.eye(n, dtype=w.dtype)
    return jnp.einsum("lnab,nm->lnamb", w, eye).reshape(depth, n * a, n * b)


def _lane_pad(a, n):
    return jnp.pad(a, [(0, 0)] * (a.ndim - 1) + [(0, n - a.shape[-1])])


def kernel(x, meta_tokens, norm_g, ffn_w_in, ffn_w_out, w_in, w_out, fox_b_f, conv_dw_w, conv_dw_b, conv_ln_g,
           conv_ln_b, lru_conv_w, lru_conv_b, lru_w_a, lru_b_a, lru_w_i, lru_b_i, lru_lambda, dsa_kv_norm_g,
           dsa_w_uk, dsa_w_uv, idx_k_ln_g, idx_k_ln_b):
    bsz, seq, d = x.shape
    depth = norm_g.shape[0]
    assert seq % BLOCK == 0 and d % 128 == 0
    topk = min(TOPK_MAX, seq // 4)
    tp = PAD + N_META + seq
    rows = bsz * tp

    ffn_w_in_m = ffn_w_in.astype(MXU_DTYPE)
    ffn_w_out_m = ffn_w_out.astype(MXU_DTYPE)
    w_in_m = _pack_w_in(w_in)
    w_out_m = w_out.astype(MXU_DTYPE)
    wuk_m = (_block_diag(dsa_w_uk.transpose(0, 1, 3, 2)) * HEAD_DIM ** -0.5).astype(MXU_DTYPE)
    wuvt_m = _block_diag(dsa_w_uv).transpose(0, 2, 1).astype(MXU_DTYPE)
    wa_m = _block_diag(lru_w_a).astype(MXU_DTYPE)
    wi_m = _block_diag(lru_w_i).astype(MXU_DTYPE)
    row2 = lambda a: a[:, None, :]
    dww = jnp.pad(conv_dw_w, ((0, 0), (0, CONV_HALO - CONV_WIDTH), (0, 0)))
    lcw = jnp.pad(lru_conv_w, ((0, 0), (0, LRU_HALO - LRU_CONV_WIDTH), (0, 0)))
    ln_g8 = row2(jnp.tile(idx_k_ln_g, (1, IDX_HEADS)))
    ln_b8 = row2(jnp.tile(idx_k_ln_b, (1, IDX_HEADS)))

    meta = jnp.broadcast_to(meta_tokens[None].astype(x.dtype), (bsz, N_META, d))
    h = jnp.concatenate([jnp.zeros((bsz, PAD, d), x.dtype), meta, x], axis=1).reshape(rows, d)

    for l in range(depth):
        g = norm_g[l][:, None, :]
        h = _ffn(h, g[0], g[1], ffn_w_in_m[l, 0], ffn_w_out_m[l, 0])
        z = _inproj(h, g[2], w_in_m[l], wuk_m[l], row2(dsa_kv_norm_g)[l], ln_g8[l], ln_b8[l])
        seq_params = {
            "fox_b_f": _lane_pad(fox_b_f[l][None], 128), "conv_dw_w": dww[l], "conv_dw_b": row2(conv_dw_b)[l],
            "conv_ln_g": row2(conv_ln_g)[l], "conv_ln_b": row2(conv_ln_b)[l], "lru_conv_w": lcw[l],
            "lru_conv_b": row2(lru_conv_b)[l], "lru_w_a": wa_m[l], "lru_b_a": row2(lru_b_a)[l],
            "lru_w_i": wi_m[l], "lru_b_i": row2(lru_b_i)[l], "lru_lambda": row2(lru_lambda)[l]}
        y_conv, y_lru, cumt, kx = _seqmix(z, bsz, tp, seq_params)
        r3 = lambda a: a.reshape(bsz, tp, a.shape[-1])
        tpk = -(-tp // KEY_STEP) * KEY_STEP
        keys = lambda a: jnp.pad(r3(a), ((0, 0), (0, tpk - tp), (0, 0)))
        keys_t = lambda a: jnp.pad(a.reshape(a.shape[0], bsz, tp).transpose(1, 0, 2), ((0, 0), (0, 0), (0, tpk - tp)))
        y_fox = _fox(z["fqT"], keys(z["fk"]), keys(kx), keys_t(z["fvT"]), cumt)
        y_dsa = _dsa(z["iqT"], z["whT"], z["qlT"], keys(z["ki"]), keys(z["c"]), keys_t(z["cT"]), wuvt_m[l], topk)
        r2 = lambda a: a.reshape(rows, a.shape[-1])
        h = _outproj(h, r2(y_fox), r2(y_conv), r2(y_lru), r2(y_dsa), w_out_m[l], g[3], tp)
        h = _ffn(h, g[4], g[5], ffn_w_in_m[l, 1], ffn_w_out_m[l, 1])

    return h.reshape(bsz, tp, d)[:, PAD + N_META:]
```

```python
import functools

import jax
import jax.numpy as jnp
import numpy as np
from jax import lax
from jax.experimental import pallas as pl
from jax.experimental.pallas import tpu as pltpu

N_META = 16
BLOCK = 128
PAD = BLOCK - N_META
KEY_STEP = 512
GROUP_W = 256
HEAD_DIM = 64
FOX_HEADS = 4
CONV_WIDTH = 31
CONV_HALO = 32
LRU_BLOCKS = 4
LRU_CONV_WIDTH = 4
LRU_HALO = 8
LRU_C = 8.0
DSA_HEADS = 4
DSA_LATENT = 128
IDX_HEADS = 8
IDX_DIM = 32
TOPK_MAX = 256
RMS_EPS = 1e-6
LN_EPS = 1e-5
SPLIT_SIZES = (GROUP_W, GROUP_W, GROUP_W, FOX_HEADS, 2 * GROUP_W, GROUP_W, GROUP_W,
               DSA_HEADS * HEAD_DIM, DSA_LATENT, IDX_HEADS * IDX_DIM, IDX_DIM, IDX_HEADS)

MXU_DTYPE = jnp.bfloat16
F32 = jnp.float32
NEG = -1e30
LOG2E = 1.4426950408889634
SCORE_MASKED = -3e38
SEARCH_WARMUP = 12
VMEM_LIMIT = 56 * 1024 * 1024

_INPROJ_GROUPS = (("fq", 256), ("fk", 256), ("fv", 256), ("ff", 128), ("cu", 512), ("lx", 256),
                  ("lg", 256), ("dq", 256), ("dkv", 128), ("iq", 256), ("ik", 256), ("iw", 128))
_INPROJ_OFF = {}
_o = 0
for _n, _w in _INPROJ_GROUPS:
    _INPROJ_OFF[_n] = (_o, _w)
    _o += _w
INPROJ_COLS = _o


def _params(*sem):
    return pltpu.CompilerParams(dimension_semantics=sem, vmem_limit_bytes=VMEM_LIMIT)


def _rms(x, g):
    return x * lax.rsqrt(jnp.mean(x * x, axis=-1, keepdims=True) + RMS_EPS) * g


def _dot(a, b):
    return jnp.dot(a, b, preferred_element_type=F32)


def _log_sigmoid(x):
    return jnp.minimum(x, 0.0) - jnp.log1p(jnp.exp(-jnp.abs(x)))


def _expm1(y):
    e = jnp.exp(y)
    regular = (e != 1.0) & (e > 0.0)
    r = (e - 1.0) * y / jnp.log(jnp.where(regular, e, 2.0))
    return jnp.where(regular, r, jnp.where(e > 0.0, y, -1.0))


def _row_tile(rows):
    return 640 if rows % 640 == 0 else BLOCK


def _ffn_kernel(h_ref, gpre_ref, gpost_ref, win_ref, wout_ref, o_ref, *, d_ff, chunk):
    x = h_ref[...]
    xn = _rms(x, gpre_ref[...]).astype(MXU_DTYPE)
    acc = jnp.zeros(x.shape, F32)
    for c in range(d_ff // chunk):
        gate = _dot(xn, win_ref[:, c * chunk:(c + 1) * chunk])
        up = _dot(xn, win_ref[:, d_ff + c * chunk:d_ff + (c + 1) * chunk])
        a = (gate * jax.nn.sigmoid(gate) * up).astype(MXU_DTYPE)
        acc = acc + _dot(a, wout_ref[c * chunk:(c + 1) * chunk, :])
    o_ref[...] = x + 0.5 * _rms(acc, gpost_ref[...])


def _ffn(h, g_pre, g_post, w_in, w_out):
    rows, d = h.shape
    d_ff = w_out.shape[0]
    tm = _row_tile(rows)
    chunk = 512 if d_ff % 512 == 0 else d_ff
    const = lambda i: (0, 0)
    return pl.pallas_call(
        functools.partial(_ffn_kernel, d_ff=d_ff, chunk=chunk),
        grid=(rows // tm,),
        in_specs=[pl.BlockSpec((tm, d), lambda i: (i, 0)),
                  pl.BlockSpec((1, d), const), pl.BlockSpec((1, d), const),
                  pl.BlockSpec(w_in.shape, const, pipeline_mode=pl.Buffered(1)),
                  pl.BlockSpec(w_out.shape, const, pipeline_mode=pl.Buffered(1))],
        out_specs=pl.BlockSpec((tm, d), lambda i: (i, 0)),
        out_shape=jax.ShapeDtypeStruct((rows, d), F32),
        compiler_params=_params("parallel"),
        name="ffn",
    )(h, g_pre, g_post, w_in, w_out)


def _inproj_kernel(h_ref, g_ref, w_ref, wuk_ref, kvg_ref, lng_ref, lnb_ref,
                   fqt_ref, fk_ref, fvt_ref, ff_ref, cu_ref, lx_ref, lg_ref,
                   qlt_ref, c_ref, ct_ref, iqt_ref, ki_ref, wht_ref):
    xn = _rms(h_ref[...], g_ref[...]).astype(MXU_DTYPE)

    def proj(name):
        lo, n = _INPROJ_OFF[name]
        return _dot(xn, w_ref[:, lo:lo + n])

    fqt_ref[...] = proj("fq").T.astype(MXU_DTYPE)
    fk_ref[...] = proj("fk").astype(MXU_DTYPE)
    fvt_ref[...] = proj("fv").T.astype(MXU_DTYPE)
    ff_ref[...] = proj("ff")
    cu_ref[...] = proj("cu")
    lx_ref[...] = proj("lx")
    lg_ref[...] = proj("lg")
    qlt_ref[...] = _dot(proj("dq").astype(MXU_DTYPE), wuk_ref[...]).T.astype(MXU_DTYPE)
    c = _rms(proj("dkv"), kvg_ref[...])
    c_ref[...] = c.astype(MXU_DTYPE)
    ct_ref[...] = c.T.astype(MXU_DTYPE)
    iqt_ref[...] = proj("iq").T.astype(MXU_DTYPE)
    ik = proj("ik")
    mu = jnp.mean(ik, axis=-1, keepdims=True)
    var = jnp.mean(jnp.square(ik - mu), axis=-1, keepdims=True)
    ki = (ik - mu) * lax.rsqrt(var + LN_EPS) * lng_ref[...] + lnb_ref[...]
    lane = lax.broadcasted_iota(jnp.int32, (1, BLOCK), 1)
    ki_ref[...] = jnp.where(lane < IDX_DIM, ki[:, :BLOCK], 0.0).astype(MXU_DTYPE)
    wht_ref[...] = (proj("iw") * (IDX_HEADS ** -0.5 * IDX_DIM ** -0.5)).T[:IDX_HEADS, :]


def _inproj(h, g, w, wuk, kvg, lng, lnb):
    rows, d = h.shape
    tm = _row_tile(rows)
    const = lambda i: (0, 0)
    row = lambda i: (i, 0)
    col = lambda i: (0, i)
    outs = (("fqT", 256, MXU_DTYPE, True), ("fk", 256, MXU_DTYPE, False), ("fvT", 256, MXU_DTYPE, True),
            ("ff", 128, F32, False), ("cu", 512, F32, False), ("lx", 256, F32, False), ("lg", 256, F32, False),
            ("qlT", 512, MXU_DTYPE, True), ("c", 128, MXU_DTYPE, False), ("cT", 128, MXU_DTYPE, True),
            ("iqT", 256, MXU_DTYPE, True), ("ki", BLOCK, MXU_DTYPE, False), ("whT", IDX_HEADS, F32, True))
    res = pl.pallas_call(
        _inproj_kernel,
        grid=(rows // tm,),
        in_specs=[pl.BlockSpec((tm, d), row), pl.BlockSpec((1, d), const),
                  pl.BlockSpec(w.shape, const, pipeline_mode=pl.Buffered(1)),
                  pl.BlockSpec(wuk.shape, const),
                  pl.BlockSpec(kvg.shape, const), pl.BlockSpec(lng.shape, const), pl.BlockSpec(lnb.shape, const)],
        out_specs=[pl.BlockSpec((n, tm), col) if t else pl.BlockSpec((tm, n), row) for _, n, _, t in outs],
        out_shape=[jax.ShapeDtypeStruct((n, rows) if t else (rows, n), dt) for _, n, dt, t in outs],
        compiler_params=_params("parallel"),
        name="in_proj",
    )(h, g, w, wuk, kvg, lng, lnb)
    return dict(zip([n for n, _, _, _ in outs], res))


def _shift_rows(x, s, fill, rows):
    return jnp.where(rows >= s, pltpu.roll(x, s, axis=0), fill)


def _seqmix_kernel(ff_ref, cu_ref, lx_ref, lg_ref, bf_ref, dww_ref, dwb_ref, lng_ref, lnb_ref,
                   lcw_ref, lcb_ref, wa_ref, ba_ref, wi_ref, bi_ref, lam_ref,
                   yc_ref, yl_ref, cumt_ref, kx_ref,
                   glu_buf, lx_buf, h_carry, cum_carry):
    t = pl.program_id(1)

    @pl.when(t == 0)
    def _():
        glu_buf[...] = jnp.zeros(glu_buf.shape, F32)
        lx_buf[...] = jnp.zeros(lx_buf.shape, F32)
        h_carry[...] = jnp.zeros(h_carry.shape, F32)
        cum_carry[...] = jnp.zeros(cum_carry.shape, F32)

    rows = lax.broadcasted_iota(jnp.int32, (BLOCK, 1), 0)
    valid = (t * BLOCK + rows) >= PAD

    cu = cu_ref[0]
    glu = jnp.where(valid, cu[:, :GROUP_W] * jax.nn.sigmoid(cu[:, GROUP_W:]), 0.0)
    glu_buf[CONV_HALO:, :] = glu
    acc = jnp.zeros((BLOCK, GROUP_W), F32) + dwb_ref[...]
    for k in range(CONV_WIDTH):
        lo = CONV_HALO - (CONV_WIDTH - 1) + k
        acc = acc + dww_ref[k:k + 1, :] * glu_buf[lo:lo + BLOCK, :]
    glu_buf[:CONV_HALO, :] = glu_buf[BLOCK:, :]
    mu = jnp.mean(acc, axis=-1, keepdims=True)
    var = jnp.mean(jnp.square(acc - mu), axis=-1, keepdims=True)
    hc = (acc - mu) * lax.rsqrt(var + LN_EPS) * lng_ref[...] + lnb_ref[...]
    yc_ref[0] = (hc * jax.nn.sigmoid(hc)).astype(yc_ref.dtype)

    lx_buf[LRU_HALO:, :] = jnp.where(valid, lx_ref[0], 0.0)
    xc = jnp.zeros((BLOCK, GROUP_W), F32) + lcb_ref[...]
    for k in range(LRU_CONV_WIDTH):
        lo = LRU_HALO - (LRU_CONV_WIDTH - 1) + k
        xc = xc + lcw_ref[k:k + 1, :] * lx_buf[lo:lo + BLOCK, :]
    lx_buf[:LRU_HALO, :] = lx_buf[BLOCK:, :]
    xcm = xc.astype(MXU_DTYPE)
    r = jax.nn.sigmoid(_dot(xcm, wa_ref[...]) + ba_ref[...])
    gi = jax.nn.sigmoid(_dot(xcm, wi_ref[...]) + bi_ref[...])
    log_a = LRU_C * r * _log_sigmoid(lam_ref[...])
    a = jnp.exp(log_a)
    u = jnp.where(valid, jnp.sqrt(-_expm1(2.0 * log_a)) * (gi * xc), 0.0)
    s = 1
    while s < BLOCK:
        u = a * _shift_rows(u, s, 0.0, rows) + u
        a = a * _shift_rows(a, s, 1.0, rows)
        s *= 2
    hl = u + a * h_carry[0:1, :]
    h_carry[...] = jnp.broadcast_to(hl[BLOCK - 1:BLOCK, :], h_carry.shape)
    g = lg_ref[0]
    gelu = 0.5 * g * (1.0 + jnp.tanh(np.sqrt(2.0 / np.pi).astype(np.float32) * (g + 0.044715 * g * g * g)))
    yl_ref[0] = (hl * gelu).astype(yl_ref.dtype)

    cs = _log_sigmoid(ff_ref[0] + bf_ref[...])
    s = 1
    while s < BLOCK:
        cs = cs + _shift_rows(cs, s, 0.0, rows)
        s *= 2
    cs = cs + cum_carry[0:1, :]
    cum_carry[...] = jnp.broadcast_to(cs[BLOCK - 1:BLOCK, :], cum_carry.shape)
    cs = cs * LOG2E
    cumt_ref[0] = cs.T[:8, :]
    lanes = lax.broadcasted_iota(jnp.int32, (1, BLOCK), 1)
    ck = jnp.where(lanes < FOX_HEADS, jnp.where(valid, cs, -NEG), 0.0)
    hi = ck.astype(jnp.bfloat16).astype(F32)
    mid = (ck - hi).astype(jnp.bfloat16).astype(F32)
    low = (ck - hi - mid).astype(jnp.bfloat16).astype(F32)
    ones = jnp.where((lanes >= 3 * FOX_HEADS) & (lanes < 3 * FOX_HEADS + 3), 1.0, 0.0)
    kx_ref[0] = (hi + pltpu.roll(mid, FOX_HEADS, axis=1) + pltpu.roll(low, 2 * FOX_HEADS, axis=1)
                 + ones).astype(kx_ref.dtype)


def _seqmix(z, bsz, tp, p):
    nblk = tp // BLOCK
    blk = lambda n: pl.BlockSpec((1, BLOCK, n), lambda b, t: (b, t, 0))
    const = lambda a: pl.BlockSpec(a.shape, lambda b, t: (0, 0))
    r3 = lambda a: a.reshape(bsz, tp, a.shape[-1])
    params = (p["fox_b_f"], p["conv_dw_w"], p["conv_dw_b"], p["conv_ln_g"], p["conv_ln_b"],
              p["lru_conv_w"], p["lru_conv_b"], p["lru_w_a"], p["lru_b_a"], p["lru_w_i"], p["lru_b_i"],
              p["lru_lambda"])
    return pl.pallas_call(
        _seqmix_kernel,
        grid=(bsz, nblk),
        in_specs=[blk(128), blk(512), blk(256), blk(256)] + [const(a) for a in params],
        out_specs=[blk(256), blk(256), pl.BlockSpec((1, 8, BLOCK), lambda b, t: (b, 0, t)), blk(128)],
        out_shape=[jax.ShapeDtypeStruct((bsz, tp, GROUP_W), MXU_DTYPE),
                   jax.ShapeDtypeStruct((bsz, tp, GROUP_W), MXU_DTYPE),
                   jax.ShapeDtypeStruct((bsz, 8, tp), F32),
                   jax.ShapeDtypeStruct((bsz, tp, 128), jnp.bfloat16)],
        scratch_shapes=[pltpu.VMEM((CONV_HALO + BLOCK, GROUP_W), F32),
                        pltpu.VMEM((LRU_HALO + BLOCK, GROUP_W), F32),
                        pltpu.VMEM((8, GROUP_W), F32),
                        pltpu.VMEM((8, 128), F32)],
        compiler_params=_params("parallel", "arbitrary"),
        name="seq_mix",
    )(r3(z["ff"]), r3(z["cu"]), r3(z["lx"]), r3(z["lg"]), *params)


def _fox_kernel(qt_ref, cqt_ref, k_ref, kx_ref, vt_ref, o_ref, sa_ref, sb_ref):
    qi = pl.program_id(1)
    pairs = FOX_HEADS // 2
    two = 2 * BLOCK
    qt = qt_ref[...]
    cqt = cqt_ref[0]
    row = lax.broadcasted_iota(jnp.int32, (BLOCK, 1), 0)
    k_off = lax.broadcasted_iota(jnp.int32, (KEY_STEP, 1), 0)
    q_pos = qi * BLOCK + lax.broadcasted_iota(jnp.int32, (1, BLOCK), 1)
    q_pos2 = jnp.concatenate([q_pos, q_pos], axis=1)
    q_rhs = []
    for p in range(pairs):
        qp = qt[p * BLOCK:(p + 1) * BLOCK]
        halves, extras = [], []
        for c in range(2):
            h = 2 * p + c
            halves.append(jnp.where((row >= c * HEAD_DIM) & (row < (c + 1) * HEAD_DIM), qp, jnp.zeros_like(qp)))
            cq = cqt[h:h + 1]
            cq_hi = cq.astype(jnp.bfloat16).astype(F32)
            cq_mid = (cq - cq_hi).astype(jnp.bfloat16).astype(F32)
            cq_low = (cq - cq_hi - cq_mid).astype(jnp.bfloat16).astype(F32)
            is_piece = (row == h) | (row == FOX_HEADS + h) | (row == 2 * FOX_HEADS + h)
            extra = jnp.where(is_piece, -1.0, 0.0)
            for i, piece in enumerate((cq_hi, cq_mid, cq_low)):
                extra = jnp.where(row == 3 * FOX_HEADS + i, piece, extra)
            extras.append(extra)
        q_rhs.append(jnp.concatenate([jnp.concatenate(halves, axis=1),
                                      jnp.concatenate(extras, axis=1).astype(MXU_DTYPE)], axis=0))

    def qk(j, buf):
        ks = pl.multiple_of(j * KEY_STEP, KEY_STEP)
        kxb = kx_ref[0, pl.ds(ks, KEY_STEP), :]
        for p in range(pairs):
            keys = jnp.concatenate([k_ref[0, pl.ds(ks, KEY_STEP), p * BLOCK:(p + 1) * BLOCK], kxb], axis=1)
            buf[p] = _dot(keys, q_rhs[p])

    def update(j, buf, carry, causal_mask):
        ks = pl.multiple_of(j * KEY_STEP, KEY_STEP)
        out = []
        for p in range(pairs):
            m, l, acc = carry[p]
            s = buf[p]
            if causal_mask:
                s = jnp.where(ks + k_off <= q_pos2, s, NEG)
            m_new = jnp.maximum(m, jnp.max(s, axis=0, keepdims=True))
            alpha = jnp.exp2(m - m_new)
            pr = jnp.exp2(s - m_new)
            l = alpha * l + jnp.sum(pr, axis=0, keepdims=True)
            acc = alpha * acc + _dot(vt_ref[0, p * BLOCK:(p + 1) * BLOCK, pl.ds(ks, KEY_STEP)], pr.astype(MXU_DTYPE))
            out.append((m_new, l, acc))
        return tuple(out)

    init = tuple((jnp.full((1, two), NEG, F32), jnp.zeros((1, two), F32), jnp.zeros((BLOCK, two), F32))
                 for _ in range(pairs))
    n_full = (qi * BLOCK) // KEY_STEP

    qk(0, sa_ref)

    def pair_of_steps(t, carry):
        j = 2 * t
        qk(j + 1, sb_ref)
        carry = update(j, sa_ref, carry, False)
        qk(j + 2, sa_ref)
        return update(j + 1, sb_ref, carry, False)

    carry = lax.fori_loop(0, n_full // 2, pair_of_steps, init)

    def tail_two(carry):
        qk(n_full, sb_ref)
        return update(n_full, sb_ref, update(n_full - 1, sa_ref, carry, True), True)

    carry = lax.cond(n_full % 2 == 1, tail_two, lambda carry: update(n_full, sa_ref, carry, True), carry)
    outs = []
    for p in range(pairs):
        _, l, acc = carry[p]
        o_t = acc / l
        outs.append(jnp.where(row < HEAD_DIM, o_t[:, :BLOCK], o_t[:, BLOCK:]).T)
    o_ref[0] = jnp.concatenate(outs, axis=1).astype(o_ref.dtype)


def _fox(fqt, fk, kx, fvt, cumt):
    bsz, _, tp = cumt.shape
    nblk = tp // BLOCK
    full = lambda a: pl.BlockSpec((1,) + a.shape[1:], lambda b, i: (b, 0, 0))
    return pl.pallas_call(
        _fox_kernel,
        grid=(bsz, nblk),
        in_specs=[pl.BlockSpec((GROUP_W, BLOCK), lambda b, i: (0, b * nblk + i)),
                  pl.BlockSpec((1, 8, BLOCK), lambda b, i: (b, 0, i)), full(fk), full(kx), full(fvt)],
        out_specs=pl.BlockSpec((1, BLOCK, GROUP_W), lambda b, i: (b, i, 0)),
        out_shape=jax.ShapeDtypeStruct((bsz, tp, GROUP_W), MXU_DTYPE),
        scratch_shapes=[pltpu.VMEM((FOX_HEADS // 2, KEY_STEP, 2 * BLOCK), F32)] * 2,
        compiler_params=_params("parallel", "arbitrary"),
        name="fox",
    )(fqt, cumt, fk, kx, fvt)


def _dsa_kernel(iqt_ref, wht_ref, qlt_ref, ki_ref, c_ref, ct_ref, wuvt_ref, o_ref,
                sc_ref, sa_ref, sb_ref, da_ref, db_ref, *, topk):
    qi = pl.program_id(1)
    nkb = (qi * BLOCK) // KEY_STEP + 1
    k_off = lax.broadcasted_iota(jnp.int32, (KEY_STEP, 1), 0)
    q_pos = qi * BLOCK + lax.broadcasted_iota(jnp.int32, (1, BLOCK), 1)
    n_valid = q_pos - PAD + 1
    fold_rows = KEY_STEP // 8

    def fold(op, w):
        parts = [w[i * fold_rows:(i + 1) * fold_rows] for i in range(8)]
        return op(op(op(parts[0], parts[1]), op(parts[2], parts[3])), op(op(parts[4], parts[5]), op(parts[6], parts[7])))

    iqt = iqt_ref[...]
    wht = wht_ref[...]
    zeros = jnp.zeros((BLOCK - IDX_DIM, BLOCK), iqt.dtype)
    q_heads = jnp.concatenate(
        [jnp.concatenate([iqt[h * IDX_DIM:(h + 1) * IDX_DIM], zeros], axis=0) for h in range(IDX_HEADS)],
        axis=1)

    def head_dots(j, buf):
        ks = pl.multiple_of(j * KEY_STEP, KEY_STEP)
        buf[...] = _dot(ki_ref[0, pl.ds(ks, KEY_STEP), :], q_heads)

    def score_step(j, buf, stats):
        amax, s0, s1, s2 = stats
        sc = jnp.zeros((KEY_STEP, BLOCK), F32)
        for h in range(IDX_HEADS):
            sc = sc + jnp.maximum(buf[:, h * BLOCK:(h + 1) * BLOCK], 0.0) * wht[h:h + 1, :]
        k_pos = j * KEY_STEP + k_off
        valid = (k_pos <= q_pos) & (k_pos >= PAD)
        sc_ref[j] = jnp.where(valid, sc, SCORE_MASKED)
        sample = jnp.where(valid[:fold_rows], sc[:fold_rows], 0.0)
        return (jnp.maximum(amax, fold(jnp.maximum, jnp.abs(sc))),
                s0 + jnp.where(valid[:fold_rows], 1.0, 0.0), s1 + sample, s2 + sample * sample)

    last = nkb - 1
    head_dots(0, da_ref)

    def score_pair(t, stats):
        j = 2 * t
        head_dots(j + 1, db_ref)
        stats = score_step(j, da_ref, stats)
        head_dots(jnp.minimum(j + 2, last), da_ref)
        return score_step(j + 1, db_ref, stats)

    stats = lax.fori_loop(0, nkb // 2, score_pair, (jnp.zeros((fold_rows, BLOCK), F32),) * 4)
    amax, s0, s1, s2 = lax.cond(nkb % 2 == 1, lambda st: score_step(last, da_ref, st), lambda st: st, stats)
    bound = jnp.max(amax, axis=0, keepdims=True) * 1.0001 + 1e-30

    @pl.when(nkb % 2 == 1)
    def _():
        sc_ref[nkb] = jnp.full((KEY_STEP, BLOCK), SCORE_MASKED, F32)

    def count(*preds):
        def body(jj, accs):
            tiles = (sc_ref[2 * jj], sc_ref[2 * jj + 1])
            return tuple(tuple(acc + fold(jnp.add, jnp.where(pred(t), 1, 0)) for acc, t in zip(pair, tiles))
                         for pair, pred in zip(accs, preds))
        zeros = jnp.zeros((fold_rows, BLOCK), jnp.int32)
        accs = lax.fori_loop(0, (nkb + 1) // 2, body, tuple((zeros, zeros) for _ in preds))
        return [jnp.sum(a + b, axis=0, keepdims=True).astype(F32) for a, b in accs]

    kf = float(topk)
    take_all = n_valid <= topk
    c_ge0, c_gt0 = count(lambda s: s >= 0.0, lambda s: s > 0.0)
    positive = c_gt0 >= kf
    zero_tie = (c_ge0 >= kf) & jnp.logical_not(positive)
    lo = jnp.where(positive | zero_tie, 0.0, -bound)
    c_lo = jnp.where(positive | zero_tie, c_ge0, n_valid.astype(F32))
    hi = jnp.where(positive, bound, 0.0)
    c_hi = jnp.where(positive, 0.0, jnp.where(zero_tie, c_gt0, c_ge0))
    done = jnp.where(take_all | zero_tie | (c_lo == kf), 1, 0)

    n_s = jnp.maximum(jnp.sum(s0, axis=0, keepdims=True), 1.0)
    mean = jnp.sum(s1, axis=0, keepdims=True) / n_s
    std = jnp.sqrt(jnp.maximum(jnp.sum(s2, axis=0, keepdims=True) / n_s - mean * mean, 0.0))
    tail = jnp.clip(kf / jnp.maximum(n_valid, 1).astype(F32), 1e-6, 1.0 - 1e-6)
    upper = tail < 0.5
    t_q = jnp.sqrt(-2.0 * jnp.log(jnp.where(upper, tail, 1.0 - tail)))
    z_q = t_q - ((0.010328 * t_q + 0.802853) * t_q + 2.515517) / (((0.001308 * t_q + 0.189269) * t_q + 1.432788) * t_q + 1.0)
    guess = mean + jnp.where(upper, z_q, -z_q) * std

    def next_probe(lo, hi, c_lo, c_hi, it):
        mid = 0.5 * lo + 0.5 * hi
        inside = (mid > lo) & (mid < hi)
        log_lo = jnp.log(jnp.maximum(c_lo, 1.0))
        frac = (log_lo - np.log(kf)) / (log_lo - jnp.log(jnp.maximum(c_hi, 0.5)))
        probe = lo + (hi - lo) * jnp.clip(frac, 0.02, 0.98)
        turn = jnp.zeros_like(done) + it
        probe = jnp.where(turn == 0, guess, probe)
        use_probe = (turn % 4 != 3) & (probe > lo) & (probe < hi)
        return jnp.where(use_probe, probe, mid), jnp.where(inside, 1, 0)

    def search_step(state):
        lo, hi, c_lo, c_hi, done, probe, inside, it = state
        c, = count(lambda s: s >= probe)
        active = (done == 0) & (inside > 0)
        up = active & (c >= kf)
        down = active & (c < kf)
        lo, c_lo = jnp.where(up, probe, lo), jnp.where(up, c, c_lo)
        hi, c_hi = jnp.where(down, probe, hi), jnp.where(down, c, c_hi)
        done = jnp.where((done > 0) | (inside == 0) | (c_lo == kf), 1, 0)
        probe, inside = next_probe(lo, hi, c_lo, c_hi, it + 1)
        return lo, hi, c_lo, c_hi, done, probe, inside, it + 1

    def unfinished(state):
        return jnp.sum(1 - state[4])

    probe0, inside0 = next_probe(lo, hi, c_lo, c_hi, jnp.int32(0))
    state = (lo, hi, c_lo, c_hi, done, probe0, inside0, jnp.int32(0))
    state = lax.fori_loop(0, SEARCH_WARMUP, lambda _, st: search_step(st), state)

    def search_body(carry):
        state = search_step(carry[0])
        return state, unfinished(state)

    state, _ = lax.while_loop(lambda carry: carry[1] > 0, search_body, (state, unfinished(state)))
    thr, _, c_thr, c_above = state[:4]
    thr = jnp.where(take_all, 0.5 * SCORE_MASKED, thr)
    tied = jnp.logical_not(take_all) & (c_thr > kf)
    any_tied = jnp.max(jnp.where(tied, 1, 0)) > 0

    qlt = qlt_ref[...]
    q_lat = jnp.concatenate([qlt[h * DSA_LATENT:(h + 1) * DSA_LATENT] for h in range(DSA_HEADS)], axis=1)
    wide = DSA_HEADS * BLOCK

    def attend(select):
        def qk(j, buf):
            ks = pl.multiple_of(j * KEY_STEP, KEY_STEP)
            buf[...] = _dot(c_ref[0, pl.ds(ks, KEY_STEP), :], q_lat)

        def consume(j, buf, carry):
            m, l, acc, run = carry
            ks = pl.multiple_of(j * KEY_STEP, KEY_STEP)
            sel, run = select(sc_ref[j], run)
            s_all = buf[...]
            s_all = jnp.concatenate([jnp.where(sel, s_all[:, h * BLOCK:(h + 1) * BLOCK], NEG)
                                     for h in range(DSA_HEADS)], axis=1)
            m_new = jnp.maximum(m, jnp.max(s_all, axis=0, keepdims=True))
            alpha = jnp.exp2(m - m_new)
            pr = jnp.exp2(s_all - m_new)
            l = alpha * l + jnp.sum(pr, axis=0, keepdims=True)
            acc = alpha * acc + _dot(ct_ref[0, :, pl.ds(ks, KEY_STEP)], pr.astype(MXU_DTYPE))
            return m_new, l, acc, run

        qk(0, sa_ref)

        def pair(t, carry):
            j = 2 * t
            qk(j + 1, sb_ref)
            carry = consume(j, sa_ref, carry)
            qk(jnp.minimum(j + 2, last), sa_ref)
            return consume(j + 1, sb_ref, carry)

        init = (jnp.full((1, wide), NEG, F32), jnp.zeros((1, wide), F32), jnp.zeros((DSA_LATENT, wide), F32),
                jnp.zeros((1, BLOCK), F32))
        carry = lax.fori_loop(0, nkb // 2, pair, init)
        _, l, acc, _ = lax.cond(nkb % 2 == 1, lambda c: consume(last, sa_ref, c), lambda c: c, carry)
        o_lat = (acc / jnp.where(l > 0.0, l, 1.0)).astype(MXU_DTYPE)
        y_t = jnp.zeros((GROUP_W, BLOCK), F32)
        for h in range(DSA_HEADS):
            y_t = y_t + _dot(wuvt_ref[:, h * DSA_LATENT:(h + 1) * DSA_LATENT], o_lat[:, h * BLOCK:(h + 1) * BLOCK])
        o_ref[0] = y_t.T.astype(o_ref.dtype)

    @pl.when(jnp.logical_not(any_tied))
    def _():
        attend(lambda s, run: (s >= thr, run))

    @pl.when(any_tied)
    def _():
        quota = jnp.where(tied, kf - c_above, 2.0 ** 30)
        r_i = lax.broadcasted_iota(jnp.int32, (KEY_STEP, KEY_STEP), 0)
        c_i = lax.broadcasted_iota(jnp.int32, (KEY_STEP, KEY_STEP), 1)
        tri = jnp.where(c_i <= r_i, 1.0, 0.0).astype(MXU_DTYPE)

        def select(s, run):
            eqf = jnp.where(s == thr, 1.0, 0.0)
            prefix = _dot(tri, eqf.astype(MXU_DTYPE)) + run
            within = jnp.where(prefix <= quota, eqf, 0.0)
            sel = (jnp.where(s > thr, 1.0, 0.0) + within) > 0.5
            return sel, run + jnp.sum(eqf, axis=0, keepdims=True)

        attend(select)


def _dsa(iqt, wht, qlt, ki, c, ct, wuvt, topk):
    bsz, tpk, _ = ki.shape
    tp = iqt.shape[1] // bsz
    nblk = tp // BLOCK
    qcol = lambda n: pl.BlockSpec((n, BLOCK), lambda b, i: (0, b * nblk + i))
    full = lambda a: pl.BlockSpec((1,) + a.shape[1:], lambda b, i: (b, 0, 0))
    return pl.pallas_call(
        functools.partial(_dsa_kernel, topk=topk),
        grid=(bsz, nblk),
        in_specs=[qcol(iqt.shape[0]), qcol(wht.shape[0]), qcol(qlt.shape[0]), full(ki), full(c), full(ct),
                  pl.BlockSpec(wuvt.shape, lambda b, i: (0, 0))],
        out_specs=pl.BlockSpec((1, BLOCK, GROUP_W), lambda b, i: (b, i, 0)),
        out_shape=jax.ShapeDtypeStruct((bsz, tp, GROUP_W), MXU_DTYPE),
        scratch_shapes=[pltpu.VMEM((tpk // KEY_STEP + 1, KEY_STEP, BLOCK), F32),
                        pltpu.VMEM((KEY_STEP, DSA_HEADS * BLOCK), F32), pltpu.VMEM((KEY_STEP, DSA_HEADS * BLOCK), F32),
                        pltpu.VMEM((KEY_STEP, IDX_HEADS * BLOCK), F32), pltpu.VMEM((KEY_STEP, IDX_HEADS * BLOCK), F32)],
        compiler_params=_params("parallel", "arbitrary"),
        name="dsa",
    )(iqt, wht, qlt, ki, c, ct, wuvt)


def _outproj_kernel(h_ref, yf_ref, yc_ref, yl_ref, yd_ref, w_ref, g_ref, o_ref, *, tm, tp):
    mix = _dot(yf_ref[...], w_ref[0:GROUP_W, :])
    mix = mix + _dot(yc_ref[...], w_ref[GROUP_W:2 * GROUP_W, :])
    mix = mix + _dot(yl_ref[...], w_ref[2 * GROUP_W:3 * GROUP_W, :])
    mix = mix + _dot(yd_ref[...], w_ref[3 * GROUP_W:4 * GROUP_W, :])
    row = (pl.program_id(0) * tm) % tp + lax.broadcasted_iota(jnp.int32, (tm, 1), 0)
    o_ref[...] = jnp.where(row >= PAD, h_ref[...] + _rms(mix, g_ref[...]), 0.0)


def _outproj(h, yf, yc, yl, yd, w, g, tp):
    rows, d = h.shape
    tm = _row_tile(rows)
    row = lambda i: (i, 0)
    const = lambda i: (0, 0)
    return pl.pallas_call(
        functools.partial(_outproj_kernel, tm=tm, tp=tp),
        grid=(rows // tm,),
        in_specs=[pl.BlockSpec((tm, d), row)] + [pl.BlockSpec((tm, GROUP_W), row)] * 4
                 + [pl.BlockSpec(w.shape, const), pl.BlockSpec((1, d), const)],
        out_specs=pl.BlockSpec((tm, d), row),
        out_shape=jax.ShapeDtypeStruct((rows, d), F32),
        compiler_params=_params("parallel"),
        name="out_proj",
    )(h, yf, yc, yl, yd, w, g)


def _pack_w_in(w_in):
    offs = np.cumsum((0,) + SPLIT_SIZES)
    fq, fk, fv, ff, cu, lx, lg, dq, dkv, iq, ik, iw = (w_in[..., offs[i]:offs[i + 1]] for i in range(12))
    padc = lambda a, n: jnp.pad(a, ((0, 0), (0, 0), (0, n - a.shape[-1])))
    cols = [fq * (HEAD_DIM ** -0.5 * LOG2E), fk, fv, padc(ff, 128), cu, lx, lg, dq, dkv, iq,
            jnp.tile(ik, (1, 1, IDX_HEADS)), padc(iw, 128)]
    return jnp.concatenate(cols, axis=-1).astype(MXU_DTYPE)


def _block_diag(w):
    depth, n, a, b = w.shape
    eye = jnp.eye(n, dtype=w.dtype)
    return jnp.einsum("lnab,nm->lnamb", w, eye).reshape(depth, n * a, n * b)


def _lane_pad(a, n):
    return jnp.pad(a, [(0, 0)] * (a.ndim - 1) + [(0, n - a.shape[-1])])


def kernel(x, meta_tokens, norm_g, ffn_w_in, ffn_w_out, w_in, w_out, fox_b_f, conv_dw_w, conv_dw_b, conv_ln_g,
           conv_ln_b, lru_conv_w, lru_conv_b, lru_w_a, lru_b_a, lru_w_i, lru_b_i, lru_lambda, dsa_kv_norm_g,
           dsa_w_uk, dsa_w_uv, idx_k_ln_g, idx_k_ln_b):
    bsz, seq, d = x.shape
    depth = norm_g.shape[0]
    assert seq % BLOCK == 0 and d % 128 == 0
    topk = min(TOPK_MAX, seq // 4)
    tp = PAD + N_META + seq
    rows = bsz * tp

    ffn_w_in_m = ffn_w_in.astype(MXU_DTYPE)
    ffn_w_out_m = ffn_w_out.astype(MXU_DTYPE)
    w_in_m = _pack_w_in(w_in)
    w_out_m = w_out.astype(MXU_DTYPE)
    wuk_m = (_block_diag(dsa_w_uk.transpose(0, 1, 3, 2)) * (HEAD_DIM ** -0.5 * LOG2E)).astype(MXU_DTYPE)
    wuvt_m = _block_diag(dsa_w_uv).transpose(0, 2, 1).astype(MXU_DTYPE)
    wa_m = _block_diag(lru_w_a).astype(MXU_DTYPE)
    wi_m = _block_diag(lru_w_i).astype(MXU_DTYPE)
    row2 = lambda a: a[:, None, :]
    dww = jnp.pad(conv_dw_w, ((0, 0), (0, CONV_HALO - CONV_WIDTH), (0, 0)))
    lcw = jnp.pad(lru_conv_w, ((0, 0), (0, LRU_HALO - LRU_CONV_WIDTH), (0, 0)))
    ln_g8 = row2(jnp.tile(idx_k_ln_g, (1, IDX_HEADS)))
    ln_b8 = row2(jnp.tile(idx_k_ln_b, (1, IDX_HEADS)))

    meta = jnp.broadcast_to(meta_tokens[None].astype(x.dtype), (bsz, N_META, d))
    h = jnp.concatenate([jnp.zeros((bsz, PAD, d), x.dtype), meta, x], axis=1).reshape(rows, d)

    for l in range(depth):
        g = norm_g[l][:, None, :]
        h = _ffn(h, g[0], g[1], ffn_w_in_m[l, 0], ffn_w_out_m[l, 0])
        z = _inproj(h, g[2], w_in_m[l], wuk_m[l], row2(dsa_kv_norm_g)[l], ln_g8[l], ln_b8[l])
        seq_params = {
            "fox_b_f": _lane_pad(fox_b_f[l][None], 128), "conv_dw_w": dww[l], "conv_dw_b": row2(conv_dw_b)[l],
            "conv_ln_g": row2(conv_ln_g)[l], "conv_ln_b": row2(conv_ln_b)[l], "lru_conv_w": lcw[l],
            "lru_conv_b": row2(lru_conv_b)[l], "lru_w_a": wa_m[l], "lru_b_a": row2(lru_b_a)[l],
            "lru_w_i": wi_m[l], "lru_b_i": row2(lru_b_i)[l], "lru_lambda": row2(lru_lambda)[l]}
        y_conv, y_lru, cumt, kx = _seqmix(z, bsz, tp, seq_params)
        r3 = lambda a: a.reshape(bsz, tp, a.shape[-1])
        tpk = -(-tp // KEY_STEP) * KEY_STEP
        keys = lambda a: jnp.pad(r3(a), ((0, 0), (0, tpk - tp), (0, 0)))
        keys_t = lambda a: jnp.pad(a.reshape(a.shape[0], bsz, tp).transpose(1, 0, 2), ((0, 0), (0, 0), (0, tpk - tp)))
        y_fox = _fox(z["fqT"], keys(z["fk"]), keys(kx), keys_t(z["fvT"]), cumt)
        y_dsa = _dsa(z["iqT"], z["whT"], z["qlT"], keys(z["ki"]), keys(z["c"]), keys_t(z["cT"]), wuvt_m[l], topk)
        r2 = lambda a: a.reshape(rows, a.shape[-1])
        h = _outproj(h, r2(y_fox), r2(y_conv), r2(y_lru), r2(y_dsa), w_out_m[l], g[3], tp)
        h = _ffn(h, g[4], g[5], ffn_w_in_m[l, 1], ffn_w_out_m[l, 1])

    return h.reshape(bsz, tp, d)[:, PAD + N_META:]
```

```python
import functools

import jax
import jax.numpy as jnp
import numpy as np
from jax import lax
from jax.experimental import pallas as pl
from jax.experimental.pallas import tpu as pltpu

N_META = 16
BLOCK = 128
PAD = BLOCK - N_META
KEY_STEP = 512
GROUP_W = 256
HEAD_DIM = 64
FOX_HEADS = 4
CONV_WIDTH = 31
CONV_HALO = 32
LRU_BLOCKS = 4
LRU_CONV_WIDTH = 4
LRU_HALO = 8
LRU_C = 8.0
DSA_HEADS = 4
DSA_LATENT = 128
IDX_HEADS = 8
IDX_DIM = 32
TOPK_MAX = 256
RMS_EPS = 1e-6
LN_EPS = 1e-5
SPLIT_SIZES = (GROUP_W, GROUP_W, GROUP_W, FOX_HEADS, 2 * GROUP_W, GROUP_W, GROUP_W,
               DSA_HEADS * HEAD_DIM, DSA_LATENT, IDX_HEADS * IDX_DIM, IDX_DIM, IDX_HEADS)

MXU_DTYPE = jnp.bfloat16
F32 = jnp.float32
NEG = -1e30
LOG2E = 1.4426950408889634
SCORE_MASKED = -3e38
SEARCH_WARMUP = 12
VMEM_LIMIT = 56 * 1024 * 1024

_INPROJ_GROUPS = (("fq", 256), ("fk", 256), ("fv", 256), ("ff", 128), ("cu", 512), ("lx", 256),
                  ("lg", 256), ("dq", 256), ("dkv", 128), ("iq", 256), ("ik", 256), ("iw", 128))
_INPROJ_OFF = {}
_o = 0
for _n, _w in _INPROJ_GROUPS:
    _INPROJ_OFF[_n] = (_o, _w)
    _o += _w
INPROJ_COLS = _o


def _params(*sem):
    return pltpu.CompilerParams(dimension_semantics=sem, vmem_limit_bytes=VMEM_LIMIT)


def _rms(x, g):
    return x * lax.rsqrt(jnp.mean(x * x, axis=-1, keepdims=True) + RMS_EPS) * g


def _dot(a, b):
    return jnp.dot(a, b, preferred_element_type=F32)


def _log_sigmoid(x):
    return jnp.minimum(x, 0.0) - jnp.log1p(jnp.exp(-jnp.abs(x)))


def _expm1(y):
    e = jnp.exp(y)
    regular = (e != 1.0) & (e > 0.0)
    r = (e - 1.0) * y / jnp.log(jnp.where(regular, e, 2.0))
    return jnp.where(regular, r, jnp.where(e > 0.0, y, -1.0))


def _row_tile(rows):
    return 640 if rows % 640 == 0 else BLOCK


def _ffn_kernel(h_ref, gpre_ref, gpost_ref, win_ref, wout_ref, o_ref, *, d_ff, chunk):
    x = h_ref[...]
    xn = _rms(x, gpre_ref[...]).astype(MXU_DTYPE)
    acc = jnp.zeros(x.shape, F32)
    for c in range(d_ff // chunk):
        gate = _dot(xn, win_ref[:, c * chunk:(c + 1) * chunk])
        up = _dot(xn, win_ref[:, d_ff + c * chunk:d_ff + (c + 1) * chunk])
        a = (gate * jax.nn.sigmoid(gate) * up).astype(MXU_DTYPE)
        acc = acc + _dot(a, wout_ref[c * chunk:(c + 1) * chunk, :])
    o_ref[...] = x + 0.5 * _rms(acc, gpost_ref[...])


def _ffn(h, g_pre, g_post, w_in, w_out):
    rows, d = h.shape
    d_ff = w_out.shape[0]
    tm = _row_tile(rows)
    chunk = 512 if d_ff % 512 == 0 else d_ff
    const = lambda i: (0, 0)
    return pl.pallas_call(
        functools.partial(_ffn_kernel, d_ff=d_ff, chunk=chunk),
        grid=(rows // tm,),
        in_specs=[pl.BlockSpec((tm, d), lambda i: (i, 0)),
                  pl.BlockSpec((1, d), const), pl.BlockSpec((1, d), const),
                  pl.BlockSpec(w_in.shape, const, pipeline_mode=pl.Buffered(1)),
                  pl.BlockSpec(w_out.shape, const, pipeline_mode=pl.Buffered(1))],
        out_specs=pl.BlockSpec((tm, d), lambda i: (i, 0)),
        out_shape=jax.ShapeDtypeStruct((rows, d), F32),
        compiler_params=_params("parallel"),
        name="ffn",
    )(h, g_pre, g_post, w_in, w_out)


def _inproj_kernel(h_ref, g_ref, w_ref, wuk_ref, kvg_ref, lng_ref, lnb_ref,
                   fqt_ref, fk_ref, fvt_ref, ff_ref, cu_ref, lx_ref, lg_ref,
                   qlt_ref, c_ref, ct_ref, iqt_ref, ki_ref, wht_ref):
    xn = _rms(h_ref[...], g_ref[...]).astype(MXU_DTYPE)

    def proj(name):
        lo, n = _INPROJ_OFF[name]
        return _dot(xn, w_ref[:, lo:lo + n])

    fqt_ref[...] = proj("fq").T.astype(MXU_DTYPE)
    fk_ref[...] = proj("fk").astype(MXU_DTYPE)
    fvt_ref[...] = proj("fv").T.astype(MXU_DTYPE)
    ff_ref[...] = proj("ff")
    cu_ref[...] = proj("cu")
    lx_ref[...] = proj("lx")
    lg_ref[...] = proj("lg")
    qlt_ref[...] = _dot(proj("dq").astype(MXU_DTYPE), wuk_ref[...]).T.astype(MXU_DTYPE)
    c = _rms(proj("dkv"), kvg_ref[...])
    c_ref[...] = c.astype(MXU_DTYPE)
    ct_ref[...] = c.T.astype(MXU_DTYPE)
    iqt_ref[...] = proj("iq").T.astype(MXU_DTYPE)
    ik = proj("ik")
    mu = jnp.mean(ik, axis=-1, keepdims=True)
    var = jnp.mean(jnp.square(ik - mu), axis=-1, keepdims=True)
    ki = (ik - mu) * lax.rsqrt(var + LN_EPS) * lng_ref[...] + lnb_ref[...]
    lane = lax.broadcasted_iota(jnp.int32, (1, BLOCK), 1)
    ki_ref[...] = jnp.where(lane < IDX_DIM, ki[:, :BLOCK], 0.0).astype(MXU_DTYPE)
    wht_ref[...] = (proj("iw") * (IDX_HEADS ** -0.5 * IDX_DIM ** -0.5)).T[:IDX_HEADS, :]


def _inproj(h, g, w, wuk, kvg, lng, lnb):
    rows, d = h.shape
    tm = _row_tile(rows)
    const = lambda i: (0, 0)
    row = lambda i: (i, 0)
    col = lambda i: (0, i)
    outs = (("fqT", 256, MXU_DTYPE, True), ("fk", 256, MXU_DTYPE, False), ("fvT", 256, MXU_DTYPE, True),
            ("ff", 128, F32, False), ("cu", 512, F32, False), ("lx", 256, F32, False), ("lg", 256, F32, False),
            ("qlT", 512, MXU_DTYPE, True), ("c", 128, MXU_DTYPE, False), ("cT", 128, MXU_DTYPE, True),
            ("iqT", 256, MXU_DTYPE, True), ("ki", BLOCK, MXU_DTYPE, False), ("whT", IDX_HEADS, F32, True))
    res = pl.pallas_call(
        _inproj_kernel,
        grid=(rows // tm,),
        in_specs=[pl.BlockSpec((tm, d), row), pl.BlockSpec((1, d), const),
                  pl.BlockSpec(w.shape, const, pipeline_mode=pl.Buffered(1)),
                  pl.BlockSpec(wuk.shape, const),
                  pl.BlockSpec(kvg.shape, const), pl.BlockSpec(lng.shape, const), pl.BlockSpec(lnb.shape, const)],
        out_specs=[pl.BlockSpec((n, tm), col) if t else pl.BlockSpec((tm, n), row) for _, n, _, t in outs],
        out_shape=[jax.ShapeDtypeStruct((n, rows) if t else (rows, n), dt) for _, n, dt, t in outs],
        compiler_params=_params("parallel"),
        name="in_proj",
    )(h, g, w, wuk, kvg, lng, lnb)
    return dict(zip([n for n, _, _, _ in outs], res))


def _shift_rows(x, s, fill, rows):
    return jnp.where(rows >= s, pltpu.roll(x, s, axis=0), fill)


def _seqmix_kernel(ff_ref, cu_ref, lx_ref, lg_ref, bf_ref, dww_ref, dwb_ref, lng_ref, lnb_ref,
                   lcw_ref, lcb_ref, wa_ref, ba_ref, wi_ref, bi_ref, lam_ref,
                   yc_ref, yl_ref, cumt_ref, kx_ref,
                   glu_buf, lx_buf, h_carry, cum_carry, shift_buf):
    t = pl.program_id(1)

    @pl.when(t == 0)
    def _():
        glu_buf[...] = jnp.zeros(glu_buf.shape, F32)
        lx_buf[...] = jnp.zeros(lx_buf.shape, F32)
        h_carry[...] = jnp.zeros(h_carry.shape, F32)
        cum_carry[...] = jnp.zeros(cum_carry.shape, F32)

    rows = lax.broadcasted_iota(jnp.int32, (BLOCK, 1), 0)
    valid = (t * BLOCK + rows) >= PAD

    cu = cu_ref[0]
    glu = jnp.where(valid, cu[:, :GROUP_W] * jax.nn.sigmoid(cu[:, GROUP_W:]), 0.0)
    glu_buf[CONV_HALO:, :] = glu
    acc = jnp.zeros((BLOCK, GROUP_W), F32) + dwb_ref[...]
    span = BLOCK + CONV_HALO - 8
    for r in range(1, 8):
        shift_buf[r, :span, :] = glu_buf[r:r + span, :]
    for k in range(CONV_WIDTH):
        lo = CONV_HALO - (CONV_WIDTH - 1) + k
        r, base = lo % 8, lo - lo % 8
        window = glu_buf[base:base + BLOCK, :] if r == 0 else shift_buf[r, base:base + BLOCK, :]
        acc = acc + dww_ref[k:k + 1, :] * window
    glu_buf[:CONV_HALO, :] = glu_buf[BLOCK:, :]
    mu = jnp.mean(acc, axis=-1, keepdims=True)
    var = jnp.mean(jnp.square(acc - mu), axis=-1, keepdims=True)
    hc = (acc - mu) * lax.rsqrt(var + LN_EPS) * lng_ref[...] + lnb_ref[...]
    yc_ref[0] = (hc * jax.nn.sigmoid(hc)).astype(yc_ref.dtype)

    lx_buf[LRU_HALO:, :] = jnp.where(valid, lx_ref[0], 0.0)
    xc = jnp.zeros((BLOCK, GROUP_W), F32) + lcb_ref[...]
    for k in range(LRU_CONV_WIDTH):
        lo = LRU_HALO - (LRU_CONV_WIDTH - 1) + k
        xc = xc + lcw_ref[k:k + 1, :] * lx_buf[lo:lo + BLOCK, :]
    lx_buf[:LRU_HALO, :] = lx_buf[BLOCK:, :]
    xcm = xc.astype(MXU_DTYPE)
    r = jax.nn.sigmoid(_dot(xcm, wa_ref[...]) + ba_ref[...])
    gi = jax.nn.sigmoid(_dot(xcm, wi_ref[...]) + bi_ref[...])
    log_a = LRU_C * r * _log_sigmoid(lam_ref[...])
    a = jnp.exp(log_a)
    u = jnp.where(valid, jnp.sqrt(-_expm1(2.0 * log_a)) * (gi * xc), 0.0)
    s = 1
    while s < BLOCK:
        u = a * _shift_rows(u, s, 0.0, rows) + u
        a = a * _shift_rows(a, s, 1.0, rows)
        s *= 2
    hl = u + a * h_carry[0:1, :]
    h_carry[...] = jnp.broadcast_to(hl[BLOCK - 1:BLOCK, :], h_carry.shape)
    g = lg_ref[0]
    gelu = 0.5 * g * (1.0 + jnp.tanh(np.sqrt(2.0 / np.pi).astype(np.float32) * (g + 0.044715 * g * g * g)))
    yl_ref[0] = (hl * gelu).astype(yl_ref.dtype)

    cs = _log_sigmoid(ff_ref[0] + bf_ref[...])
    s = 1
    while s < BLOCK:
        cs = cs + _shift_rows(cs, s, 0.0, rows)
        s *= 2
    cs = cs + cum_carry[0:1, :]
    cum_carry[...] = jnp.broadcast_to(cs[BLOCK - 1:BLOCK, :], cum_carry.shape)
    cs = cs * LOG2E
    cumt_ref[0] = cs.T[:8, :]
    lanes = lax.broadcasted_iota(jnp.int32, (1, BLOCK), 1)
    ck = jnp.where(lanes < FOX_HEADS, jnp.where(valid, cs, -NEG), 0.0)
    hi = ck.astype(jnp.bfloat16).astype(F32)
    mid = (ck - hi).astype(jnp.bfloat16).astype(F32)
    low = (ck - hi - mid).astype(jnp.bfloat16).astype(F32)
    ones = jnp.where((lanes >= 3 * FOX_HEADS) & (lanes < 3 * FOX_HEADS + 3), 1.0, 0.0)
    kx_ref[0] = (hi + pltpu.roll(mid, FOX_HEADS, axis=1) + pltpu.roll(low, 2 * FOX_HEADS, axis=1)
                 + ones).astype(kx_ref.dtype)


def _seqmix(z, bsz, tp, p):
    nblk = tp // BLOCK
    blk = lambda n: pl.BlockSpec((1, BLOCK, n), lambda b, t: (b, t, 0))
    const = lambda a: pl.BlockSpec(a.shape, lambda b, t: (0, 0))
    r3 = lambda a: a.reshape(bsz, tp, a.shape[-1])
    params = (p["fox_b_f"], p["conv_dw_w"], p["conv_dw_b"], p["conv_ln_g"], p["conv_ln_b"],
              p["lru_conv_w"], p["lru_conv_b"], p["lru_w_a"], p["lru_b_a"], p["lru_w_i"], p["lru_b_i"],
              p["lru_lambda"])
    return pl.pallas_call(
        _seqmix_kernel,
        grid=(bsz, nblk),
        in_specs=[blk(128), blk(512), blk(256), blk(256)] + [const(a) for a in params],
        out_specs=[blk(256), blk(256), pl.BlockSpec((1, 8, BLOCK), lambda b, t: (b, 0, t)), blk(128)],
        out_shape=[jax.ShapeDtypeStruct((bsz, tp, GROUP_W), MXU_DTYPE),
                   jax.ShapeDtypeStruct((bsz, tp, GROUP_W), MXU_DTYPE),
                   jax.ShapeDtypeStruct((bsz, 8, tp), F32),
                   jax.ShapeDtypeStruct((bsz, tp, 128), jnp.bfloat16)],
        scratch_shapes=[pltpu.VMEM((CONV_HALO + BLOCK, GROUP_W), F32),
                        pltpu.VMEM((LRU_HALO + BLOCK, GROUP_W), F32),
                        pltpu.VMEM((8, GROUP_W), F32),
                        pltpu.VMEM((8, 128), F32),
                        pltpu.VMEM((8, CONV_HALO + BLOCK, GROUP_W), F32)],
        compiler_params=_params("parallel", "arbitrary"),
        name="seq_mix",
    )(r3(z["ff"]), r3(z["cu"]), r3(z["lx"]), r3(z["lg"]), *params)


def _fox_kernel(qt_ref, cqt_ref, k_ref, kx_ref, vt_ref, o_ref, sa_ref, sb_ref):
    qi = pl.program_id(1)
    pairs = FOX_HEADS // 2
    two = 2 * BLOCK
    qt = qt_ref[...]
    cqt = cqt_ref[0]
    row = lax.broadcasted_iota(jnp.int32, (BLOCK, 1), 0)
    k_off = lax.broadcasted_iota(jnp.int32, (KEY_STEP, 1), 0)
    q_pos = qi * BLOCK + lax.broadcasted_iota(jnp.int32, (1, BLOCK), 1)
    q_pos2 = jnp.concatenate([q_pos, q_pos], axis=1)
    q_rhs = []
    for p in range(pairs):
        qp = qt[p * BLOCK:(p + 1) * BLOCK]
        halves, extras = [], []
        for c in range(2):
            h = 2 * p + c
            halves.append(jnp.where((row >= c * HEAD_DIM) & (row < (c + 1) * HEAD_DIM), qp, jnp.zeros_like(qp)))
            cq = cqt[h:h + 1]
            cq_hi = cq.astype(jnp.bfloat16).astype(F32)
            cq_mid = (cq - cq_hi).astype(jnp.bfloat16).astype(F32)
            cq_low = (cq - cq_hi - cq_mid).astype(jnp.bfloat16).astype(F32)
            is_piece = (row == h) | (row == FOX_HEADS + h) | (row == 2 * FOX_HEADS + h)
            extra = jnp.where(is_piece, -1.0, 0.0)
            for i, piece in enumerate((cq_hi, cq_mid, cq_low)):
                extra = jnp.where(row == 3 * FOX_HEADS + i, piece, extra)
            extras.append(extra)
        q_rhs.append(jnp.concatenate([jnp.concatenate(halves, axis=1),
                                      jnp.concatenate(extras, axis=1).astype(MXU_DTYPE)], axis=0))

    def qk(j, buf):
        ks = pl.multiple_of(j * KEY_STEP, KEY_STEP)
        kxb = kx_ref[0, pl.ds(ks, KEY_STEP), :]
        for p in range(pairs):
            keys = jnp.concatenate([k_ref[0, pl.ds(ks, KEY_STEP), p * BLOCK:(p + 1) * BLOCK], kxb], axis=1)
            buf[p] = _dot(keys, q_rhs[p])

    def update(j, buf, carry, causal_mask):
        ks = pl.multiple_of(j * KEY_STEP, KEY_STEP)
        out = []
        for p in range(pairs):
            m, l, acc = carry[p]
            s = buf[p]
            if causal_mask:
                s = jnp.where(ks + k_off <= q_pos2, s, NEG)
            m_new = jnp.maximum(m, jnp.max(s, axis=0, keepdims=True))
            alpha = jnp.exp2(m - m_new)
            pr = jnp.exp2(s - m_new)
            l = alpha * l + jnp.sum(pr, axis=0, keepdims=True)
            acc = alpha * acc + _dot(vt_ref[0, p * BLOCK:(p + 1) * BLOCK, pl.ds(ks, KEY_STEP)], pr.astype(MXU_DTYPE))
            out.append((m_new, l, acc))
        return tuple(out)

    init = tuple((jnp.full((1, two), NEG, F32), jnp.zeros((1, two), F32), jnp.zeros((BLOCK, two), F32))
                 for _ in range(pairs))
    n_full = (qi * BLOCK) // KEY_STEP

    qk(0, sa_ref)

    def pair_of_steps(t, carry):
        j = 2 * t
        qk(j + 1, sb_ref)
        carry = update(j, sa_ref, carry, False)
        qk(j + 2, sa_ref)
        return update(j + 1, sb_ref, carry, False)

    carry = lax.fori_loop(0, n_full // 2, pair_of_steps, init)

    def tail_two(carry):
        qk(n_full, sb_ref)
        return update(n_full, sb_ref, update(n_full - 1, sa_ref, carry, True), True)

    carry = lax.cond(n_full % 2 == 1, tail_two, lambda carry: update(n_full, sa_ref, carry, True), carry)
    outs = []
    for p in range(pairs):
        _, l, acc = carry[p]
        o_t = acc / l
        outs.append(jnp.where(row < HEAD_DIM, o_t[:, :BLOCK], o_t[:, BLOCK:]).T)
    o_ref[0] = jnp.concatenate(outs, axis=1).astype(o_ref.dtype)


def _fox(fqt, fk, kx, fvt, cumt):
    bsz, _, tp = cumt.shape
    nblk = tp // BLOCK
    full = lambda a: pl.BlockSpec((1,) + a.shape[1:], lambda b, i: (b, 0, 0))
    return pl.pallas_call(
        _fox_kernel,
        grid=(bsz, nblk),
        in_specs=[pl.BlockSpec((GROUP_W, BLOCK), lambda b, i: (0, b * nblk + i)),
                  pl.BlockSpec((1, 8, BLOCK), lambda b, i: (b, 0, i)), full(fk), full(kx), full(fvt)],
        out_specs=pl.BlockSpec((1, BLOCK, GROUP_W), lambda b, i: (b, i, 0)),
        out_shape=jax.ShapeDtypeStruct((bsz, tp, GROUP_W), MXU_DTYPE),
        scratch_shapes=[pltpu.VMEM((FOX_HEADS // 2, KEY_STEP, 2 * BLOCK), F32)] * 2,
        compiler_params=_params("parallel", "arbitrary"),
        name="fox",
    )(fqt, cumt, fk, kx, fvt)


def _dsa_kernel(iqt_ref, wht_ref, qlt_ref, ki_ref, c_ref, ct_ref, wuvt_ref, o_ref,
                sc_ref, sa_ref, sb_ref, da_ref, db_ref, *, topk):
    qi = pl.program_id(1)
    nkb = (qi * BLOCK) // KEY_STEP + 1
    k_off = lax.broadcasted_iota(jnp.int32, (KEY_STEP, 1), 0)
    q_pos = qi * BLOCK + lax.broadcasted_iota(jnp.int32, (1, BLOCK), 1)
    n_valid = q_pos - PAD + 1
    fold_rows = KEY_STEP // 8

    def fold(op, w):
        parts = [w[i * fold_rows:(i + 1) * fold_rows] for i in range(8)]
        return op(op(op(parts[0], parts[1]), op(parts[2], parts[3])), op(op(parts[4], parts[5]), op(parts[6], parts[7])))

    iqt = iqt_ref[...]
    wht = wht_ref[...]
    zeros = jnp.zeros((BLOCK - IDX_DIM, BLOCK), iqt.dtype)
    q_heads = jnp.concatenate(
        [jnp.concatenate([iqt[h * IDX_DIM:(h + 1) * IDX_DIM], zeros], axis=0) for h in range(IDX_HEADS)],
        axis=1)

    def head_dots(j, buf):
        ks = pl.multiple_of(j * KEY_STEP, KEY_STEP)
        buf[...] = _dot(ki_ref[0, pl.ds(ks, KEY_STEP), :], q_heads)

    def score_step(j, buf, stats):
        amax, s0, s1, s2 = stats
        sc = jnp.zeros((KEY_STEP, BLOCK), F32)
        for h in range(IDX_HEADS):
            sc = sc + jnp.maximum(buf[:, h * BLOCK:(h + 1) * BLOCK], 0.0) * wht[h:h + 1, :]
        k_pos = j * KEY_STEP + k_off
        valid = (k_pos <= q_pos) & (k_pos >= PAD)
        sc_ref[j] = jnp.where(valid, sc, SCORE_MASKED)
        sample = jnp.where(valid[:fold_rows], sc[:fold_rows], 0.0)
        return (jnp.maximum(amax, fold(jnp.maximum, jnp.abs(sc))),
                s0 + jnp.where(valid[:fold_rows], 1.0, 0.0), s1 + sample, s2 + sample * sample)

    last = nkb - 1
    head_dots(0, da_ref)

    def score_pair(t, stats):
        j = 2 * t
        head_dots(j + 1, db_ref)
        stats = score_step(j, da_ref, stats)
        head_dots(jnp.minimum(j + 2, last), da_ref)
        return score_step(j + 1, db_ref, stats)

    stats = lax.fori_loop(0, nkb // 2, score_pair, (jnp.zeros((fold_rows, BLOCK), F32),) * 4)
    amax, s0, s1, s2 = lax.cond(nkb % 2 == 1, lambda st: score_step(last, da_ref, st), lambda st: st, stats)
    bound = jnp.max(amax, axis=0, keepdims=True) * 1.0001 + 1e-30

    @pl.when(nkb % 2 == 1)
    def _():
        sc_ref[nkb] = jnp.full((KEY_STEP, BLOCK), SCORE_MASKED, F32)

    def count(*preds):
        def body(jj, accs):
            tiles = (sc_ref[2 * jj], sc_ref[2 * jj + 1])
            return tuple(tuple(acc + fold(jnp.add, jnp.where(pred(t), 1, 0)) for acc, t in zip(pair, tiles))
                         for pair, pred in zip(accs, preds))
        zeros = jnp.zeros((fold_rows, BLOCK), jnp.int32)
        accs = lax.fori_loop(0, (nkb + 1) // 2, body, tuple((zeros, zeros) for _ in preds))
        return [jnp.sum(a + b, axis=0, keepdims=True).astype(F32) for a, b in accs]

    kf = float(topk)
    take_all = n_valid <= topk
    c_ge0, c_gt0 = count(lambda s: s >= 0.0, lambda s: s > 0.0)
    positive = c_gt0 >= kf
    zero_tie = (c_ge0 >= kf) & jnp.logical_not(positive)
    lo = jnp.where(positive | zero_tie, 0.0, -bound)
    c_lo = jnp.where(positive | zero_tie, c_ge0, n_valid.astype(F32))
    hi = jnp.where(positive, bound, 0.0)
    c_hi = jnp.where(positive, 0.0, jnp.where(zero_tie, c_gt0, c_ge0))
    done = jnp.where(take_all | zero_tie | (c_lo == kf), 1, 0)

    n_s = jnp.maximum(jnp.sum(s0, axis=0, keepdims=True), 1.0)
    mean = jnp.sum(s1, axis=0, keepdims=True) / n_s
    std = jnp.sqrt(jnp.maximum(jnp.sum(s2, axis=0, keepdims=True) / n_s - mean * mean, 0.0))
    tail = jnp.clip(kf / jnp.maximum(n_valid, 1).astype(F32), 1e-6, 1.0 - 1e-6)
    upper = tail < 0.5
    t_q = jnp.sqrt(-2.0 * jnp.log(jnp.where(upper, tail, 1.0 - tail)))
    z_q = t_q - ((0.010328 * t_q + 0.802853) * t_q + 2.515517) / (((0.001308 * t_q + 0.189269) * t_q + 1.432788) * t_q + 1.0)
    guess = mean + jnp.where(upper, z_q, -z_q) * std

    def next_probe(lo, hi, c_lo, c_hi, it):
        mid = 0.5 * lo + 0.5 * hi
        inside = (mid > lo) & (mid < hi)
        log_lo = jnp.log(jnp.maximum(c_lo, 1.0))
        frac = (log_lo - np.log(kf)) / (log_lo - jnp.log(jnp.maximum(c_hi, 0.5)))
        probe = lo + (hi - lo) * jnp.clip(frac, 0.02, 0.98)
        turn = jnp.zeros_like(done) + it
        probe = jnp.where(turn == 0, guess, probe)
        use_probe = (turn % 4 != 3) & (probe > lo) & (probe < hi)
        return jnp.where(use_probe, probe, mid), jnp.where(inside, 1, 0)

    def search_step(state):
        lo, hi, c_lo, c_hi, done, probe, inside, it = state
        c, = count(lambda s: s >= probe)
        active = (done == 0) & (inside > 0)
        up = active & (c >= kf)
        down = active & (c < kf)
        lo, c_lo = jnp.where(up, probe, lo), jnp.where(up, c, c_lo)
        hi, c_hi = jnp.where(down, probe, hi), jnp.where(down, c, c_hi)
        done = jnp.where((done > 0) | (inside == 0) | (c_lo == kf), 1, 0)
        probe, inside = next_probe(lo, hi, c_lo, c_hi, it + 1)
        return lo, hi, c_lo, c_hi, done, probe, inside, it + 1

    def unfinished(state):
        return jnp.sum(1 - state[4])

    probe0, inside0 = next_probe(lo, hi, c_lo, c_hi, jnp.int32(0))
    state = (lo, hi, c_lo, c_hi, done, probe0, inside0, jnp.int32(0))
    state = lax.fori_loop(0, SEARCH_WARMUP, lambda _, st: search_step(st), state)

    def search_body(carry):
        state = search_step(carry[0])
        return state, unfinished(state)

    state, _ = lax.while_loop(lambda carry: carry[1] > 0, search_body, (state, unfinished(state)))
    thr, _, c_thr, c_above = state[:4]
    thr = jnp.where(take_all, 0.5 * SCORE_MASKED, thr)
    tied = jnp.logical_not(take_all) & (c_thr > kf)
    any_tied = jnp.max(jnp.where(tied, 1, 0)) > 0

    qlt = qlt_ref[...]
    q_lat = jnp.concatenate([qlt[h * DSA_LATENT:(h + 1) * DSA_LATENT] for h in range(DSA_HEADS)], axis=1)
    wide = DSA_HEADS * BLOCK

    def attend(select):
        def qk(j, buf):
            ks = pl.multiple_of(j * KEY_STEP, KEY_STEP)
            buf[...] = _dot(c_ref[0, pl.ds(ks, KEY_STEP), :], q_lat)

        def consume(j, buf, carry):
            m, l, acc, run = carry
            ks = pl.multiple_of(j * KEY_STEP, KEY_STEP)
            sel, run = select(sc_ref[j], run)
            s_all = buf[...]
            s_all = jnp.concatenate([jnp.where(sel, s_all[:, h * BLOCK:(h + 1) * BLOCK], NEG)
                                     for h in range(DSA_HEADS)], axis=1)
            m_new = jnp.maximum(m, jnp.max(s_all, axis=0, keepdims=True))
            alpha = jnp.exp2(m - m_new)
            pr = jnp.exp2(s_all - m_new)
            l = alpha * l + jnp.sum(pr, axis=0, keepdims=True)
            acc = alpha * acc + _dot(ct_ref[0, :, pl.ds(ks, KEY_STEP)], pr.astype(MXU_DTYPE))
            return m_new, l, acc, run

        qk(0, sa_ref)

        def pair(t, carry):
            j = 2 * t
            qk(j + 1, sb_ref)
            carry = consume(j, sa_ref, carry)
            qk(jnp.minimum(j + 2, last), sa_ref)
            return consume(j + 1, sb_ref, carry)

        init = (jnp.full((1, wide), NEG, F32), jnp.zeros((1, wide), F32), jnp.zeros((DSA_LATENT, wide), F32),
                jnp.zeros((1, BLOCK), F32))
        carry = lax.fori_loop(0, nkb // 2, pair, init)
        _, l, acc, _ = lax.cond(nkb % 2 == 1, lambda c: consume(last, sa_ref, c), lambda c: c, carry)
        o_lat = (acc / jnp.where(l > 0.0, l, 1.0)).astype(MXU_DTYPE)
        y_t = jnp.zeros((GROUP_W, BLOCK), F32)
        for h in range(DSA_HEADS):
            y_t = y_t + _dot(wuvt_ref[:, h * DSA_LATENT:(h + 1) * DSA_LATENT], o_lat[:, h * BLOCK:(h + 1) * BLOCK])
        o_ref[0] = y_t.T.astype(o_ref.dtype)

    @pl.when(jnp.logical_not(any_tied))
    def _():
        attend(lambda s, run: (s >= thr, run))

    @pl.when(any_tied)
    def _():
        quota = jnp.where(tied, kf - c_above, 2.0 ** 30)
        r_i = lax.broadcasted_iota(jnp.int32, (KEY_STEP, KEY_STEP), 0)
        c_i = lax.broadcasted_iota(jnp.int32, (KEY_STEP, KEY_STEP), 1)
        tri = jnp.where(c_i <= r_i, 1.0, 0.0).astype(MXU_DTYPE)

        def select(s, run):
            eqf = jnp.where(s == thr, 1.0, 0.0)
            prefix = _dot(tri, eqf.astype(MXU_DTYPE)) + run
            within = jnp.where(prefix <= quota, eqf, 0.0)
            sel = (jnp.where(s > thr, 1.0, 0.0) + within) > 0.5
            return sel, run + jnp.sum(eqf, axis=0, keepdims=True)

        attend(select)


def _dsa(iqt, wht, qlt, ki, c, ct, wuvt, topk):
    bsz, tpk, _ = ki.shape
    tp = iqt.shape[1] // bsz
    nblk = tp // BLOCK
    qcol = lambda n: pl.BlockSpec((n, BLOCK), lambda b, i: (0, b * nblk + i))
    full = lambda a: pl.BlockSpec((1,) + a.shape[1:], lambda b, i: (b, 0, 0))
    return pl.pallas_call(
        functools.partial(_dsa_kernel, topk=topk),
        grid=(bsz, nblk),
        in_specs=[qcol(iqt.shape[0]), qcol(wht.shape[0]), qcol(qlt.shape[0]), full(ki), full(c), full(ct),
                  pl.BlockSpec(wuvt.shape, lambda b, i: (0, 0))],
        out_specs=pl.BlockSpec((1, BLOCK, GROUP_W), lambda b, i: (b, i, 0)),
        out_shape=jax.ShapeDtypeStruct((bsz, tp, GROUP_W), MXU_DTYPE),
        scratch_shapes=[pltpu.VMEM((tpk // KEY_STEP + 1, KEY_STEP, BLOCK), F32),
                        pltpu.VMEM((KEY_STEP, DSA_HEADS * BLOCK), F32), pltpu.VMEM((KEY_STEP, DSA_HEADS * BLOCK), F32),
                        pltpu.VMEM((KEY_STEP, IDX_HEADS * BLOCK), F32), pltpu.VMEM((KEY_STEP, IDX_HEADS * BLOCK), F32)],
        compiler_params=_params("parallel", "arbitrary"),
        name="dsa",
    )(iqt, wht, qlt, ki, c, ct, wuvt)


def _outproj_kernel(h_ref, yf_ref, yc_ref, yl_ref, yd_ref, w_ref, g_ref, o_ref, *, tm, tp):
    mix = _dot(yf_ref[...], w_ref[0:GROUP_W, :])
    mix = mix + _dot(yc_ref[...], w_ref[GROUP_W:2 * GROUP_W, :])
    mix = mix + _dot(yl_ref[...], w_ref[2 * GROUP_W:3 * GROUP_W, :])
    mix = mix + _dot(yd_ref[...], w_ref[3 * GROUP_W:4 * GROUP_W, :])
    row = (pl.program_id(0) * tm) % tp + lax.broadcasted_iota(jnp.int32, (tm, 1), 0)
    o_ref[...] = jnp.where(row >= PAD, h_ref[...] + _rms(mix, g_ref[...]), 0.0)


def _outproj(h, yf, yc, yl, yd, w, g, tp):
    rows, d = h.shape
    tm = _row_tile(rows)
    row = lambda i: (i, 0)
    const = lambda i: (0, 0)
    return pl.pallas_call(
        functools.partial(_outproj_kernel, tm=tm, tp=tp),
        grid=(rows // tm,),
        in_specs=[pl.BlockSpec((tm, d), row)] + [pl.BlockSpec((tm, GROUP_W), row)] * 4
                 + [pl.BlockSpec(w.shape, const), pl.BlockSpec((1, d), const)],
        out_specs=pl.BlockSpec((tm, d), row),
        out_shape=jax.ShapeDtypeStruct((rows, d), F32),
        compiler_params=_params("parallel"),
        name="out_proj",
    )(h, yf, yc, yl, yd, w, g)


def _pack_w_in(w_in):
    offs = np.cumsum((0,) + SPLIT_SIZES)
    fq, fk, fv, ff, cu, lx, lg, dq, dkv, iq, ik, iw = (w_in[..., offs[i]:offs[i + 1]] for i in range(12))
    padc = lambda a, n: jnp.pad(a, ((0, 0), (0, 0), (0, n - a.shape[-1])))
    cols = [fq * (HEAD_DIM ** -0.5 * LOG2E), fk, fv, padc(ff, 128), cu, lx, lg, dq, dkv, iq,
            jnp.tile(ik, (1, 1, IDX_HEADS)), padc(iw, 128)]
    return jnp.concatenate(cols, axis=-1).astype(MXU_DTYPE)


def _block_diag(w):
    depth, n, a, b = w.shape
    eye = jnp.eye(n, dtype=w.dtype)
    return jnp.einsum("lnab,nm->lnamb", w, eye).reshape(depth, n * a, n * b)


def _lane_pad(a, n):
    return jnp.pad(a, [(0, 0)] * (a.ndim - 1) + [(0, n - a.shape[-1])])


def kernel(x, meta_tokens, norm_g, ffn_w_in, ffn_w_out, w_in, w_out, fox_b_f, conv_dw_w, conv_dw_b, conv_ln_g,
           conv_ln_b, lru_conv_w, lru_conv_b, lru_w_a, lru_b_a, lru_w_i, lru_b_i, lru_lambda, dsa_kv_norm_g,
           dsa_w_uk, dsa_w_uv, idx_k_ln_g, idx_k_ln_b):
    bsz, seq, d = x.shape
    depth = norm_g.shape[0]
    assert seq % BLOCK == 0 and d % 128 == 0
    topk = min(TOPK_MAX, seq // 4)
    tp = PAD + N_META + seq
    rows = bsz * tp

    ffn_w_in_m = ffn_w_in.astype(MXU_DTYPE)
    ffn_w_out_m = ffn_w_out.astype(MXU_DTYPE)
    w_in_m = _pack_w_in(w_in)
    w_out_m = w_out.astype(MXU_DTYPE)
    wuk_m = (_block_diag(dsa_w_uk.transpose(0, 1, 3, 2)) * (HEAD_DIM ** -0.5 * LOG2E)).astype(MXU_DTYPE)
    wuvt_m = _block_diag(dsa_w_uv).transpose(0, 2, 1).astype(MXU_DTYPE)
    wa_m = _block_diag(lru_w_a).astype(MXU_DTYPE)
    wi_m = _block_diag(lru_w_i).astype(MXU_DTYPE)
    row2 = lambda a: a[:, None, :]
    dww = jnp.pad(conv_dw_w, ((0, 0), (0, CONV_HALO - CONV_WIDTH), (0, 0)))
    lcw = jnp.pad(lru_conv_w, ((0, 0), (0, LRU_HALO - LRU_CONV_WIDTH), (0, 0)))
    ln_g8 = row2(jnp.tile(idx_k_ln_g, (1, IDX_HEADS)))
    ln_b8 = row2(jnp.tile(idx_k_ln_b, (1, IDX_HEADS)))

    meta = jnp.broadcast_to(meta_tokens[None].astype(x.dtype), (bsz, N_META, d))
    h = jnp.concatenate([jnp.zeros((bsz, PAD, d), x.dtype), meta, x], axis=1).reshape(rows, d)

    for l in range(depth):
        g = norm_g[l][:, None, :]
        h = _ffn(h, g[0], g[1], ffn_w_in_m[l, 0], ffn_w_out_m[l, 0])
        z = _inproj(h, g[2], w_in_m[l], wuk_m[l], row2(dsa_kv_norm_g)[l], ln_g8[l], ln_b8[l])
        seq_params = {
            "fox_b_f": _lane_pad(fox_b_f[l][None], 128), "conv_dw_w": dww[l], "conv_dw_b": row2(conv_dw_b)[l],
            "conv_ln_g": row2(conv_ln_g)[l], "conv_ln_b": row2(conv_ln_b)[l], "lru_conv_w": lcw[l],
            "lru_conv_b": row2(lru_conv_b)[l], "lru_w_a": wa_m[l], "lru_b_a": row2(lru_b_a)[l],
            "lru_w_i": wi_m[l], "lru_b_i": row2(lru_b_i)[l], "lru_lambda": row2(lru_lambda)[l]}
        y_conv, y_lru, cumt, kx = _seqmix(z, bsz, tp, seq_params)
        r3 = lambda a: a.reshape(bsz, tp, a.shape[-1])
        tpk = -(-tp // KEY_STEP) * KEY_STEP
        keys = lambda a: jnp.pad(r3(a), ((0, 0), (0, tpk - tp), (0, 0)))
        keys_t = lambda a: jnp.pad(a.reshape(a.shape[0], bsz, tp).transpose(1, 0, 2), ((0, 0), (0, 0), (0, tpk - tp)))
        y_fox = _fox(z["fqT"], keys(z["fk"]), keys(kx), keys_t(z["fvT"]), cumt)
        y_dsa = _dsa(z["iqT"], z["whT"], z["qlT"], keys(z["ki"]), keys(z["c"]), keys_t(z["cT"]), wuvt_m[l], topk)
        r2 = lambda a: a.reshape(rows, a.shape[-1])
        h = _outproj(h, r2(y_fox), r2(y_conv), r2(y_lru), r2(y_dsa), w_out_m[l], g[3], tp)
        h = _ffn(h, g[4], g[5], ffn_w_in_m[l, 1], ffn_w_out_m[l, 1])

    return h.reshape(bsz, tp, d)[:, PAD + N_META:]
```

```python
import functools

import jax
import jax.numpy as jnp
import numpy as np
from jax import lax
from jax.experimental import pallas as pl
from jax.experimental.pallas import tpu as pltpu

N_META = 16
BLOCK = 128
PAD = BLOCK - N_META
KEY_STEP = 512
GROUP_W = 256
HEAD_DIM = 64
FOX_HEADS = 4
CONV_WIDTH = 31
CONV_HALO = 32
LRU_BLOCKS = 4
LRU_CONV_WIDTH = 4
LRU_HALO = 8
LRU_C = 8.0
DSA_HEADS = 4
DSA_LATENT = 128
IDX_HEADS = 8
IDX_DIM = 32
TOPK_MAX = 256
RMS_EPS = 1e-6
LN_EPS = 1e-5
SPLIT_SIZES = (GROUP_W, GROUP_W, GROUP_W, FOX_HEADS, 2 * GROUP_W, GROUP_W, GROUP_W,
               DSA_HEADS * HEAD_DIM, DSA_LATENT, IDX_HEADS * IDX_DIM, IDX_DIM, IDX_HEADS)

MXU_DTYPE = jnp.bfloat16
F32 = jnp.float32
NEG = -1e30
LOG2E = 1.4426950408889634
SCORE_MASKED = -3e38
SEARCH_WARMUP = 14
VMEM_LIMIT = 56 * 1024 * 1024

_INPROJ_GROUPS = (("fq", 256), ("fk", 256), ("fv", 256), ("ff", 128), ("cu", 512), ("lx", 256),
                  ("lg", 256), ("dq", 256), ("dkv", 128), ("iq", 256), ("ik", 256), ("iw", 128))
_INPROJ_OFF = {}
_o = 0
for _n, _w in _INPROJ_GROUPS:
    _INPROJ_OFF[_n] = (_o, _w)
    _o += _w
INPROJ_COLS = _o


def _params(*sem):
    return pltpu.CompilerParams(dimension_semantics=sem, vmem_limit_bytes=VMEM_LIMIT)


def _rms(x, g):
    return x * lax.rsqrt(jnp.mean(x * x, axis=-1, keepdims=True) + RMS_EPS) * g


def _dot(a, b):
    return jnp.dot(a, b, preferred_element_type=F32)


def _log_sigmoid(x):
    return jnp.minimum(x, 0.0) - jnp.log1p(jnp.exp(-jnp.abs(x)))


def _expm1(y):
    e = jnp.exp(y)
    regular = (e != 1.0) & (e > 0.0)
    r = (e - 1.0) * y / jnp.log(jnp.where(regular, e, 2.0))
    return jnp.where(regular, r, jnp.where(e > 0.0, y, -1.0))


def _row_tile(rows):
    return 640 if rows % 640 == 0 else BLOCK


def _ffn_kernel(h_ref, gpre_ref, gpost_ref, win_ref, wout_ref, o_ref, *, d_ff, chunk):
    x = h_ref[...]
    xn = _rms(x, gpre_ref[...]).astype(MXU_DTYPE)
    acc = jnp.zeros(x.shape, F32)
    for c in range(d_ff // chunk):
        gate = _dot(xn, win_ref[:, c * chunk:(c + 1) * chunk])
        up = _dot(xn, win_ref[:, d_ff + c * chunk:d_ff + (c + 1) * chunk])
        a = (gate * jax.nn.sigmoid(gate) * up).astype(MXU_DTYPE)
        acc = acc + _dot(a, wout_ref[c * chunk:(c + 1) * chunk, :])
    o_ref[...] = x + 0.5 * _rms(acc, gpost_ref[...])


def _ffn(h, g_pre, g_post, w_in, w_out):
    rows, d = h.shape
    d_ff = w_out.shape[0]
    tm = _row_tile(rows)
    chunk = 512 if d_ff % 512 == 0 else d_ff
    const = lambda i: (0, 0)
    return pl.pallas_call(
        functools.partial(_ffn_kernel, d_ff=d_ff, chunk=chunk),
        grid=(rows // tm,),
        in_specs=[pl.BlockSpec((tm, d), lambda i: (i, 0)),
                  pl.BlockSpec((1, d), const), pl.BlockSpec((1, d), const),
                  pl.BlockSpec(w_in.shape, const, pipeline_mode=pl.Buffered(1)),
                  pl.BlockSpec(w_out.shape, const, pipeline_mode=pl.Buffered(1))],
        out_specs=pl.BlockSpec((tm, d), lambda i: (i, 0)),
        out_shape=jax.ShapeDtypeStruct((rows, d), F32),
        compiler_params=_params("parallel"),
        name="ffn",
    )(h, g_pre, g_post, w_in, w_out)


def _inproj_kernel(h_ref, g_ref, w_ref, wuk_ref, kvg_ref, lng_ref, lnb_ref,
                   fqt_ref, fk_ref, fvt_ref, ff_ref, cu_ref, lx_ref, lg_ref,
                   qlt_ref, c_ref, ct_ref, iqt_ref, ki_ref, wht_ref):
    xn = _rms(h_ref[...], g_ref[...]).astype(MXU_DTYPE)

    def proj(name):
        lo, n = _INPROJ_OFF[name]
        return _dot(xn, w_ref[:, lo:lo + n])

    fqt_ref[...] = proj("fq").T.astype(MXU_DTYPE)
    fk_ref[...] = proj("fk").astype(MXU_DTYPE)
    fvt_ref[...] = proj("fv").T.astype(MXU_DTYPE)
    ff_ref[...] = proj("ff")
    cu_ref[...] = proj("cu")
    lx_ref[...] = proj("lx")
    lg_ref[...] = proj("lg")
    qlt_ref[...] = _dot(proj("dq").astype(MXU_DTYPE), wuk_ref[...]).T.astype(MXU_DTYPE)
    c = _rms(proj("dkv"), kvg_ref[...])
    c_ref[...] = c.astype(MXU_DTYPE)
    ct_ref[...] = c.T.astype(MXU_DTYPE)
    iqt_ref[...] = proj("iq").T.astype(MXU_DTYPE)
    ik = proj("ik")
    mu = jnp.mean(ik, axis=-1, keepdims=True)
    var = jnp.mean(jnp.square(ik - mu), axis=-1, keepdims=True)
    ki = (ik - mu) * lax.rsqrt(var + LN_EPS) * lng_ref[...] + lnb_ref[...]
    lane = lax.broadcasted_iota(jnp.int32, (1, BLOCK), 1)
    ki_ref[...] = jnp.where(lane < IDX_DIM, ki[:, :BLOCK], 0.0).astype(MXU_DTYPE)
    wht_ref[...] = (proj("iw") * (IDX_HEADS ** -0.5 * IDX_DIM ** -0.5)).T[:IDX_HEADS, :]


def _inproj(h, g, w, wuk, kvg, lng, lnb):
    rows, d = h.shape
    tm = _row_tile(rows)
    const = lambda i: (0, 0)
    row = lambda i: (i, 0)
    col = lambda i: (0, i)
    outs = (("fqT", 256, MXU_DTYPE, True), ("fk", 256, MXU_DTYPE, False), ("fvT", 256, MXU_DTYPE, True),
            ("ff", 128, F32, False), ("cu", 512, F32, False), ("lx", 256, F32, False), ("lg", 256, F32, False),
            ("qlT", 512, MXU_DTYPE, True), ("c", 128, MXU_DTYPE, False), ("cT", 128, MXU_DTYPE, True),
            ("iqT", 256, MXU_DTYPE, True), ("ki", BLOCK, MXU_DTYPE, False), ("whT", IDX_HEADS, F32, True))
    res = pl.pallas_call(
        _inproj_kernel,
        grid=(rows // tm,),
        in_specs=[pl.BlockSpec((tm, d), row), pl.BlockSpec((1, d), const),
                  pl.BlockSpec(w.shape, const, pipeline_mode=pl.Buffered(1)),
                  pl.BlockSpec(wuk.shape, const),
                  pl.BlockSpec(kvg.shape, const), pl.BlockSpec(lng.shape, const), pl.BlockSpec(lnb.shape, const)],
        out_specs=[pl.BlockSpec((n, tm), col) if t else pl.BlockSpec((tm, n), row) for _, n, _, t in outs],
        out_shape=[jax.ShapeDtypeStruct((n, rows) if t else (rows, n), dt) for _, n, dt, t in outs],
        compiler_params=_params("parallel"),
        name="in_proj",
    )(h, g, w, wuk, kvg, lng, lnb)
    return dict(zip([n for n, _, _, _ in outs], res))


def _shift_rows(x, s, fill, rows):
    return jnp.where(rows >= s, pltpu.roll(x, s, axis=0), fill)


def _seqmix_kernel(ff_ref, cu_ref, lx_ref, lg_ref, bf_ref, dww_ref, dwb_ref, lng_ref, lnb_ref,
                   lcw_ref, lcb_ref, wa_ref, ba_ref, wi_ref, bi_ref, lam_ref,
                   yc_ref, yl_ref, cumt_ref, kx_ref,
                   glu_buf, lx_buf, h_carry, cum_carry, shift_buf):
    t = pl.program_id(1)

    @pl.when(t == 0)
    def _():
        glu_buf[...] = jnp.zeros(glu_buf.shape, F32)
        lx_buf[...] = jnp.zeros(lx_buf.shape, F32)
        h_carry[...] = jnp.zeros(h_carry.shape, F32)
        cum_carry[...] = jnp.zeros(cum_carry.shape, F32)

    rows = lax.broadcasted_iota(jnp.int32, (BLOCK, 1), 0)
    valid = (t * BLOCK + rows) >= PAD

    cu = cu_ref[0]
    glu = jnp.where(valid, cu[:, :GROUP_W] * jax.nn.sigmoid(cu[:, GROUP_W:]), 0.0)
    glu_buf[CONV_HALO:, :] = glu
    acc = jnp.zeros((BLOCK, GROUP_W), F32) + dwb_ref[...]
    span = BLOCK + CONV_HALO - 8
    for r in range(1, 8):
        shift_buf[r, :span, :] = glu_buf[r:r + span, :]
    for k in range(CONV_WIDTH):
        lo = CONV_HALO - (CONV_WIDTH - 1) + k
        r, base = lo % 8, lo - lo % 8
        window = glu_buf[base:base + BLOCK, :] if r == 0 else shift_buf[r, base:base + BLOCK, :]
        acc = acc + dww_ref[k:k + 1, :] * window
    glu_buf[:CONV_HALO, :] = glu_buf[BLOCK:, :]
    mu = jnp.mean(acc, axis=-1, keepdims=True)
    var = jnp.mean(jnp.square(acc - mu), axis=-1, keepdims=True)
    hc = (acc - mu) * lax.rsqrt(var + LN_EPS) * lng_ref[...] + lnb_ref[...]
    yc_ref[0] = (hc * jax.nn.sigmoid(hc)).astype(yc_ref.dtype)

    lx_buf[LRU_HALO:, :] = jnp.where(valid, lx_ref[0], 0.0)
    xc = jnp.zeros((BLOCK, GROUP_W), F32) + lcb_ref[...]
    for k in range(LRU_CONV_WIDTH):
        lo = LRU_HALO - (LRU_CONV_WIDTH - 1) + k
        xc = xc + lcw_ref[k:k + 1, :] * lx_buf[lo:lo + BLOCK, :]
    lx_buf[:LRU_HALO, :] = lx_buf[BLOCK:, :]
    xcm = xc.astype(MXU_DTYPE)
    r = jax.nn.sigmoid(_dot(xcm, wa_ref[...]) + ba_ref[...])
    gi = jax.nn.sigmoid(_dot(xcm, wi_ref[...]) + bi_ref[...])
    log_a = LRU_C * r * _log_sigmoid(lam_ref[...])
    a = jnp.exp(log_a)
    u = jnp.where(valid, jnp.sqrt(-_expm1(2.0 * log_a)) * (gi * xc), 0.0)
    s = 1
    while s < BLOCK:
        u = a * _shift_rows(u, s, 0.0, rows) + u
        a = a * _shift_rows(a, s, 1.0, rows)
        s *= 2
    hl = u + a * h_carry[0:1, :]
    h_carry[...] = jnp.broadcast_to(hl[BLOCK - 1:BLOCK, :], h_carry.shape)
    g = lg_ref[0]
    gelu = 0.5 * g * (1.0 + jnp.tanh(np.sqrt(2.0 / np.pi).astype(np.float32) * (g + 0.044715 * g * g * g)))
    yl_ref[0] = (hl * gelu).astype(yl_ref.dtype)

    cs = _log_sigmoid(ff_ref[0] + bf_ref[...])
    s = 1
    while s < BLOCK:
        cs = cs + _shift_rows(cs, s, 0.0, rows)
        s *= 2
    cs = cs + cum_carry[0:1, :]
    cum_carry[...] = jnp.broadcast_to(cs[BLOCK - 1:BLOCK, :], cum_carry.shape)
    cs = cs * LOG2E
    cumt_ref[0] = cs.T[:8, :]
    lanes = lax.broadcasted_iota(jnp.int32, (1, BLOCK), 1)
    ck = jnp.where(lanes < FOX_HEADS, jnp.where(valid, cs, -NEG), 0.0)
    hi = ck.astype(jnp.bfloat16).astype(F32)
    mid = (ck - hi).astype(jnp.bfloat16).astype(F32)
    low = (ck - hi - mid).astype(jnp.bfloat16).astype(F32)
    ones = jnp.where((lanes >= 3 * FOX_HEADS) & (lanes < 3 * FOX_HEADS + 3), 1.0, 0.0)
    kx_ref[0] = (hi + pltpu.roll(mid, FOX_HEADS, axis=1) + pltpu.roll(low, 2 * FOX_HEADS, axis=1)
                 + ones).astype(kx_ref.dtype)


def _seqmix(z, bsz, tp, p):
    nblk = tp // BLOCK
    blk = lambda n: pl.BlockSpec((1, BLOCK, n), lambda b, t: (b, t, 0))
    const = lambda a: pl.BlockSpec(a.shape, lambda b, t: (0, 0))
    r3 = lambda a: a.reshape(bsz, tp, a.shape[-1])
    params = (p["fox_b_f"], p["conv_dw_w"], p["conv_dw_b"], p["conv_ln_g"], p["conv_ln_b"],
              p["lru_conv_w"], p["lru_conv_b"], p["lru_w_a"], p["lru_b_a"], p["lru_w_i"], p["lru_b_i"],
              p["lru_lambda"])
    return pl.pallas_call(
        _seqmix_kernel,
        grid=(bsz, nblk),
        in_specs=[blk(128), blk(512), blk(256), blk(256)] + [const(a) for a in params],
        out_specs=[blk(256), blk(256), pl.BlockSpec((1, 8, BLOCK), lambda b, t: (b, 0, t)), blk(128)],
        out_shape=[jax.ShapeDtypeStruct((bsz, tp, GROUP_W), MXU_DTYPE),
                   jax.ShapeDtypeStruct((bsz, tp, GROUP_W), MXU_DTYPE),
                   jax.ShapeDtypeStruct((bsz, 8, tp), F32),
                   jax.ShapeDtypeStruct((bsz, tp, 128), jnp.bfloat16)],
        scratch_shapes=[pltpu.VMEM((CONV_HALO + BLOCK, GROUP_W), F32),
                        pltpu.VMEM((LRU_HALO + BLOCK, GROUP_W), F32),
                        pltpu.VMEM((8, GROUP_W), F32),
                        pltpu.VMEM((8, 128), F32),
                        pltpu.VMEM((8, CONV_HALO + BLOCK, GROUP_W), F32)],
        compiler_params=_params("parallel", "arbitrary"),
        name="seq_mix",
    )(r3(z["ff"]), r3(z["cu"]), r3(z["lx"]), r3(z["lg"]), *params)


def _fox_kernel(qt_ref, cqt_ref, k_ref, kx_ref, vt_ref, o_ref, sa_ref, sb_ref):
    qi = pl.program_id(1)
    pairs = FOX_HEADS // 2
    two = 2 * BLOCK
    qt = qt_ref[...]
    cqt = cqt_ref[0]
    row = lax.broadcasted_iota(jnp.int32, (BLOCK, 1), 0)
    k_off = lax.broadcasted_iota(jnp.int32, (KEY_STEP, 1), 0)
    q_pos = qi * BLOCK + lax.broadcasted_iota(jnp.int32, (1, BLOCK), 1)
    q_pos2 = jnp.concatenate([q_pos, q_pos], axis=1)
    q_rhs = []
    for p in range(pairs):
        qp = qt[p * BLOCK:(p + 1) * BLOCK]
        halves, extras = [], []
        for c in range(2):
            h = 2 * p + c
            halves.append(jnp.where((row >= c * HEAD_DIM) & (row < (c + 1) * HEAD_DIM), qp, jnp.zeros_like(qp)))
            cq = cqt[h:h + 1]
            cq_hi = cq.astype(jnp.bfloat16).astype(F32)
            cq_mid = (cq - cq_hi).astype(jnp.bfloat16).astype(F32)
            cq_low = (cq - cq_hi - cq_mid).astype(jnp.bfloat16).astype(F32)
            is_piece = (row == h) | (row == FOX_HEADS + h) | (row == 2 * FOX_HEADS + h)
            extra = jnp.where(is_piece, -1.0, 0.0)
            for i, piece in enumerate((cq_hi, cq_mid, cq_low)):
                extra = jnp.where(row == 3 * FOX_HEADS + i, piece, extra)
            extras.append(extra)
        q_rhs.append(jnp.concatenate([jnp.concatenate(halves, axis=1),
                                      jnp.concatenate(extras, axis=1).astype(MXU_DTYPE)], axis=0))

    def qk(j, buf):
        ks = pl.multiple_of(j * KEY_STEP, KEY_STEP)
        kxb = kx_ref[0, pl.ds(ks, KEY_STEP), :]
        for p in range(pairs):
            keys = jnp.concatenate([k_ref[0, pl.ds(ks, KEY_STEP), p * BLOCK:(p + 1) * BLOCK], kxb], axis=1)
            buf[p] = _dot(keys, q_rhs[p])

    def update(j, buf, carry, causal_mask):
        ks = pl.multiple_of(j * KEY_STEP, KEY_STEP)
        out = []
        for p in range(pairs):
            m, l, acc = carry[p]
            s = buf[p]
            if causal_mask:
                s = jnp.where(ks + k_off <= q_pos2, s, NEG)
            m_new = jnp.maximum(m, jnp.max(s, axis=0, keepdims=True))
            alpha = jnp.exp2(m - m_new)
            pr = jnp.exp2(s - m_new)
            l = alpha * l + jnp.sum(pr, axis=0, keepdims=True)
            acc = alpha * acc + _dot(vt_ref[0, p * BLOCK:(p + 1) * BLOCK, pl.ds(ks, KEY_STEP)], pr.astype(MXU_DTYPE))
            out.append((m_new, l, acc))
        return tuple(out)

    init = tuple((jnp.full((1, two), NEG, F32), jnp.zeros((1, two), F32), jnp.zeros((BLOCK, two), F32))
                 for _ in range(pairs))
    n_full = (qi * BLOCK) // KEY_STEP

    qk(0, sa_ref)

    def pair_of_steps(t, carry):
        j = 2 * t
        qk(j + 1, sb_ref)
        carry = update(j, sa_ref, carry, False)
        qk(j + 2, sa_ref)
        return update(j + 1, sb_ref, carry, False)

    carry = lax.fori_loop(0, n_full // 2, pair_of_steps, init)

    def tail_two(carry):
        qk(n_full, sb_ref)
        return update(n_full, sb_ref, update(n_full - 1, sa_ref, carry, True), True)

    carry = lax.cond(n_full % 2 == 1, tail_two, lambda carry: update(n_full, sa_ref, carry, True), carry)
    outs = []
    for p in range(pairs):
        _, l, acc = carry[p]
        o_t = acc / l
        outs.append(jnp.where(row < HEAD_DIM, o_t[:, :BLOCK], o_t[:, BLOCK:]).T)
    o_ref[0] = jnp.concatenate(outs, axis=1).astype(o_ref.dtype)


def _fox(fqt, fk, kx, fvt, cumt):
    bsz, _, tp = cumt.shape
    nblk = tp // BLOCK
    full = lambda a: pl.BlockSpec((1,) + a.shape[1:], lambda b, i: (b, 0, 0))
    return pl.pallas_call(
        _fox_kernel,
        grid=(bsz, nblk),
        in_specs=[pl.BlockSpec((GROUP_W, BLOCK), lambda b, i: (0, b * nblk + i)),
                  pl.BlockSpec((1, 8, BLOCK), lambda b, i: (b, 0, i)), full(fk), full(kx), full(fvt)],
        out_specs=pl.BlockSpec((1, BLOCK, GROUP_W), lambda b, i: (b, i, 0)),
        out_shape=jax.ShapeDtypeStruct((bsz, tp, GROUP_W), MXU_DTYPE),
        scratch_shapes=[pltpu.VMEM((FOX_HEADS // 2, KEY_STEP, 2 * BLOCK), F32)] * 2,
        compiler_params=_params("parallel", "arbitrary"),
        name="fox",
    )(fqt, cumt, fk, kx, fvt)


def _dsa_kernel(iqt_ref, wht_ref, qlt_ref, ki_ref, c_ref, ct_ref, wuvt_ref, o_ref,
                sc_ref, sa_ref, sb_ref, da_ref, db_ref, *, topk):
    qi = pl.program_id(1)
    nkb = (qi * BLOCK) // KEY_STEP + 1
    k_off = lax.broadcasted_iota(jnp.int32, (KEY_STEP, 1), 0)
    q_pos = qi * BLOCK + lax.broadcasted_iota(jnp.int32, (1, BLOCK), 1)
    n_valid = q_pos - PAD + 1
    fold_rows = KEY_STEP // 8

    def fold(op, w):
        parts = [w[i * fold_rows:(i + 1) * fold_rows] for i in range(8)]
        return op(op(op(parts[0], parts[1]), op(parts[2], parts[3])), op(op(parts[4], parts[5]), op(parts[6], parts[7])))

    iqt = iqt_ref[...]
    wht = wht_ref[...]
    zeros = jnp.zeros((BLOCK - IDX_DIM, BLOCK), iqt.dtype)
    q_heads = jnp.concatenate(
        [jnp.concatenate([iqt[h * IDX_DIM:(h + 1) * IDX_DIM], zeros], axis=0) for h in range(IDX_HEADS)],
        axis=1)

    def head_dots(j, buf):
        ks = pl.multiple_of(j * KEY_STEP, KEY_STEP)
        buf[...] = _dot(ki_ref[0, pl.ds(ks, KEY_STEP), :], q_heads)

    def score_step(j, buf, stats):
        amax, s0, s1, s2 = stats
        sc = jnp.zeros((KEY_STEP, BLOCK), F32)
        for h in range(IDX_HEADS):
            sc = sc + jnp.maximum(buf[:, h * BLOCK:(h + 1) * BLOCK], 0.0) * wht[h:h + 1, :]
        k_pos = j * KEY_STEP + k_off
        valid = (k_pos <= q_pos) & (k_pos >= PAD)
        sc_ref[j] = jnp.where(valid, sc, SCORE_MASKED)
        sample = jnp.where(valid[:fold_rows], sc[:fold_rows], 0.0)
        return (jnp.maximum(amax, fold(jnp.maximum, jnp.abs(sc))),
                s0 + jnp.where(valid[:fold_rows], 1.0, 0.0), s1 + sample, s2 + sample * sample)

    last = nkb - 1
    head_dots(0, da_ref)

    def score_pair(t, stats):
        j = 2 * t
        head_dots(j + 1, db_ref)
        stats = score_step(j, da_ref, stats)
        head_dots(jnp.minimum(j + 2, last), da_ref)
        return score_step(j + 1, db_ref, stats)

    stats = lax.fori_loop(0, nkb // 2, score_pair, (jnp.zeros((fold_rows, BLOCK), F32),) * 4)
    amax, s0, s1, s2 = lax.cond(nkb % 2 == 1, lambda st: score_step(last, da_ref, st), lambda st: st, stats)
    bound = jnp.max(amax, axis=0, keepdims=True) * 1.0001 + 1e-30

    @pl.when(nkb % 2 == 1)
    def _():
        sc_ref[nkb] = jnp.full((KEY_STEP, BLOCK), SCORE_MASKED, F32)

    def count(*preds):
        def body(jj, accs):
            tiles = (sc_ref[2 * jj], sc_ref[2 * jj + 1])
            return tuple(tuple(acc + fold(jnp.add, jnp.where(pred(t), 1, 0)) for acc, t in zip(pair, tiles))
                         for pair, pred in zip(accs, preds))
        zeros = jnp.zeros((fold_rows, BLOCK), jnp.int32)
        accs = lax.fori_loop(0, (nkb + 1) // 2, body, tuple((zeros, zeros) for _ in preds))
        return [jnp.sum(a + b, axis=0, keepdims=True).astype(F32) for a, b in accs]

    kf = float(topk)
    take_all = n_valid <= topk
    c_ge0, c_gt0 = count(lambda s: s >= 0.0, lambda s: s > 0.0)
    positive = c_gt0 >= kf
    zero_tie = (c_ge0 >= kf) & jnp.logical_not(positive)
    lo = jnp.where(positive | zero_tie, 0.0, -bound)
    c_lo = jnp.where(positive | zero_tie, c_ge0, n_valid.astype(F32))
    hi = jnp.where(positive, bound, 0.0)
    c_hi = jnp.where(positive, 0.0, jnp.where(zero_tie, c_gt0, c_ge0))
    done = jnp.where(take_all | zero_tie | (c_lo == kf), 1, 0)

    n_s = jnp.maximum(jnp.sum(s0, axis=0, keepdims=True), 1.0)
    mean = jnp.sum(s1, axis=0, keepdims=True) / n_s
    std = jnp.sqrt(jnp.maximum(jnp.sum(s2, axis=0, keepdims=True) / n_s - mean * mean, 0.0))
    tail = jnp.clip(kf / jnp.maximum(n_valid, 1).astype(F32), 1e-6, 1.0 - 1e-6)
    upper = tail < 0.5
    t_q = jnp.sqrt(-2.0 * jnp.log(jnp.where(upper, tail, 1.0 - tail)))
    z_q = t_q - ((0.010328 * t_q + 0.802853) * t_q + 2.515517) / (((0.001308 * t_q + 0.189269) * t_q + 1.432788) * t_q + 1.0)
    guess = mean + jnp.where(upper, z_q, -z_q) * std

    def next_probe(lo, hi, c_lo, c_hi, it):
        mid = 0.5 * lo + 0.5 * hi
        inside = (mid > lo) & (mid < hi)
        log_lo = jnp.log(jnp.maximum(c_lo, 1.0))
        frac = (log_lo - np.log(kf)) / (log_lo - jnp.log(jnp.maximum(c_hi, 0.5)))
        probe = lo + (hi - lo) * jnp.clip(frac, 0.02, 0.98)
        turn = jnp.zeros_like(done) + it
        probe = jnp.where(turn == 0, guess, probe)
        use_probe = (turn % 4 != 3) & (probe > lo) & (probe < hi)
        return jnp.where(use_probe, probe, mid), jnp.where(inside, 1, 0)

    def search_step(state):
        lo, hi, c_lo, c_hi, done, probe, inside, it = state
        c, = count(lambda s: s >= probe)
        active = (done == 0) & (inside > 0)
        up = active & (c >= kf)
        down = active & (c < kf)
        lo, c_lo = jnp.where(up, probe, lo), jnp.where(up, c, c_lo)
        hi, c_hi = jnp.where(down, probe, hi), jnp.where(down, c, c_hi)
        done = jnp.where((done > 0) | (inside == 0) | (c_lo == kf), 1, 0)
        probe, inside = next_probe(lo, hi, c_lo, c_hi, it + 1)
        return lo, hi, c_lo, c_hi, done, probe, inside, it + 1

    def unfinished(state):
        return jnp.sum(1 - state[4])

    probe0, inside0 = next_probe(lo, hi, c_lo, c_hi, jnp.int32(0))
    state = (lo, hi, c_lo, c_hi, done, probe0, inside0, jnp.int32(0))
    state = lax.fori_loop(0, SEARCH_WARMUP, lambda _, st: search_step(st), state)

    def search_body(carry):
        state = search_step(carry[0])
        return state, unfinished(state)

    state, _ = lax.while_loop(lambda carry: carry[1] > 0, search_body, (state, unfinished(state)))
    thr, _, c_thr, c_above = state[:4]
    thr = jnp.where(take_all, 0.5 * SCORE_MASKED, thr)
    tied = jnp.logical_not(take_all) & (c_thr > kf)
    any_tied = jnp.max(jnp.where(tied, 1, 0)) > 0

    qlt = qlt_ref[...]
    q_lat = jnp.concatenate([qlt[h * DSA_LATENT:(h + 1) * DSA_LATENT] for h in range(DSA_HEADS)], axis=1)
    wide = DSA_HEADS * BLOCK

    def attend(select):
        def qk(j, buf):
            ks = pl.multiple_of(j * KEY_STEP, KEY_STEP)
            buf[...] = _dot(c_ref[0, pl.ds(ks, KEY_STEP), :], q_lat)

        def consume(j, buf, carry):
            m, l, acc, run = carry
            ks = pl.multiple_of(j * KEY_STEP, KEY_STEP)
            sel, run = select(sc_ref[j], run)
            s_all = buf[...]
            s_all = jnp.concatenate([jnp.where(sel, s_all[:, h * BLOCK:(h + 1) * BLOCK], NEG)
                                     for h in range(DSA_HEADS)], axis=1)
            m_new = jnp.maximum(m, jnp.max(s_all, axis=0, keepdims=True))
            alpha = jnp.exp2(m - m_new)
            pr = jnp.exp2(s_all - m_new)
            l = alpha * l + jnp.sum(pr, axis=0, keepdims=True)
            acc = alpha * acc + _dot(ct_ref[0, :, pl.ds(ks, KEY_STEP)], pr.astype(MXU_DTYPE))
            return m_new, l, acc, run

        qk(0, sa_ref)

        def pair(t, carry):
            j = 2 * t
            qk(j + 1, sb_ref)
            carry = consume(j, sa_ref, carry)
            qk(jnp.minimum(j + 2, last), sa_ref)
            return consume(j + 1, sb_ref, carry)

        init = (jnp.full((1, wide), NEG, F32), jnp.zeros((1, wide), F32), jnp.zeros((DSA_LATENT, wide), F32),
                jnp.zeros((1, BLOCK), F32))
        carry = lax.fori_loop(0, nkb // 2, pair, init)
        _, l, acc, _ = lax.cond(nkb % 2 == 1, lambda c: consume(last, sa_ref, c), lambda c: c, carry)
        o_lat = (acc / jnp.where(l > 0.0, l, 1.0)).astype(MXU_DTYPE)
        y_t = jnp.zeros((GROUP_W, BLOCK), F32)
        for h in range(DSA_HEADS):
            y_t = y_t + _dot(wuvt_ref[:, h * DSA_LATENT:(h + 1) * DSA_LATENT], o_lat[:, h * BLOCK:(h + 1) * BLOCK])
        o_ref[0] = y_t.T.astype(o_ref.dtype)

    @pl.when(jnp.logical_not(any_tied))
    def _():
        attend(lambda s, run: (s >= thr, run))

    @pl.when(any_tied)
    def _():
        quota = jnp.where(tied, kf - c_above, 2.0 ** 30)
        r_i = lax.broadcasted_iota(jnp.int32, (KEY_STEP, KEY_STEP), 0)
        c_i = lax.broadcasted_iota(jnp.int32, (KEY_STEP, KEY_STEP), 1)
        tri = jnp.where(c_i <= r_i, 1.0, 0.0).astype(MXU_DTYPE)

        def select(s, run):
            eqf = jnp.where(s == thr, 1.0, 0.0)
            prefix = _dot(tri, eqf.astype(MXU_DTYPE)) + run
            within = jnp.where(prefix <= quota, eqf, 0.0)
            sel = (jnp.where(s > thr, 1.0, 0.0) + within) > 0.5
            return sel, run + jnp.sum(eqf, axis=0, keepdims=True)

        attend(select)


def _dsa(iqt, wht, qlt, ki, c, ct, wuvt, topk):
    bsz, tpk, _ = ki.shape
    tp = iqt.shape[1] // bsz
    nblk = tp // BLOCK
    qcol = lambda n: pl.BlockSpec((n, BLOCK), lambda b, i: (0, b * nblk + i))
    full = lambda a: pl.BlockSpec((1,) + a.shape[1:], lambda b, i: (b, 0, 0))
    return pl.pallas_call(
        functools.partial(_dsa_kernel, topk=topk),
        grid=(bsz, nblk),
        in_specs=[qcol(iqt.shape[0]), qcol(wht.shape[0]), qcol(qlt.shape[0]), full(ki), full(c), full(ct),
                  pl.BlockSpec(wuvt.shape, lambda b, i: (0, 0))],
        out_specs=pl.BlockSpec((1, BLOCK, GROUP_W), lambda b, i: (b, i, 0)),
        out_shape=jax.ShapeDtypeStruct((bsz, tp, GROUP_W), MXU_DTYPE),
        scratch_shapes=[pltpu.VMEM((tpk // KEY_STEP + 1, KEY_STEP, BLOCK), F32),
                        pltpu.VMEM((KEY_STEP, DSA_HEADS * BLOCK), F32), pltpu.VMEM((KEY_STEP, DSA_HEADS * BLOCK), F32),
                        pltpu.VMEM((KEY_STEP, IDX_HEADS * BLOCK), F32), pltpu.VMEM((KEY_STEP, IDX_HEADS * BLOCK), F32)],
        compiler_params=_params("parallel", "arbitrary"),
        name="dsa",
    )(iqt, wht, qlt, ki, c, ct, wuvt)


def _outproj_kernel(h_ref, yf_ref, yc_ref, yl_ref, yd_ref, w_ref, g_ref, o_ref, *, tm, tp):
    mix = _dot(yf_ref[...], w_ref[0:GROUP_W, :])
    mix = mix + _dot(yc_ref[...], w_ref[GROUP_W:2 * GROUP_W, :])
    mix = mix + _dot(yl_ref[...], w_ref[2 * GROUP_W:3 * GROUP_W, :])
    mix = mix + _dot(yd_ref[...], w_ref[3 * GROUP_W:4 * GROUP_W, :])
    row = (pl.program_id(0) * tm) % tp + lax.broadcasted_iota(jnp.int32, (tm, 1), 0)
    o_ref[...] = jnp.where(row >= PAD, h_ref[...] + _rms(mix, g_ref[...]), 0.0)


def _outproj(h, yf, yc, yl, yd, w, g, tp):
    rows, d = h.shape
    tm = _row_tile(rows)
    row = lambda i: (i, 0)
    const = lambda i: (0, 0)
    return pl.pallas_call(
        functools.partial(_outproj_kernel, tm=tm, tp=tp),
        grid=(rows // tm,),
        in_specs=[pl.BlockSpec((tm, d), row)] + [pl.BlockSpec((tm, GROUP_W), row)] * 4
                 + [pl.BlockSpec(w.shape, const), pl.BlockSpec((1, d), const)],
        out_specs=pl.BlockSpec((tm, d), row),
        out_shape=jax.ShapeDtypeStruct((rows, d), F32),
        compiler_params=_params("parallel"),
        name="out_proj",
    )(h, yf, yc, yl, yd, w, g)


def _pack_w_in(w_in):
    offs = np.cumsum((0,) + SPLIT_SIZES)
    fq, fk, fv, ff, cu, lx, lg, dq, dkv, iq, ik, iw = (w_in[..., offs[i]:offs[i + 1]] for i in range(12))
    padc = lambda a, n: jnp.pad(a, ((0, 0), (0, 0), (0, n - a.shape[-1])))
    cols = [fq * (HEAD_DIM ** -0.5 * LOG2E), fk, fv, padc(ff, 128), cu, lx, lg, dq, dkv, iq,
            jnp.tile(ik, (1, 1, IDX_HEADS)), padc(iw, 128)]
    return jnp.concatenate(cols, axis=-1).astype(MXU_DTYPE)


def _block_diag(w):
    depth, n, a, b = w.shape
    eye = jnp.eye(n, dtype=w.dtype)
    return jnp.einsum("lnab,nm->lnamb", w, eye).reshape(depth, n * a, n * b)


def _lane_pad(a, n):
    return jnp.pad(a, [(0, 0)] * (a.ndim - 1) + [(0, n - a.shape[-1])])


def kernel(x, meta_tokens, norm_g, ffn_w_in, ffn_w_out, w_in, w_out, fox_b_f, conv_dw_w, conv_dw_b, conv_ln_g,
           conv_ln_b, lru_conv_w, lru_conv_b, lru_w_a, lru_b_a, lru_w_i, lru_b_i, lru_lambda, dsa_kv_norm_g,
           dsa_w_uk, dsa_w_uv, idx_k_ln_g, idx_k_ln_b):
    bsz, seq, d = x.shape
    depth = norm_g.shape[0]
    assert seq % BLOCK == 0 and d % 128 == 0
    topk = min(TOPK_MAX, seq // 4)
    tp = PAD + N_META + seq
    rows = bsz * tp

    ffn_w_in_m = ffn_w_in.astype(MXU_DTYPE)
    ffn_w_out_m = ffn_w_out.astype(MXU_DTYPE)
    w_in_m = _pack_w_in(w_in)
    w_out_m = w_out.astype(MXU_DTYPE)
    wuk_m = (_block_diag(dsa_w_uk.transpose(0, 1, 3, 2)) * (HEAD_DIM ** -0.5 * LOG2E)).astype(MXU_DTYPE)
    wuvt_m = _block_diag(dsa_w_uv).transpose(0, 2, 1).astype(MXU_DTYPE)
    wa_m = _block_diag(lru_w_a).astype(MXU_DTYPE)
    wi_m = _block_diag(lru_w_i).astype(MXU_DTYPE)
    row2 = lambda a: a[:, None, :]
    dww = jnp.pad(conv_dw_w, ((0, 0), (0, CONV_HALO - CONV_WIDTH), (0, 0)))
    lcw = jnp.pad(lru_conv_w, ((0, 0), (0, LRU_HALO - LRU_CONV_WIDTH), (0, 0)))
    ln_g8 = row2(jnp.tile(idx_k_ln_g, (1, IDX_HEADS)))
    ln_b8 = row2(jnp.tile(idx_k_ln_b, (1, IDX_HEADS)))

    meta = jnp.broadcast_to(meta_tokens[None].astype(x.dtype), (bsz, N_META, d))
    h = jnp.concatenate([jnp.zeros((bsz, PAD, d), x.dtype), meta, x], axis=1).reshape(rows, d)

    for l in range(depth):
        g = norm_g[l][:, None, :]
        h = _ffn(h, g[0], g[1], ffn_w_in_m[l, 0], ffn_w_out_m[l, 0])
        z = _inproj(h, g[2], w_in_m[l], wuk_m[l], row2(dsa_kv_norm_g)[l], ln_g8[l], ln_b8[l])
        seq_params = {
            "fox_b_f": _lane_pad(fox_b_f[l][None], 128), "conv_dw_w": dww[l], "conv_dw_b": row2(conv_dw_b)[l],
            "conv_ln_g": row2(conv_ln_g)[l], "conv_ln_b": row2(conv_ln_b)[l], "lru_conv_w": lcw[l],
            "lru_conv_b": row2(lru_conv_b)[l], "lru_w_a": wa_m[l], "lru_b_a": row2(lru_b_a)[l],
            "lru_w_i": wi_m[l], "lru_b_i": row2(lru_b_i)[l], "lru_lambda": row2(lru_lambda)[l]}
        y_conv, y_lru, cumt, kx = _seqmix(z, bsz, tp, seq_params)
        r3 = lambda a: a.reshape(bsz, tp, a.shape[-1])
        tpk = -(-tp // KEY_STEP) * KEY_STEP
        keys = lambda a: jnp.pad(r3(a), ((0, 0), (0, tpk - tp), (0, 0)))
        keys_t = lambda a: jnp.pad(a.reshape(a.shape[0], bsz, tp).transpose(1, 0, 2), ((0, 0), (0, 0), (0, tpk - tp)))
        y_fox = _fox(z["fqT"], keys(z["fk"]), keys(kx), keys_t(z["fvT"]), cumt)
        y_dsa = _dsa(z["iqT"], z["whT"], z["qlT"], keys(z["ki"]), keys(z["c"]), keys_t(z["cT"]), wuvt_m[l], topk)
        r2 = lambda a: a.reshape(rows, a.shape[-1])
        h = _outproj(h, r2(y_fox), r2(y_conv), r2(y_lru), r2(y_dsa), w_out_m[l], g[3], tp)
        h = _ffn(h, g[4], g[5], ffn_w_in_m[l, 1], ffn_w_out_m[l, 1])

    return h.reshape(bsz, tp, d)[:, PAD + N_META:]
```

```python
import functools

import jax
import jax.numpy as jnp
import numpy as np
from jax import lax
from jax.experimental import pallas as pl
from jax.experimental.pallas import tpu as pltpu

N_META = 16
BLOCK = 128
PAD = BLOCK - N_META
KEY_STEP = 512
GROUP_W = 256
HEAD_DIM = 64
FOX_HEADS = 4
CONV_WIDTH = 31
CONV_HALO = 32
LRU_BLOCKS = 4
LRU_CONV_WIDTH = 4
LRU_HALO = 8
LRU_C = 8.0
DSA_HEADS = 4
DSA_LATENT = 128
IDX_HEADS = 8
IDX_DIM = 32
TOPK_MAX = 256
RMS_EPS = 1e-6
LN_EPS = 1e-5
SPLIT_SIZES = (GROUP_W, GROUP_W, GROUP_W, FOX_HEADS, 2 * GROUP_W, GROUP_W, GROUP_W,
               DSA_HEADS * HEAD_DIM, DSA_LATENT, IDX_HEADS * IDX_DIM, IDX_DIM, IDX_HEADS)

MXU_DTYPE = jnp.bfloat16
F32 = jnp.float32
NEG = -1e30
LOG2E = 1.4426950408889634
SCORE_MASKED = -3e38
SEARCH_WARMUP = 14
VMEM_LIMIT = 56 * 1024 * 1024

_INPROJ_GROUPS = (("fq", 256), ("fk", 256), ("fv", 256), ("ff", 128), ("cu", 512), ("lx", 256),
                  ("lg", 256), ("dq", 256), ("dkv", 128), ("iq", 256), ("ik", 256), ("iw", 128))
_INPROJ_OFF = {}
_o = 0
for _n, _w in _INPROJ_GROUPS:
    _INPROJ_OFF[_n] = (_o, _w)
    _o += _w
INPROJ_COLS = _o


def _params(*sem):
    return pltpu.CompilerParams(dimension_semantics=sem, vmem_limit_bytes=VMEM_LIMIT)


def _rms(x, g):
    return x * lax.rsqrt(jnp.mean(x * x, axis=-1, keepdims=True) + RMS_EPS) * g


def _dot(a, b):
    return jnp.dot(a, b, preferred_element_type=F32)


def _log_sigmoid(x):
    return jnp.minimum(x, 0.0) - jnp.log1p(jnp.exp(-jnp.abs(x)))


def _expm1(y):
    e = jnp.exp(y)
    regular = (e != 1.0) & (e > 0.0)
    r = (e - 1.0) * y / jnp.log(jnp.where(regular, e, 2.0))
    return jnp.where(regular, r, jnp.where(e > 0.0, y, -1.0))


def _row_tile(rows):
    return 640 if rows % 640 == 0 else BLOCK


def _ffn_kernel(h_ref, gpre_ref, gpost_ref, win_ref, wout_ref, o_ref, *, d_ff, chunk):
    x = h_ref[...]
    xn = _rms(x, gpre_ref[...]).astype(MXU_DTYPE)
    acc = jnp.zeros(x.shape, F32)
    for c in range(d_ff // chunk):
        gate = _dot(xn, win_ref[:, c * chunk:(c + 1) * chunk])
        up = _dot(xn, win_ref[:, d_ff + c * chunk:d_ff + (c + 1) * chunk])
        a = (gate * jax.nn.sigmoid(gate) * up).astype(MXU_DTYPE)
        acc = acc + _dot(a, wout_ref[c * chunk:(c + 1) * chunk, :])
    o_ref[...] = x + 0.5 * _rms(acc, gpost_ref[...])


def _ffn(h, g_pre, g_post, w_in, w_out):
    rows, d = h.shape
    d_ff = w_out.shape[0]
    tm = _row_tile(rows)
    chunk = 512 if d_ff % 512 == 0 else d_ff
    const = lambda i: (0, 0)
    return pl.pallas_call(
        functools.partial(_ffn_kernel, d_ff=d_ff, chunk=chunk),
        grid=(rows // tm,),
        in_specs=[pl.BlockSpec((tm, d), lambda i: (i, 0)),
                  pl.BlockSpec((1, d), const), pl.BlockSpec((1, d), const),
                  pl.BlockSpec(w_in.shape, const, pipeline_mode=pl.Buffered(1)),
                  pl.BlockSpec(w_out.shape, const, pipeline_mode=pl.Buffered(1))],
        out_specs=pl.BlockSpec((tm, d), lambda i: (i, 0)),
        out_shape=jax.ShapeDtypeStruct((rows, d), F32),
        compiler_params=_params("parallel"),
        name="ffn",
    )(h, g_pre, g_post, w_in, w_out)


def _inproj_kernel(h_ref, g_ref, w_ref, wuk_ref, kvg_ref, lng_ref, lnb_ref,
                   fqt_ref, fk_ref, fvt_ref, ff_ref, cu_ref, lx_ref, lg_ref,
                   qlt_ref, c_ref, ct_ref, iqt_ref, ki_ref, wht_ref):
    xn = _rms(h_ref[...], g_ref[...]).astype(MXU_DTYPE)

    def proj(name):
        lo, n = _INPROJ_OFF[name]
        return _dot(xn, w_ref[:, lo:lo + n])

    fqt_ref[...] = proj("fq").T.astype(MXU_DTYPE)
    fk_ref[...] = proj("fk").astype(MXU_DTYPE)
    fvt_ref[...] = proj("fv").T.astype(MXU_DTYPE)
    ff_ref[...] = proj("ff")
    cu_ref[...] = proj("cu")
    lx_ref[...] = proj("lx")
    lg_ref[...] = proj("lg")
    qlt_ref[...] = _dot(proj("dq").astype(MXU_DTYPE), wuk_ref[...]).T.astype(MXU_DTYPE)
    c = _rms(proj("dkv"), kvg_ref[...])
    c_ref[...] = c.astype(MXU_DTYPE)
    ct_ref[...] = c.T.astype(MXU_DTYPE)
    iqt_ref[...] = proj("iq").T.astype(MXU_DTYPE)
    ik = proj("ik")
    mu = jnp.mean(ik, axis=-1, keepdims=True)
    var = jnp.mean(jnp.square(ik - mu), axis=-1, keepdims=True)
    ki = (ik - mu) * lax.rsqrt(var + LN_EPS) * lng_ref[...] + lnb_ref[...]
    lane = lax.broadcasted_iota(jnp.int32, (1, BLOCK), 1)
    ki_ref[...] = jnp.where(lane < IDX_DIM, ki[:, :BLOCK], 0.0).astype(MXU_DTYPE)
    wht_ref[...] = (proj("iw") * (IDX_HEADS ** -0.5 * IDX_DIM ** -0.5)).T[:IDX_HEADS, :]


def _inproj(h, g, w, wuk, kvg, lng, lnb):
    rows, d = h.shape
    tm = _row_tile(rows)
    const = lambda i: (0, 0)
    row = lambda i: (i, 0)
    col = lambda i: (0, i)
    outs = (("fqT", 256, MXU_DTYPE, True), ("fk", 256, MXU_DTYPE, False), ("fvT", 256, MXU_DTYPE, True),
            ("ff", 128, F32, False), ("cu", 512, F32, False), ("lx", 256, F32, False), ("lg", 256, F32, False),
            ("qlT", 512, MXU_DTYPE, True), ("c", 128, MXU_DTYPE, False), ("cT", 128, MXU_DTYPE, True),
            ("iqT", 256, MXU_DTYPE, True), ("ki", BLOCK, MXU_DTYPE, False), ("whT", IDX_HEADS, F32, True))
    res = pl.pallas_call(
        _inproj_kernel,
        grid=(rows // tm,),
        in_specs=[pl.BlockSpec((tm, d), row), pl.BlockSpec((1, d), const),
                  pl.BlockSpec(w.shape, const, pipeline_mode=pl.Buffered(1)),
                  pl.BlockSpec(wuk.shape, const),
                  pl.BlockSpec(kvg.shape, const), pl.BlockSpec(lng.shape, const), pl.BlockSpec(lnb.shape, const)],
        out_specs=[pl.BlockSpec((n, tm), col) if t else pl.BlockSpec((tm, n), row) for _, n, _, t in outs],
        out_shape=[jax.ShapeDtypeStruct((n, rows) if t else (rows, n), dt) for _, n, dt, t in outs],
        compiler_params=_params("parallel"),
        name="in_proj",
    )(h, g, w, wuk, kvg, lng, lnb)
    return dict(zip([n for n, _, _, _ in outs], res))


def _shift_rows(x, s, fill, rows):
    return jnp.where(rows >= s, pltpu.roll(x, s, axis=0), fill)


def _seqmix_kernel(ff_ref, cu_ref, lx_ref, lg_ref, bf_ref, dww_ref, dwb_ref, lng_ref, lnb_ref,
                   lcw_ref, lcb_ref, wa_ref, ba_ref, wi_ref, bi_ref, lam_ref,
                   yc_ref, yl_ref, cumt_ref, kx_ref,
                   glu_buf, lx_buf, h_carry, cum_carry, shift_buf):
    t = pl.program_id(1)

    @pl.when(t == 0)
    def _():
        glu_buf[...] = jnp.zeros(glu_buf.shape, F32)
        lx_buf[...] = jnp.zeros(lx_buf.shape, F32)
        h_carry[...] = jnp.zeros(h_carry.shape, F32)
        cum_carry[...] = jnp.zeros(cum_carry.shape, F32)

    rows = lax.broadcasted_iota(jnp.int32, (BLOCK, 1), 0)
    valid = (t * BLOCK + rows) >= PAD

    cu = cu_ref[0]
    glu = jnp.where(valid, cu[:, :GROUP_W] * jax.nn.sigmoid(cu[:, GROUP_W:]), 0.0)
    glu_buf[CONV_HALO:, :] = glu
    acc = jnp.zeros((BLOCK, GROUP_W), F32) + dwb_ref[...]
    span = BLOCK + CONV_HALO - 8
    for r in range(1, 8):
        shift_buf[r, :span, :] = glu_buf[r:r + span, :]
    for k in range(CONV_WIDTH):
        lo = CONV_HALO - (CONV_WIDTH - 1) + k
        r, base = lo % 8, lo - lo % 8
        window = glu_buf[base:base + BLOCK, :] if r == 0 else shift_buf[r, base:base + BLOCK, :]
        acc = acc + dww_ref[k:k + 1, :] * window
    glu_buf[:CONV_HALO, :] = glu_buf[BLOCK:, :]
    mu = jnp.mean(acc, axis=-1, keepdims=True)
    var = jnp.mean(jnp.square(acc - mu), axis=-1, keepdims=True)
    hc = (acc - mu) * lax.rsqrt(var + LN_EPS) * lng_ref[...] + lnb_ref[...]
    yc_ref[0] = (hc * jax.nn.sigmoid(hc)).astype(yc_ref.dtype)

    lx_buf[LRU_HALO:, :] = jnp.where(valid, lx_ref[0], 0.0)
    xc = jnp.zeros((BLOCK, GROUP_W), F32) + lcb_ref[...]
    for k in range(LRU_CONV_WIDTH):
        lo = LRU_HALO - (LRU_CONV_WIDTH - 1) + k
        xc = xc + lcw_ref[k:k + 1, :] * lx_buf[lo:lo + BLOCK, :]
    lx_buf[:LRU_HALO, :] = lx_buf[BLOCK:, :]
    xcm = xc.astype(MXU_DTYPE)
    r = jax.nn.sigmoid(_dot(xcm, wa_ref[...]) + ba_ref[...])
    gi = jax.nn.sigmoid(_dot(xcm, wi_ref[...]) + bi_ref[...])
    log_a = LRU_C * r * _log_sigmoid(lam_ref[...])
    a = jnp.exp(log_a)
    u = jnp.where(valid, jnp.sqrt(-_expm1(2.0 * log_a)) * (gi * xc), 0.0)
    s = 1
    while s < BLOCK:
        u = a * _shift_rows(u, s, 0.0, rows) + u
        a = a * _shift_rows(a, s, 1.0, rows)
        s *= 2
    hl = u + a * h_carry[0:1, :]
    h_carry[...] = jnp.broadcast_to(hl[BLOCK - 1:BLOCK, :], h_carry.shape)
    g = lg_ref[0]
    gelu = 0.5 * g * (1.0 + jnp.tanh(np.sqrt(2.0 / np.pi).astype(np.float32) * (g + 0.044715 * g * g * g)))
    yl_ref[0] = (hl * gelu).astype(yl_ref.dtype)

    cs = _log_sigmoid(ff_ref[0] + bf_ref[...])
    s = 1
    while s < BLOCK:
        cs = cs + _shift_rows(cs, s, 0.0, rows)
        s *= 2
    cs = cs + cum_carry[0:1, :]
    cum_carry[...] = jnp.broadcast_to(cs[BLOCK - 1:BLOCK, :], cum_carry.shape)
    cs = cs * LOG2E
    cumt_ref[0] = cs.T[:8, :]
    lanes = lax.broadcasted_iota(jnp.int32, (1, BLOCK), 1)
    ck = jnp.where(lanes < FOX_HEADS, jnp.where(valid, cs, -NEG), 0.0)
    hi = ck.astype(jnp.bfloat16).astype(F32)
    mid = (ck - hi).astype(jnp.bfloat16).astype(F32)
    low = (ck - hi - mid).astype(jnp.bfloat16).astype(F32)
    ones = jnp.where((lanes >= 3 * FOX_HEADS) & (lanes < 3 * FOX_HEADS + 3), 1.0, 0.0)
    kx_ref[0] = (hi + pltpu.roll(mid, FOX_HEADS, axis=1) + pltpu.roll(low, 2 * FOX_HEADS, axis=1)
                 + ones).astype(kx_ref.dtype)


def _seqmix(z, bsz, tp, p):
    nblk = tp // BLOCK
    blk = lambda n: pl.BlockSpec((1, BLOCK, n), lambda b, t: (b, t, 0))
    const = lambda a: pl.BlockSpec(a.shape, lambda b, t: (0, 0))
    r3 = lambda a: a.reshape(bsz, tp, a.shape[-1])
    params = (p["fox_b_f"], p["conv_dw_w"], p["conv_dw_b"], p["conv_ln_g"], p["conv_ln_b"],
              p["lru_conv_w"], p["lru_conv_b"], p["lru_w_a"], p["lru_b_a"], p["lru_w_i"], p["lru_b_i"],
              p["lru_lambda"])
    return pl.pallas_call(
        _seqmix_kernel,
        grid=(bsz, nblk),
        in_specs=[blk(128), blk(512), blk(256), blk(256)] + [const(a) for a in params],
        out_specs=[blk(256), blk(256), pl.BlockSpec((1, 8, BLOCK), lambda b, t: (b, 0, t)), blk(128)],
        out_shape=[jax.ShapeDtypeStruct((bsz, tp, GROUP_W), MXU_DTYPE),
                   jax.ShapeDtypeStruct((bsz, tp, GROUP_W), MXU_DTYPE),
                   jax.ShapeDtypeStruct((bsz, 8, tp), F32),
                   jax.ShapeDtypeStruct((bsz, tp, 128), jnp.bfloat16)],
        scratch_shapes=[pltpu.VMEM((CONV_HALO + BLOCK, GROUP_W), F32),
                        pltpu.VMEM((LRU_HALO + BLOCK, GROUP_W), F32),
                        pltpu.VMEM((8, GROUP_W), F32),
                        pltpu.VMEM((8, 128), F32),
                        pltpu.VMEM((8, CONV_HALO + BLOCK, GROUP_W), F32)],
        compiler_params=_params("parallel", "arbitrary"),
        name="seq_mix",
    )(r3(z["ff"]), r3(z["cu"]), r3(z["lx"]), r3(z["lg"]), *params)


def _fox_kernel(qt_ref, cqt_ref, k_ref, kx_ref, vt_ref, o_ref, sa_ref, sb_ref):
    qi = pl.program_id(1)
    pairs = FOX_HEADS // 2
    two = 2 * BLOCK
    qt = qt_ref[...]
    cqt = cqt_ref[0]
    row = lax.broadcasted_iota(jnp.int32, (BLOCK, 1), 0)
    k_off = lax.broadcasted_iota(jnp.int32, (KEY_STEP, 1), 0)
    q_pos = qi * BLOCK + lax.broadcasted_iota(jnp.int32, (1, BLOCK), 1)
    q_pos2 = jnp.concatenate([q_pos, q_pos], axis=1)
    q_rhs = []
    for p in range(pairs):
        qp = qt[p * BLOCK:(p + 1) * BLOCK]
        halves, extras = [], []
        for c in range(2):
            h = 2 * p + c
            halves.append(jnp.where((row >= c * HEAD_DIM) & (row < (c + 1) * HEAD_DIM), qp, jnp.zeros_like(qp)))
            cq = cqt[h:h + 1]
            cq_hi = cq.astype(jnp.bfloat16).astype(F32)
            cq_mid = (cq - cq_hi).astype(jnp.bfloat16).astype(F32)
            cq_low = (cq - cq_hi - cq_mid).astype(jnp.bfloat16).astype(F32)
            is_piece = (row == h) | (row == FOX_HEADS + h) | (row == 2 * FOX_HEADS + h)
            extra = jnp.where(is_piece, -1.0, 0.0)
            for i, piece in enumerate((cq_hi, cq_mid, cq_low)):
                extra = jnp.where(row == 3 * FOX_HEADS + i, piece, extra)
            extras.append(extra)
        q_rhs.append(jnp.concatenate([jnp.concatenate(halves, axis=1),
                                      jnp.concatenate(extras, axis=1).astype(MXU_DTYPE)], axis=0))

    def qk(j, buf):
        ks = pl.multiple_of(j * KEY_STEP, KEY_STEP)
        kxb = kx_ref[0, pl.ds(ks, KEY_STEP), :]
        for p in range(pairs):
            keys = jnp.concatenate([k_ref[0, pl.ds(ks, KEY_STEP), p * BLOCK:(p + 1) * BLOCK], kxb], axis=1)
            buf[p] = _dot(keys, q_rhs[p])

    def update(j, buf, carry, causal_mask):
        ks = pl.multiple_of(j * KEY_STEP, KEY_STEP)
        out = []
        for p in range(pairs):
            m, l, acc = carry[p]
            s = buf[p]
            if causal_mask:
                s = jnp.where(ks + k_off <= q_pos2, s, NEG)
            m_new = jnp.maximum(m, jnp.max(s, axis=0, keepdims=True))
            alpha = jnp.exp2(m - m_new)
            pr = jnp.exp2(s - m_new)
            l = alpha * l + jnp.sum(pr, axis=0, keepdims=True)
            acc = alpha * acc + _dot(vt_ref[0, p * BLOCK:(p + 1) * BLOCK, pl.ds(ks, KEY_STEP)], pr.astype(MXU_DTYPE))
            out.append((m_new, l, acc))
        return tuple(out)

    init = tuple((jnp.full((1, two), NEG, F32), jnp.zeros((1, two), F32), jnp.zeros((BLOCK, two), F32))
                 for _ in range(pairs))
    n_full = (qi * BLOCK) // KEY_STEP

    qk(0, sa_ref)

    def pair_of_steps(t, carry):
        j = 2 * t
        qk(j + 1, sb_ref)
        carry = update(j, sa_ref, carry, False)
        qk(j + 2, sa_ref)
        return update(j + 1, sb_ref, carry, False)

    carry = lax.fori_loop(0, n_full // 2, pair_of_steps, init)

    def tail_two(carry):
        qk(n_full, sb_ref)
        return update(n_full, sb_ref, update(n_full - 1, sa_ref, carry, True), True)

    carry = lax.cond(n_full % 2 == 1, tail_two, lambda carry: update(n_full, sa_ref, carry, True), carry)
    outs = []
    for p in range(pairs):
        _, l, acc = carry[p]
        o_t = acc / l
        outs.append(jnp.where(row < HEAD_DIM, o_t[:, :BLOCK], o_t[:, BLOCK:]).T)
    o_ref[0] = jnp.concatenate(outs, axis=1).astype(o_ref.dtype)


def _fox(fqt, fk, kx, fvt, cumt):
    bsz, _, tp = cumt.shape
    nblk = tp // BLOCK
    full = lambda a: pl.BlockSpec((1,) + a.shape[1:], lambda b, i: (b, 0, 0))
    return pl.pallas_call(
        _fox_kernel,
        grid=(bsz, nblk),
        in_specs=[pl.BlockSpec((GROUP_W, BLOCK), lambda b, i: (0, b * nblk + i)),
                  pl.BlockSpec((1, 8, BLOCK), lambda b, i: (b, 0, i)), full(fk), full(kx), full(fvt)],
        out_specs=pl.BlockSpec((1, BLOCK, GROUP_W), lambda b, i: (b, i, 0)),
        out_shape=jax.ShapeDtypeStruct((bsz, tp, GROUP_W), MXU_DTYPE),
        scratch_shapes=[pltpu.VMEM((FOX_HEADS // 2, KEY_STEP, 2 * BLOCK), F32)] * 2,
        compiler_params=_params("parallel", "arbitrary"),
        name="fox",
    )(fqt, cumt, fk, kx, fvt)


def _dsa_kernel(iqt_ref, wht_ref, qlt_ref, ki_ref, c_ref, ct_ref, wuvt_ref, o_ref,
                sc_ref, sa_ref, sb_ref, da_ref, db_ref, *, topk):
    qi = pl.program_id(1)
    nkb = (qi * BLOCK) // KEY_STEP + 1
    k_off = lax.broadcasted_iota(jnp.int32, (KEY_STEP, 1), 0)
    q_pos = qi * BLOCK + lax.broadcasted_iota(jnp.int32, (1, BLOCK), 1)
    n_valid = q_pos - PAD + 1
    fold_rows = KEY_STEP // 8

    def fold(op, w):
        parts = [w[i * fold_rows:(i + 1) * fold_rows] for i in range(8)]
        return op(op(op(parts[0], parts[1]), op(parts[2], parts[3])), op(op(parts[4], parts[5]), op(parts[6], parts[7])))

    iqt = iqt_ref[...]
    wht = wht_ref[...]
    zeros = jnp.zeros((BLOCK - IDX_DIM, BLOCK), iqt.dtype)
    q_heads = jnp.concatenate(
        [jnp.concatenate([iqt[h * IDX_DIM:(h + 1) * IDX_DIM], zeros], axis=0) for h in range(IDX_HEADS)],
        axis=1)

    def head_dots(j, buf):
        ks = pl.multiple_of(j * KEY_STEP, KEY_STEP)
        buf[...] = _dot(ki_ref[0, pl.ds(ks, KEY_STEP), :], q_heads)

    def score_step(j, buf, stats):
        amax, s0, s1, s2 = stats
        sc = jnp.zeros((KEY_STEP, BLOCK), F32)
        for h in range(IDX_HEADS):
            sc = sc + jnp.maximum(buf[:, h * BLOCK:(h + 1) * BLOCK], 0.0) * wht[h:h + 1, :]
        k_pos = j * KEY_STEP + k_off
        valid = (k_pos <= q_pos) & (k_pos >= PAD)
        sc_ref[j] = jnp.where(valid, sc, SCORE_MASKED)
        sample = jnp.where(valid[:fold_rows], sc[:fold_rows], 0.0)
        return (jnp.maximum(amax, fold(jnp.maximum, jnp.abs(sc))),
                s0 + jnp.where(valid[:fold_rows], 1.0, 0.0), s1 + sample, s2 + sample * sample)

    last = nkb - 1
    head_dots(0, da_ref)

    def score_pair(t, stats):
        j = 2 * t
        head_dots(j + 1, db_ref)
        stats = score_step(j, da_ref, stats)
        head_dots(jnp.minimum(j + 2, last), da_ref)
        return score_step(j + 1, db_ref, stats)

    stats = lax.fori_loop(0, nkb // 2, score_pair, (jnp.zeros((fold_rows, BLOCK), F32),) * 4)
    amax, s0, s1, s2 = lax.cond(nkb % 2 == 1, lambda st: score_step(last, da_ref, st), lambda st: st, stats)
    bound = jnp.max(amax, axis=0, keepdims=True) * 1.0001 + 1e-30

    @pl.when(nkb % 2 == 1)
    def _():
        sc_ref[nkb] = jnp.full((KEY_STEP, BLOCK), SCORE_MASKED, F32)

    def count(*preds):
        def body(jj, accs):
            tiles = (sc_ref[2 * jj], sc_ref[2 * jj + 1])
            return tuple(tuple(acc + fold(jnp.add, jnp.where(pred(t), 1, 0)) for acc, t in zip(pair, tiles))
                         for pair, pred in zip(accs, preds))
        zeros = jnp.zeros((fold_rows, BLOCK), jnp.int32)
        accs = lax.fori_loop(0, (nkb + 1) // 2, body, tuple((zeros, zeros) for _ in preds))
        return [jnp.sum(a + b, axis=0, keepdims=True).astype(F32) for a, b in accs]

    kf = float(topk)
    take_all = n_valid <= topk
    c_ge0, c_gt0 = count(lambda s: s >= 0.0, lambda s: s > 0.0)
    positive = c_gt0 >= kf
    zero_tie = (c_ge0 >= kf) & jnp.logical_not(positive)
    lo = jnp.where(positive | zero_tie, 0.0, -bound)
    c_lo = jnp.where(positive | zero_tie, c_ge0, n_valid.astype(F32))
    hi = jnp.where(positive, bound, 0.0)
    c_hi = jnp.where(positive, 0.0, jnp.where(zero_tie, c_gt0, c_ge0))
    done = jnp.where(take_all | zero_tie | (c_lo == kf), 1, 0)

    n_s = jnp.maximum(jnp.sum(s0, axis=0, keepdims=True), 1.0)
    mean = jnp.sum(s1, axis=0, keepdims=True) / n_s
    std = jnp.sqrt(jnp.maximum(jnp.sum(s2, axis=0, keepdims=True) / n_s - mean * mean, 0.0))
    tail = jnp.clip(kf / jnp.maximum(n_valid, 1).astype(F32), 1e-6, 1.0 - 1e-6)
    upper = tail < 0.5
    t_q = jnp.sqrt(-2.0 * jnp.log(jnp.where(upper, tail, 1.0 - tail)))
    z_q = t_q - ((0.010328 * t_q + 0.802853) * t_q + 2.515517) / (((0.001308 * t_q + 0.189269) * t_q + 1.432788) * t_q + 1.0)
    guess = mean + jnp.where(upper, z_q, -z_q) * std

    def next_probe(lo, hi, c_lo, c_hi, it):
        mid = 0.5 * lo + 0.5 * hi
        inside = (mid > lo) & (mid < hi)
        log_lo = jnp.log(jnp.maximum(c_lo, 1.0))
        frac = (log_lo - np.log(kf)) / (log_lo - jnp.log(jnp.maximum(c_hi, 0.5)))
        probe = lo + (hi - lo) * jnp.clip(frac, 0.02, 0.98)
        turn = jnp.zeros_like(done) + it
        probe = jnp.where(turn == 0, guess, probe)
        use_probe = (turn % 4 != 3) & (probe > lo) & (probe < hi)
        return jnp.where(use_probe, probe, mid), jnp.where(inside, 1, 0)

    def search_step(state):
        lo, hi, c_lo, c_hi, done, probe, inside, it = state
        c, = count(lambda s: s >= probe)
        active = (done == 0) & (inside > 0)
        up = active & (c >= kf)
        down = active & (c < kf)
        lo, c_lo = jnp.where(up, probe, lo), jnp.where(up, c, c_lo)
        hi, c_hi = jnp.where(down, probe, hi), jnp.where(down, c, c_hi)
        done = jnp.where((done > 0) | (inside == 0) | (c_lo == kf), 1, 0)
        probe, inside = next_probe(lo, hi, c_lo, c_hi, it + 1)
        return lo, hi, c_lo, c_hi, done, probe, inside, it + 1

    def unfinished(state):
        return jnp.sum(1 - state[4])

    probe0, inside0 = next_probe(lo, hi, c_lo, c_hi, jnp.int32(0))
    state = (lo, hi, c_lo, c_hi, done, probe0, inside0, jnp.int32(0))
    state = lax.fori_loop(0, SEARCH_WARMUP, lambda _, st: search_step(st), state)

    def search_body(carry):
        state = search_step(carry[0])
        return state, unfinished(state)

    state, _ = lax.while_loop(lambda carry: carry[1] > 0, search_body, (state, unfinished(state)))
    thr, _, c_thr, c_above = state[:4]
    thr = jnp.where(take_all, 0.5 * SCORE_MASKED, thr)
    tied = jnp.logical_not(take_all) & (c_thr > kf)
    any_tied = jnp.max(jnp.where(tied, 1, 0)) > 0

    qlt = qlt_ref[...]
    q_lat = jnp.concatenate([qlt[h * DSA_LATENT:(h + 1) * DSA_LATENT] for h in range(DSA_HEADS)], axis=1)
    wide = DSA_HEADS * BLOCK

    def attend(select):
        def qk(j, buf):
            ks = pl.multiple_of(j * KEY_STEP, KEY_STEP)
            buf[...] = _dot(c_ref[0, pl.ds(ks, KEY_STEP), :], q_lat)

        def consume(j, buf, carry):
            m, l, acc, run = carry
            ks = pl.multiple_of(j * KEY_STEP, KEY_STEP)
            sel, run = select(sc_ref[j], run)
            s_all = buf[...]
            s_all = jnp.concatenate([jnp.where(sel, s_all[:, h * BLOCK:(h + 1) * BLOCK], NEG)
                                     for h in range(DSA_HEADS)], axis=1)
            m_new = jnp.maximum(m, jnp.max(s_all, axis=0, keepdims=True))
            alpha = jnp.exp2(m - m_new)
            pr = jnp.exp2(s_all - m_new)
            l = alpha * l + jnp.sum(pr, axis=0, keepdims=True)
            acc = alpha * acc + _dot(ct_ref[0, :, pl.ds(ks, KEY_STEP)], pr.astype(MXU_DTYPE))
            return m_new, l, acc, run

        qk(0, sa_ref)

        def pair(t, carry):
            j = 2 * t
            qk(j + 1, sb_ref)
            carry = consume(j, sa_ref, carry)
            qk(jnp.minimum(j + 2, last), sa_ref)
            return consume(j + 1, sb_ref, carry)

        init = (jnp.full((1, wide), NEG, F32), jnp.zeros((1, wide), F32), jnp.zeros((DSA_LATENT, wide), F32),
                jnp.zeros((1, BLOCK), F32))
        carry = lax.fori_loop(0, nkb // 2, pair, init)
        _, l, acc, _ = lax.cond(nkb % 2 == 1, lambda c: consume(last, sa_ref, c), lambda c: c, carry)
        o_lat = (acc / jnp.where(l > 0.0, l, 1.0)).astype(MXU_DTYPE)
        y_t = jnp.zeros((GROUP_W, BLOCK), F32)
        for h in range(DSA_HEADS):
            y_t = y_t + _dot(wuvt_ref[:, h * DSA_LATENT:(h + 1) * DSA_LATENT], o_lat[:, h * BLOCK:(h + 1) * BLOCK])
        o_ref[0] = y_t.T.astype(o_ref.dtype)

    @pl.when(jnp.logical_not(any_tied))
    def _():
        attend(lambda s, run: (s >= thr, run))

    @pl.when(any_tied)
    def _():
        quota = jnp.where(tied, kf - c_above, 2.0 ** 30)
        r_i = lax.broadcasted_iota(jnp.int32, (KEY_STEP, KEY_STEP), 0)
        c_i = lax.broadcasted_iota(jnp.int32, (KEY_STEP, KEY_STEP), 1)
        tri = jnp.where(c_i <= r_i, 1.0, 0.0).astype(MXU_DTYPE)

        def select(s, run):
            eqf = jnp.where(s == thr, 1.0, 0.0)
            prefix = _dot(tri, eqf.astype(MXU_DTYPE)) + run
            within = jnp.where(prefix <= quota, eqf, 0.0)
            sel = (jnp.where(s > thr, 1.0, 0.0) + within) > 0.5
            return sel, run + jnp.sum(eqf, axis=0, keepdims=True)

        attend(select)


def _dsa(iqt, wht, qlt, ki, c, ct, wuvt, topk):
    bsz, tpk, _ = ki.shape
    tp = iqt.shape[1] // bsz
    nblk = tp // BLOCK
    qcol = lambda n: pl.BlockSpec((n, BLOCK), lambda b, i: (0, b * nblk + i))
    full = lambda a: pl.BlockSpec((1,) + a.shape[1:], lambda b, i: (b, 0, 0))
    return pl.pallas_call(
        functools.partial(_dsa_kernel, topk=topk),
        grid=(bsz, nblk),
        in_specs=[qcol(iqt.shape[0]), qcol(wht.shape[0]), qcol(qlt.shape[0]), full(ki), full(c), full(ct),
                  pl.BlockSpec(wuvt.shape, lambda b, i: (0, 0))],
        out_specs=pl.BlockSpec((1, BLOCK, GROUP_W), lambda b, i: (b, i, 0)),
        out_shape=jax.ShapeDtypeStruct((bsz, tp, GROUP_W), MXU_DTYPE),
        scratch_shapes=[pltpu.VMEM((tpk // KEY_STEP + 1, KEY_STEP, BLOCK), F32),
                        pltpu.VMEM((KEY_STEP, DSA_HEADS * BLOCK), F32), pltpu.VMEM((KEY_STEP, DSA_HEADS * BLOCK), F32),
                        pltpu.VMEM((KEY_STEP, IDX_HEADS * BLOCK), F32), pltpu.VMEM((KEY_STEP, IDX_HEADS * BLOCK), F32)],
        compiler_params=_params("parallel", "arbitrary"),
        name="dsa",
    )(iqt, wht, qlt, ki, c, ct, wuvt)


def _outproj_ffn_kernel(h_ref, yf_ref, yc_ref, yl_ref, yd_ref, w_ref, g_ref, gpre_ref, gpost_ref, win_ref, wout_ref,
                        o_ref, *, tm, tp, d_ff, chunk):
    mix = _dot(yf_ref[...], w_ref[0:GROUP_W, :])
    mix = mix + _dot(yc_ref[...], w_ref[GROUP_W:2 * GROUP_W, :])
    mix = mix + _dot(yl_ref[...], w_ref[2 * GROUP_W:3 * GROUP_W, :])
    mix = mix + _dot(yd_ref[...], w_ref[3 * GROUP_W:4 * GROUP_W, :])
    row = (pl.program_id(0) * tm) % tp + lax.broadcasted_iota(jnp.int32, (tm, 1), 0)
    x = jnp.where(row >= PAD, h_ref[...] + _rms(mix, g_ref[...]), 0.0)
    xn = _rms(x, gpre_ref[...]).astype(MXU_DTYPE)
    acc = jnp.zeros(x.shape, F32)
    for c in range(d_ff // chunk):
        gate = _dot(xn, win_ref[:, c * chunk:(c + 1) * chunk])
        up = _dot(xn, win_ref[:, d_ff + c * chunk:d_ff + (c + 1) * chunk])
        a = (gate * jax.nn.sigmoid(gate) * up).astype(MXU_DTYPE)
        acc = acc + _dot(a, wout_ref[c * chunk:(c + 1) * chunk, :])
    o_ref[...] = x + 0.5 * _rms(acc, gpost_ref[...])


def _outproj_ffn(h, yf, yc, yl, yd, w, g, g_pre, g_post, w_in, w_out, tp):
    rows, d = h.shape
    d_ff = w_out.shape[0]
    tm = _row_tile(rows)
    chunk = 512 if d_ff % 512 == 0 else d_ff
    row = lambda i: (i, 0)
    const = lambda i: (0, 0)
    return pl.pallas_call(
        functools.partial(_outproj_ffn_kernel, tm=tm, tp=tp, d_ff=d_ff, chunk=chunk),
        grid=(rows // tm,),
        in_specs=[pl.BlockSpec((tm, d), row)] + [pl.BlockSpec((tm, GROUP_W), row)] * 4
                 + [pl.BlockSpec(w.shape, const, pipeline_mode=pl.Buffered(1)),
                    pl.BlockSpec((1, d), const), pl.BlockSpec((1, d), const), pl.BlockSpec((1, d), const),
                    pl.BlockSpec(w_in.shape, const, pipeline_mode=pl.Buffered(1)),
                    pl.BlockSpec(w_out.shape, const, pipeline_mode=pl.Buffered(1))],
        out_specs=pl.BlockSpec((tm, d), row),
        out_shape=jax.ShapeDtypeStruct((rows, d), F32),
        compiler_params=_params("parallel"),
        name="out_proj_ffn",
    )(h, yf, yc, yl, yd, w, g, g_pre, g_post, w_in, w_out)


def _pack_w_in(w_in):
    offs = np.cumsum((0,) + SPLIT_SIZES)
    fq, fk, fv, ff, cu, lx, lg, dq, dkv, iq, ik, iw = (w_in[..., offs[i]:offs[i + 1]] for i in range(12))
    padc = lambda a, n: jnp.pad(a, ((0, 0), (0, 0), (0, n - a.shape[-1])))
    cols = [fq * (HEAD_DIM ** -0.5 * LOG2E), fk, fv, padc(ff, 128), cu, lx, lg, dq, dkv, iq,
            jnp.tile(ik, (1, 1, IDX_HEADS)), padc(iw, 128)]
    return jnp.concatenate(cols, axis=-1).astype(MXU_DTYPE)


def _block_diag(w):
    depth, n, a, b = w.shape
    eye = jnp.eye(n, dtype=w.dtype)
    return jnp.einsum("lnab,nm->lnamb", w, eye).reshape(depth, n * a, n * b)


def _lane_pad(a, n):
    return jnp.pad(a, [(0, 0)] * (a.ndim - 1) + [(0, n - a.shape[-1])])


def kernel(x, meta_tokens, norm_g, ffn_w_in, ffn_w_out, w_in, w_out, fox_b_f, conv_dw_w, conv_dw_b, conv_ln_g,
           conv_ln_b, lru_conv_w, lru_conv_b, lru_w_a, lru_b_a, lru_w_i, lru_b_i, lru_lambda, dsa_kv_norm_g,
           dsa_w_uk, dsa_w_uv, idx_k_ln_g, idx_k_ln_b):
    bsz, seq, d = x.shape
    depth = norm_g.shape[0]
    assert seq % BLOCK == 0 and d % 128 == 0
    topk = min(TOPK_MAX, seq // 4)
    tp = PAD + N_META + seq
    rows = bsz * tp

    ffn_w_in_m = ffn_w_in.astype(MXU_DTYPE)
    ffn_w_out_m = ffn_w_out.astype(MXU_DTYPE)
    w_in_m = _pack_w_in(w_in)
    w_out_m = w_out.astype(MXU_DTYPE)
    wuk_m = (_block_diag(dsa_w_uk.transpose(0, 1, 3, 2)) * (HEAD_DIM ** -0.5 * LOG2E)).astype(MXU_DTYPE)
    wuvt_m = _block_diag(dsa_w_uv).transpose(0, 2, 1).astype(MXU_DTYPE)
    wa_m = _block_diag(lru_w_a).astype(MXU_DTYPE)
    wi_m = _block_diag(lru_w_i).astype(MXU_DTYPE)
    row2 = lambda a: a[:, None, :]
    dww = jnp.pad(conv_dw_w, ((0, 0), (0, CONV_HALO - CONV_WIDTH), (0, 0)))
    lcw = jnp.pad(lru_conv_w, ((0, 0), (0, LRU_HALO - LRU_CONV_WIDTH), (0, 0)))
    ln_g8 = row2(jnp.tile(idx_k_ln_g, (1, IDX_HEADS)))
    ln_b8 = row2(jnp.tile(idx_k_ln_b, (1, IDX_HEADS)))

    meta = jnp.broadcast_to(meta_tokens[None].astype(x.dtype), (bsz, N_META, d))
    h = jnp.concatenate([jnp.zeros((bsz, PAD, d), x.dtype), meta, x], axis=1).reshape(rows, d)

    for l in range(depth):
        g = norm_g[l][:, None, :]
        h = _ffn(h, g[0], g[1], ffn_w_in_m[l, 0], ffn_w_out_m[l, 0])
        z = _inproj(h, g[2], w_in_m[l], wuk_m[l], row2(dsa_kv_norm_g)[l], ln_g8[l], ln_b8[l])
        seq_params = {
            "fox_b_f": _lane_pad(fox_b_f[l][None], 128), "conv_dw_w": dww[l], "conv_dw_b": row2(conv_dw_b)[l],
            "conv_ln_g": row2(conv_ln_g)[l], "conv_ln_b": row2(conv_ln_b)[l], "lru_conv_w": lcw[l],
            "lru_conv_b": row2(lru_conv_b)[l], "lru_w_a": wa_m[l], "lru_b_a": row2(lru_b_a)[l],
            "lru_w_i": wi_m[l], "lru_b_i": row2(lru_b_i)[l], "lru_lambda": row2(lru_lambda)[l]}
        y_conv, y_lru, cumt, kx = _seqmix(z, bsz, tp, seq_params)
        r3 = lambda a: a.reshape(bsz, tp, a.shape[-1])
        tpk = -(-tp // KEY_STEP) * KEY_STEP
        keys = lambda a: jnp.pad(r3(a), ((0, 0), (0, tpk - tp), (0, 0)))
        keys_t = lambda a: jnp.pad(a.reshape(a.shape[0], bsz, tp).transpose(1, 0, 2), ((0, 0), (0, 0), (0, tpk - tp)))
        y_fox = _fox(z["fqT"], keys(z["fk"]), keys(kx), keys_t(z["fvT"]), cumt)
        y_dsa = _dsa(z["iqT"], z["whT"], z["qlT"], keys(z["ki"]), keys(z["c"]), keys_t(z["cT"]), wuvt_m[l], topk)
        r2 = lambda a: a.reshape(rows, a.shape[-1])
        h = _outproj_ffn(h, r2(y_fox), r2(y_conv), r2(y_lru), r2(y_dsa), w_out_m[l], g[3], g[4], g[5],
                         ffn_w_in_m[l, 1], ffn_w_out_m[l, 1], tp)

    return h.reshape(bsz, tp, d)[:, PAD + N_META:]
```

```python
import functools

import jax
import jax.numpy as jnp
import numpy as np
from jax import lax
from jax.experimental import pallas as pl
from jax.experimental.pallas import tpu as pltpu

N_META = 16
BLOCK = 128
PAD = BLOCK - N_META
KEY_STEP = 512
GROUP_W = 256
HEAD_DIM = 64
FOX_HEADS = 4
CONV_WIDTH = 31
CONV_HALO = 32
LRU_BLOCKS = 4
LRU_CONV_WIDTH = 4
LRU_HALO = 8
LRU_C = 8.0
DSA_HEADS = 4
DSA_LATENT = 128
IDX_HEADS = 8
IDX_DIM = 32
TOPK_MAX = 256
RMS_EPS = 1e-6
LN_EPS = 1e-5
SPLIT_SIZES = (GROUP_W, GROUP_W, GROUP_W, FOX_HEADS, 2 * GROUP_W, GROUP_W, GROUP_W,
               DSA_HEADS * HEAD_DIM, DSA_LATENT, IDX_HEADS * IDX_DIM, IDX_DIM, IDX_HEADS)

MXU_DTYPE = jnp.bfloat16
F32 = jnp.float32
NEG = -1e30
LOG2E = 1.4426950408889634
SCORE_MASKED = -3e38
SEARCH_WARMUP = 14
VMEM_LIMIT = 56 * 1024 * 1024

_INPROJ_GROUPS = (("fq", 256), ("fk", 256), ("fv", 256), ("ff", 128), ("cu", 512), ("lx", 256),
                  ("lg", 256), ("dq", 256), ("dkv", 128), ("iq", 256), ("ik", 256), ("iw", 128))
_INPROJ_OFF = {}
_o = 0
for _n, _w in _INPROJ_GROUPS:
    _INPROJ_OFF[_n] = (_o, _w)
    _o += _w
INPROJ_COLS = _o


def _params(*sem):
    return pltpu.CompilerParams(dimension_semantics=sem, vmem_limit_bytes=VMEM_LIMIT)


def _rms(x, g):
    return x * lax.rsqrt(jnp.mean(x * x, axis=-1, keepdims=True) + RMS_EPS) * g


def _dot(a, b):
    return jnp.dot(a, b, preferred_element_type=F32)


def _log_sigmoid(x):
    return jnp.minimum(x, 0.0) - jnp.log1p(jnp.exp(-jnp.abs(x)))


def _expm1(y):
    e = jnp.exp(y)
    regular = (e != 1.0) & (e > 0.0)
    r = (e - 1.0) * y / jnp.log(jnp.where(regular, e, 2.0))
    return jnp.where(regular, r, jnp.where(e > 0.0, y, -1.0))


def _row_tile(rows):
    return 640 if rows % 640 == 0 else BLOCK


def _ffn_kernel(h_ref, gpre_ref, gpost_ref, win_ref, wout_ref, o_ref, *, d_ff, chunk):
    x = h_ref[...]
    xn = _rms(x, gpre_ref[...]).astype(MXU_DTYPE)
    acc = jnp.zeros(x.shape, F32)
    for c in range(d_ff // chunk):
        gate = _dot(xn, win_ref[:, c * chunk:(c + 1) * chunk])
        up = _dot(xn, win_ref[:, d_ff + c * chunk:d_ff + (c + 1) * chunk])
        a = (gate * jax.nn.sigmoid(gate) * up).astype(MXU_DTYPE)
        acc = acc + _dot(a, wout_ref[c * chunk:(c + 1) * chunk, :])
    o_ref[...] = x + 0.5 * _rms(acc, gpost_ref[...])


def _ffn_inproj_kernel(h_ref, gpre_ref, gpost_ref, win_ref, wout_ref, g_ref, w_ref, wuk_ref, kvg_ref, lng_ref, lnb_ref,
                       ho_ref, fqt_ref, fk_ref, fvt_ref, ff_ref, cu_ref, lx_ref, lg_ref,
                       qlt_ref, c_ref, ct_ref, iqt_ref, ki_ref, wht_ref, *, d_ff, chunk):
    _ffn_kernel(h_ref, gpre_ref, gpost_ref, win_ref, wout_ref, ho_ref, d_ff=d_ff, chunk=chunk)
    xn = _rms(ho_ref[...], g_ref[...]).astype(MXU_DTYPE)

    def proj(name):
        lo, n = _INPROJ_OFF[name]
        return _dot(xn, w_ref[:, lo:lo + n])

    fqt_ref[...] = proj("fq").T.astype(MXU_DTYPE)
    fk_ref[...] = proj("fk").astype(MXU_DTYPE)
    fvt_ref[...] = proj("fv").T.astype(MXU_DTYPE)
    ff_ref[...] = proj("ff")
    cu_ref[...] = proj("cu")
    lx_ref[...] = proj("lx")
    lg_ref[...] = proj("lg")
    qlt_ref[...] = _dot(proj("dq").astype(MXU_DTYPE), wuk_ref[...]).T.astype(MXU_DTYPE)
    c = _rms(proj("dkv"), kvg_ref[...])
    c_ref[...] = c.astype(MXU_DTYPE)
    ct_ref[...] = c.T.astype(MXU_DTYPE)
    iqt_ref[...] = proj("iq").T.astype(MXU_DTYPE)
    ik = proj("ik")
    mu = jnp.mean(ik, axis=-1, keepdims=True)
    var = jnp.mean(jnp.square(ik - mu), axis=-1, keepdims=True)
    ki = (ik - mu) * lax.rsqrt(var + LN_EPS) * lng_ref[...] + lnb_ref[...]
    lane = lax.broadcasted_iota(jnp.int32, (1, BLOCK), 1)
    ki_ref[...] = jnp.where(lane < IDX_DIM, ki[:, :BLOCK], 0.0).astype(MXU_DTYPE)
    wht_ref[...] = (proj("iw") * (IDX_HEADS ** -0.5 * IDX_DIM ** -0.5)).T[:IDX_HEADS, :]


def _ffn_inproj(h, g_pre, g_post, w_in, w_out, g, w, wuk, kvg, lng, lnb):
    rows, d = h.shape
    d_ff = w_out.shape[0]
    chunk = 512 if d_ff % 512 == 0 else d_ff
    tm = _row_tile(rows)
    const = lambda i: (0, 0)
    row = lambda i: (i, 0)
    col = lambda i: (0, i)
    outs = (("fqT", 256, MXU_DTYPE, True), ("fk", 256, MXU_DTYPE, False), ("fvT", 256, MXU_DTYPE, True),
            ("ff", 128, F32, False), ("cu", 512, F32, False), ("lx", 256, F32, False), ("lg", 256, F32, False),
            ("qlT", 512, MXU_DTYPE, True), ("c", 128, MXU_DTYPE, False), ("cT", 128, MXU_DTYPE, True),
            ("iqT", 256, MXU_DTYPE, True), ("ki", BLOCK, MXU_DTYPE, False), ("whT", IDX_HEADS, F32, True))
    res = pl.pallas_call(
        functools.partial(_ffn_inproj_kernel, d_ff=d_ff, chunk=chunk),
        grid=(rows // tm,),
        in_specs=[pl.BlockSpec((tm, d), row), pl.BlockSpec((1, d), const), pl.BlockSpec((1, d), const),
                  pl.BlockSpec(w_in.shape, const, pipeline_mode=pl.Buffered(1)),
                  pl.BlockSpec(w_out.shape, const, pipeline_mode=pl.Buffered(1)),
                  pl.BlockSpec((1, d), const),
                  pl.BlockSpec(w.shape, const, pipeline_mode=pl.Buffered(1)),
                  pl.BlockSpec(wuk.shape, const),
                  pl.BlockSpec(kvg.shape, const), pl.BlockSpec(lng.shape, const), pl.BlockSpec(lnb.shape, const)],
        out_specs=[pl.BlockSpec((tm, d), row)]
                  + [pl.BlockSpec((n, tm), col) if t else pl.BlockSpec((tm, n), row) for _, n, _, t in outs],
        out_shape=[jax.ShapeDtypeStruct((rows, d), F32)]
                  + [jax.ShapeDtypeStruct((n, rows) if t else (rows, n), dt) for _, n, dt, t in outs],
        compiler_params=_params("parallel"),
        name="ffn_in_proj",
    )(h, g_pre, g_post, w_in, w_out, g, w, wuk, kvg, lng, lnb)
    return res[0], dict(zip([n for n, _, _, _ in outs], res[1:]))


def _shift_rows(x, s, fill, rows):
    return jnp.where(rows >= s, pltpu.roll(x, s, axis=0), fill)


def _seqmix_kernel(ff_ref, cu_ref, lx_ref, lg_ref, bf_ref, dww_ref, dwb_ref, lng_ref, lnb_ref,
                   lcw_ref, lcb_ref, wa_ref, ba_ref, wi_ref, bi_ref, lam_ref,
                   yc_ref, yl_ref, cumt_ref, kx_ref,
                   glu_buf, lx_buf, h_carry, cum_carry, shift_buf):
    t = pl.program_id(1)

    @pl.when(t == 0)
    def _():
        glu_buf[...] = jnp.zeros(glu_buf.shape, F32)
        lx_buf[...] = jnp.zeros(lx_buf.shape, F32)
        h_carry[...] = jnp.zeros(h_carry.shape, F32)
        cum_carry[...] = jnp.zeros(cum_carry.shape, F32)

    rows = lax.broadcasted_iota(jnp.int32, (BLOCK, 1), 0)
    valid = (t * BLOCK + rows) >= PAD

    cu = cu_ref[0]
    glu = jnp.where(valid, cu[:, :GROUP_W] * jax.nn.sigmoid(cu[:, GROUP_W:]), 0.0)
    glu_buf[CONV_HALO:, :] = glu
    acc = jnp.zeros((BLOCK, GROUP_W), F32) + dwb_ref[...]
    span = BLOCK + CONV_HALO - 8
    for r in range(1, 8):
        shift_buf[r, :span, :] = glu_buf[r:r + span, :]
    for k in range(CONV_WIDTH):
        lo = CONV_HALO - (CONV_WIDTH - 1) + k
        r, base = lo % 8, lo - lo % 8
        window = glu_buf[base:base + BLOCK, :] if r == 0 else shift_buf[r, base:base + BLOCK, :]
        acc = acc + dww_ref[k:k + 1, :] * window
    glu_buf[:CONV_HALO, :] = glu_buf[BLOCK:, :]
    mu = jnp.mean(acc, axis=-1, keepdims=True)
    var = jnp.mean(jnp.square(acc - mu), axis=-1, keepdims=True)
    hc = (acc - mu) * lax.rsqrt(var + LN_EPS) * lng_ref[...] + lnb_ref[...]
    yc_ref[0] = (hc * jax.nn.sigmoid(hc)).astype(yc_ref.dtype)

    lx_buf[LRU_HALO:, :] = jnp.where(valid, lx_ref[0], 0.0)
    xc = jnp.zeros((BLOCK, GROUP_W), F32) + lcb_ref[...]
    for k in range(LRU_CONV_WIDTH):
        lo = LRU_HALO - (LRU_CONV_WIDTH - 1) + k
        xc = xc + lcw_ref[k:k + 1, :] * lx_buf[lo:lo + BLOCK, :]
    lx_buf[:LRU_HALO, :] = lx_buf[BLOCK:, :]
    xcm = xc.astype(MXU_DTYPE)
    r = jax.nn.sigmoid(_dot(xcm, wa_ref[...]) + ba_ref[...])
    gi = jax.nn.sigmoid(_dot(xcm, wi_ref[...]) + bi_ref[...])
    log_a = LRU_C * r * _log_sigmoid(lam_ref[...])
    a = jnp.exp(log_a)
    u = jnp.where(valid, jnp.sqrt(-_expm1(2.0 * log_a)) * (gi * xc), 0.0)
    s = 1
    while s < BLOCK:
        u = a * _shift_rows(u, s, 0.0, rows) + u
        a = a * _shift_rows(a, s, 1.0, rows)
        s *= 2
    hl = u + a * h_carry[0:1, :]
    h_carry[...] = jnp.broadcast_to(hl[BLOCK - 1:BLOCK, :], h_carry.shape)
    g = lg_ref[0]
    gelu = 0.5 * g * (1.0 + jnp.tanh(np.sqrt(2.0 / np.pi).astype(np.float32) * (g + 0.044715 * g * g * g)))
    yl_ref[0] = (hl * gelu).astype(yl_ref.dtype)

    cs = _log_sigmoid(ff_ref[0] + bf_ref[...])
    s = 1
    while s < BLOCK:
        cs = cs + _shift_rows(cs, s, 0.0, rows)
        s *= 2
    cs = cs + cum_carry[0:1, :]
    cum_carry[...] = jnp.broadcast_to(cs[BLOCK - 1:BLOCK, :], cum_carry.shape)
    cs = cs * LOG2E
    cumt_ref[0] = cs.T[:8, :]
    lanes = lax.broadcasted_iota(jnp.int32, (1, BLOCK), 1)
    ck = jnp.where(lanes < FOX_HEADS, jnp.where(valid, cs, -NEG), 0.0)
    hi = ck.astype(jnp.bfloat16).astype(F32)
    mid = (ck - hi).astype(jnp.bfloat16).astype(F32)
    low = (ck - hi - mid).astype(jnp.bfloat16).astype(F32)
    ones = jnp.where((lanes >= 3 * FOX_HEADS) & (lanes < 3 * FOX_HEADS + 3), 1.0, 0.0)
    kx_ref[0] = (hi + pltpu.roll(mid, FOX_HEADS, axis=1) + pltpu.roll(low, 2 * FOX_HEADS, axis=1)
                 + ones).astype(kx_ref.dtype)


def _seqmix(z, bsz, tp, p):
    nblk = tp // BLOCK
    blk = lambda n: pl.BlockSpec((1, BLOCK, n), lambda b, t: (b, t, 0))
    const = lambda a: pl.BlockSpec(a.shape, lambda b, t: (0, 0))
    r3 = lambda a: a.reshape(bsz, tp, a.shape[-1])
    params = (p["fox_b_f"], p["conv_dw_w"], p["conv_dw_b"], p["conv_ln_g"], p["conv_ln_b"],
              p["lru_conv_w"], p["lru_conv_b"], p["lru_w_a"], p["lru_b_a"], p["lru_w_i"], p["lru_b_i"],
              p["lru_lambda"])
    return pl.pallas_call(
        _seqmix_kernel,
        grid=(bsz, nblk),
        in_specs=[blk(128), blk(512), blk(256), blk(256)] + [const(a) for a in params],
        out_specs=[blk(256), blk(256), pl.BlockSpec((1, 8, BLOCK), lambda b, t: (b, 0, t)), blk(128)],
        out_shape=[jax.ShapeDtypeStruct((bsz, tp, GROUP_W), MXU_DTYPE),
                   jax.ShapeDtypeStruct((bsz, tp, GROUP_W), MXU_DTYPE),
                   jax.ShapeDtypeStruct((bsz, 8, tp), F32),
                   jax.ShapeDtypeStruct((bsz, tp, 128), jnp.bfloat16)],
        scratch_shapes=[pltpu.VMEM((CONV_HALO + BLOCK, GROUP_W), F32),
                        pltpu.VMEM((LRU_HALO + BLOCK, GROUP_W), F32),
                        pltpu.VMEM((8, GROUP_W), F32),
                        pltpu.VMEM((8, 128), F32),
                        pltpu.VMEM((8, CONV_HALO + BLOCK, GROUP_W), F32)],
        compiler_params=_params("parallel", "arbitrary"),
        name="seq_mix",
    )(r3(z["ff"]), r3(z["cu"]), r3(z["lx"]), r3(z["lg"]), *params)


def _fox_kernel(qt_ref, cqt_ref, k_ref, kx_ref, vt_ref, o_ref, sa_ref, sb_ref):
    qi = pl.program_id(1)
    pairs = FOX_HEADS // 2
    two = 2 * BLOCK
    qt = qt_ref[...]
    cqt = cqt_ref[0]
    row = lax.broadcasted_iota(jnp.int32, (BLOCK, 1), 0)
    k_off = lax.broadcasted_iota(jnp.int32, (KEY_STEP, 1), 0)
    q_pos = qi * BLOCK + lax.broadcasted_iota(jnp.int32, (1, BLOCK), 1)
    q_pos2 = jnp.concatenate([q_pos, q_pos], axis=1)
    q_rhs = []
    for p in range(pairs):
        qp = qt[p * BLOCK:(p + 1) * BLOCK]
        halves, extras = [], []
        for c in range(2):
            h = 2 * p + c
            halves.append(jnp.where((row >= c * HEAD_DIM) & (row < (c + 1) * HEAD_DIM), qp, jnp.zeros_like(qp)))
            cq = cqt[h:h + 1]
            cq_hi = cq.astype(jnp.bfloat16).astype(F32)
            cq_mid = (cq - cq_hi).astype(jnp.bfloat16).astype(F32)
            cq_low = (cq - cq_hi - cq_mid).astype(jnp.bfloat16).astype(F32)
            is_piece = (row == h) | (row == FOX_HEADS + h) | (row == 2 * FOX_HEADS + h)
            extra = jnp.where(is_piece, -1.0, 0.0)
            for i, piece in enumerate((cq_hi, cq_mid, cq_low)):
                extra = jnp.where(row == 3 * FOX_HEADS + i, piece, extra)
            extras.append(extra)
        q_rhs.append(jnp.concatenate([jnp.concatenate(halves, axis=1),
                                      jnp.concatenate(extras, axis=1).astype(MXU_DTYPE)], axis=0))

    def qk(j, buf):
        ks = pl.multiple_of(j * KEY_STEP, KEY_STEP)
        kxb = kx_ref[0, pl.ds(ks, KEY_STEP), :]
        for p in range(pairs):
            keys = jnp.concatenate([k_ref[0, pl.ds(ks, KEY_STEP), p * BLOCK:(p + 1) * BLOCK], kxb], axis=1)
            buf[p] = _dot(keys, q_rhs[p])

    def update(j, buf, carry, causal_mask):
        ks = pl.multiple_of(j * KEY_STEP, KEY_STEP)
        out = []
        for p in range(pairs):
            m, l, acc = carry[p]
            s = buf[p]
            if causal_mask:
                s = jnp.where(ks + k_off <= q_pos2, s, NEG)
            m_new = jnp.maximum(m, jnp.max(s, axis=0, keepdims=True))
            alpha = jnp.exp2(m - m_new)
            pr = jnp.exp2(s - m_new)
            l = alpha * l + jnp.sum(pr, axis=0, keepdims=True)
            acc = alpha * acc + _dot(vt_ref[0, p * BLOCK:(p + 1) * BLOCK, pl.ds(ks, KEY_STEP)], pr.astype(MXU_DTYPE))
            out.append((m_new, l, acc))
        return tuple(out)

    init = tuple((jnp.full((1, two), NEG, F32), jnp.zeros((1, two), F32), jnp.zeros((BLOCK, two), F32))
                 for _ in range(pairs))
    n_full = (qi * BLOCK) // KEY_STEP

    qk(0, sa_ref)

    def pair_of_steps(t, carry):
        j = 2 * t
        qk(j + 1, sb_ref)
        carry = update(j, sa_ref, carry, False)
        qk(j + 2, sa_ref)
        return update(j + 1, sb_ref, carry, False)

    carry = lax.fori_loop(0, n_full // 2, pair_of_steps, init)

    def tail_two(carry):
        qk(n_full, sb_ref)
        return update(n_full, sb_ref, update(n_full - 1, sa_ref, carry, True), True)

    carry = lax.cond(n_full % 2 == 1, tail_two, lambda carry: update(n_full, sa_ref, carry, True), carry)
    outs = []
    for p in range(pairs):
        _, l, acc = carry[p]
        o_t = acc / l
        outs.append(jnp.where(row < HEAD_DIM, o_t[:, :BLOCK], o_t[:, BLOCK:]).T)
    o_ref[0] = jnp.concatenate(outs, axis=1).astype(o_ref.dtype)


def _fox(fqt, fk, kx, fvt, cumt):
    bsz, _, tp = cumt.shape
    nblk = tp // BLOCK
    full = lambda a: pl.BlockSpec((1,) + a.shape[1:], lambda b, i: (b, 0, 0))
    return pl.pallas_call(
        _fox_kernel,
        grid=(bsz, nblk),
        in_specs=[pl.BlockSpec((GROUP_W, BLOCK), lambda b, i: (0, b * nblk + i)),
                  pl.BlockSpec((1, 8, BLOCK), lambda b, i: (b, 0, i)), full(fk), full(kx), full(fvt)],
        out_specs=pl.BlockSpec((1, BLOCK, GROUP_W), lambda b, i: (b, i, 0)),
        out_shape=jax.ShapeDtypeStruct((bsz, tp, GROUP_W), MXU_DTYPE),
        scratch_shapes=[pltpu.VMEM((FOX_HEADS // 2, KEY_STEP, 2 * BLOCK), F32)] * 2,
        compiler_params=_params("parallel", "arbitrary"),
        name="fox",
    )(fqt, cumt, fk, kx, fvt)


def _dsa_kernel(iqt_ref, wht_ref, qlt_ref, ki_ref, c_ref, ct_ref, wuvt_ref, o_ref,
                sc_ref, sa_ref, sb_ref, da_ref, db_ref, *, topk):
    qi = pl.program_id(1)
    nkb = (qi * BLOCK) // KEY_STEP + 1
    k_off = lax.broadcasted_iota(jnp.int32, (KEY_STEP, 1), 0)
    q_pos = qi * BLOCK + lax.broadcasted_iota(jnp.int32, (1, BLOCK), 1)
    n_valid = q_pos - PAD + 1
    fold_rows = KEY_STEP // 8

    def fold(op, w):
        parts = [w[i * fold_rows:(i + 1) * fold_rows] for i in range(8)]
        return op(op(op(parts[0], parts[1]), op(parts[2], parts[3])), op(op(parts[4], parts[5]), op(parts[6], parts[7])))

    iqt = iqt_ref[...]
    wht = wht_ref[...]
    zeros = jnp.zeros((BLOCK - IDX_DIM, BLOCK), iqt.dtype)
    q_heads = jnp.concatenate(
        [jnp.concatenate([iqt[h * IDX_DIM:(h + 1) * IDX_DIM], zeros], axis=0) for h in range(IDX_HEADS)],
        axis=1)

    def head_dots(j, buf):
        ks = pl.multiple_of(j * KEY_STEP, KEY_STEP)
        buf[...] = _dot(ki_ref[0, pl.ds(ks, KEY_STEP), :], q_heads)

    def score_step(j, buf, stats):
        amax, s0, s1, s2 = stats
        sc = jnp.zeros((KEY_STEP, BLOCK), F32)
        for h in range(IDX_HEADS):
            sc = sc + jnp.maximum(buf[:, h * BLOCK:(h + 1) * BLOCK], 0.0) * wht[h:h + 1, :]
        k_pos = j * KEY_STEP + k_off
        valid = (k_pos <= q_pos) & (k_pos >= PAD)
        sc_ref[j] = jnp.where(valid, sc, SCORE_MASKED)
        sample = jnp.where(valid[:fold_rows], sc[:fold_rows], 0.0)
        return (jnp.maximum(amax, fold(jnp.maximum, jnp.abs(sc))),
                s0 + jnp.where(valid[:fold_rows], 1.0, 0.0), s1 + sample, s2 + sample * sample)

    last = nkb - 1
    head_dots(0, da_ref)

    def score_pair(t, stats):
        j = 2 * t
        head_dots(j + 1, db_ref)
        stats = score_step(j, da_ref, stats)
        head_dots(jnp.minimum(j + 2, last), da_ref)
        return score_step(j + 1, db_ref, stats)

    stats = lax.fori_loop(0, nkb // 2, score_pair, (jnp.zeros((fold_rows, BLOCK), F32),) * 4)
    amax, s0, s1, s2 = lax.cond(nkb % 2 == 1, lambda st: score_step(last, da_ref, st), lambda st: st, stats)
    bound = jnp.max(amax, axis=0, keepdims=True) * 1.0001 + 1e-30

    @pl.when(nkb % 2 == 1)
    def _():
        sc_ref[nkb] = jnp.full((KEY_STEP, BLOCK), SCORE_MASKED, F32)

    def count(*preds):
        def body(jj, accs):
            tiles = (sc_ref[2 * jj], sc_ref[2 * jj + 1])
            return tuple(tuple(acc + fold(jnp.add, jnp.where(pred(t), 1, 0)) for acc, t in zip(pair, tiles))
                         for pair, pred in zip(accs, preds))
        zeros = jnp.zeros((fold_rows, BLOCK), jnp.int32)
        accs = lax.fori_loop(0, (nkb + 1) // 2, body, tuple((zeros, zeros) for _ in preds))
        return [jnp.sum(a + b, axis=0, keepdims=True).astype(F32) for a, b in accs]

    kf = float(topk)
    take_all = n_valid <= topk
    c_ge0, c_gt0 = count(lambda s: s >= 0.0, lambda s: s > 0.0)
    positive = c_gt0 >= kf
    zero_tie = (c_ge0 >= kf) & jnp.logical_not(positive)
    lo = jnp.where(positive | zero_tie, 0.0, -bound)
    c_lo = jnp.where(positive | zero_tie, c_ge0, n_valid.astype(F32))
    hi = jnp.where(positive, bound, 0.0)
    c_hi = jnp.where(positive, 0.0, jnp.where(zero_tie, c_gt0, c_ge0))
    done = jnp.where(take_all | zero_tie | (c_lo == kf), 1, 0)

    n_s = jnp.maximum(jnp.sum(s0, axis=0, keepdims=True), 1.0)
    mean = jnp.sum(s1, axis=0, keepdims=True) / n_s
    std = jnp.sqrt(jnp.maximum(jnp.sum(s2, axis=0, keepdims=True) / n_s - mean * mean, 0.0))
    tail = jnp.clip(kf / jnp.maximum(n_valid, 1).astype(F32), 1e-6, 1.0 - 1e-6)
    upper = tail < 0.5
    t_q = jnp.sqrt(-2.0 * jnp.log(jnp.where(upper, tail, 1.0 - tail)))
    z_q = t_q - ((0.010328 * t_q + 0.802853) * t_q + 2.515517) / (((0.001308 * t_q + 0.189269) * t_q + 1.432788) * t_q + 1.0)
    guess = mean + jnp.where(upper, z_q, -z_q) * std

    def next_probe(lo, hi, c_lo, c_hi, it):
        mid = 0.5 * lo + 0.5 * hi
        inside = (mid > lo) & (mid < hi)
        log_lo = jnp.log(jnp.maximum(c_lo, 1.0))
        frac = (log_lo - np.log(kf)) / (log_lo - jnp.log(jnp.maximum(c_hi, 0.5)))
        probe = lo + (hi - lo) * jnp.clip(frac, 0.02, 0.98)
        turn = jnp.zeros_like(done) + it
        probe = jnp.where(turn == 0, guess, probe)
        use_probe = (turn % 4 != 3) & (probe > lo) & (probe < hi)
        return jnp.where(use_probe, probe, mid), jnp.where(inside, 1, 0)

    def search_step(state):
        lo, hi, c_lo, c_hi, done, probe, inside, it = state
        c, = count(lambda s: s >= probe)
        active = (done == 0) & (inside > 0)
        up = active & (c >= kf)
        down = active & (c < kf)
        lo, c_lo = jnp.where(up, probe, lo), jnp.where(up, c, c_lo)
        hi, c_hi = jnp.where(down, probe, hi), jnp.where(down, c, c_hi)
        done = jnp.where((done > 0) | (inside == 0) | (c_lo == kf), 1, 0)
        probe, inside = next_probe(lo, hi, c_lo, c_hi, it + 1)
        return lo, hi, c_lo, c_hi, done, probe, inside, it + 1

    def unfinished(state):
        return jnp.sum(1 - state[4])

    probe0, inside0 = next_probe(lo, hi, c_lo, c_hi, jnp.int32(0))
    state = (lo, hi, c_lo, c_hi, done, probe0, inside0, jnp.int32(0))
    state = lax.fori_loop(0, SEARCH_WARMUP, lambda _, st: search_step(st), state)

    def search_body(carry):
        state = search_step(carry[0])
        return state, unfinished(state)

    state, _ = lax.while_loop(lambda carry: carry[1] > 0, search_body, (state, unfinished(state)))
    thr, _, c_thr, c_above = state[:4]
    thr = jnp.where(take_all, 0.5 * SCORE_MASKED, thr)
    tied = jnp.logical_not(take_all) & (c_thr > kf)
    any_tied = jnp.max(jnp.where(tied, 1, 0)) > 0

    qlt = qlt_ref[...]
    q_lat = jnp.concatenate([qlt[h * DSA_LATENT:(h + 1) * DSA_LATENT] for h in range(DSA_HEADS)], axis=1)
    wide = DSA_HEADS * BLOCK

    def attend(select):
        def qk(j, buf):
            ks = pl.multiple_of(j * KEY_STEP, KEY_STEP)
            buf[...] = _dot(c_ref[0, pl.ds(ks, KEY_STEP), :], q_lat)

        def consume(j, buf, carry):
            m, l, acc, run = carry
            ks = pl.multiple_of(j * KEY_STEP, KEY_STEP)
            sel, run = select(sc_ref[j], run)
            s_all = buf[...]
            s_all = jnp.concatenate([jnp.where(sel, s_all[:, h * BLOCK:(h + 1) * BLOCK], NEG)
                                     for h in range(DSA_HEADS)], axis=1)
            m_new = jnp.maximum(m, jnp.max(s_all, axis=0, keepdims=True))
            alpha = jnp.exp2(m - m_new)
            pr = jnp.exp2(s_all - m_new)
            l = alpha * l + jnp.sum(pr, axis=0, keepdims=True)
            acc = alpha * acc + _dot(ct_ref[0, :, pl.ds(ks, KEY_STEP)], pr.astype(MXU_DTYPE))
            return m_new, l, acc, run

        qk(0, sa_ref)

        def pair(t, carry):
            j = 2 * t
            qk(j + 1, sb_ref)
            carry = consume(j, sa_ref, carry)
            qk(jnp.minimum(j + 2, last), sa_ref)
            return consume(j + 1, sb_ref, carry)

        init = (jnp.full((1, wide), NEG, F32), jnp.zeros((1, wide), F32), jnp.zeros((DSA_LATENT, wide), F32),
                jnp.zeros((1, BLOCK), F32))
        carry = lax.fori_loop(0, nkb // 2, pair, init)
        _, l, acc, _ = lax.cond(nkb % 2 == 1, lambda c: consume(last, sa_ref, c), lambda c: c, carry)
        o_lat = (acc / jnp.where(l > 0.0, l, 1.0)).astype(MXU_DTYPE)
        y_t = jnp.zeros((GROUP_W, BLOCK), F32)
        for h in range(DSA_HEADS):
            y_t = y_t + _dot(wuvt_ref[:, h * DSA_LATENT:(h + 1) * DSA_LATENT], o_lat[:, h * BLOCK:(h + 1) * BLOCK])
        o_ref[0] = y_t.T.astype(o_ref.dtype)

    @pl.when(jnp.logical_not(any_tied))
    def _():
        attend(lambda s, run: (s >= thr, run))

    @pl.when(any_tied)
    def _():
        quota = jnp.where(tied, kf - c_above, 2.0 ** 30)
        r_i = lax.broadcasted_iota(jnp.int32, (KEY_STEP, KEY_STEP), 0)
        c_i = lax.broadcasted_iota(jnp.int32, (KEY_STEP, KEY_STEP), 1)
        tri = jnp.where(c_i <= r_i, 1.0, 0.0).astype(MXU_DTYPE)

        def select(s, run):
            eqf = jnp.where(s == thr, 1.0, 0.0)
            prefix = _dot(tri, eqf.astype(MXU_DTYPE)) + run
            within = jnp.where(prefix <= quota, eqf, 0.0)
            sel = (jnp.where(s > thr, 1.0, 0.0) + within) > 0.5
            return sel, run + jnp.sum(eqf, axis=0, keepdims=True)

        attend(select)


def _dsa(iqt, wht, qlt, ki, c, ct, wuvt, topk):
    bsz, tpk, _ = ki.shape
    tp = iqt.shape[1] // bsz
    nblk = tp // BLOCK
    qcol = lambda n: pl.BlockSpec((n, BLOCK), lambda b, i: (0, b * nblk + i))
    full = lambda a: pl.BlockSpec((1,) + a.shape[1:], lambda b, i: (b, 0, 0))
    return pl.pallas_call(
        functools.partial(_dsa_kernel, topk=topk),
        grid=(bsz, nblk),
        in_specs=[qcol(iqt.shape[0]), qcol(wht.shape[0]), qcol(qlt.shape[0]), full(ki), full(c), full(ct),
                  pl.BlockSpec(wuvt.shape, lambda b, i: (0, 0))],
        out_specs=pl.BlockSpec((1, BLOCK, GROUP_W), lambda b, i: (b, i, 0)),
        out_shape=jax.ShapeDtypeStruct((bsz, tp, GROUP_W), MXU_DTYPE),
        scratch_shapes=[pltpu.VMEM((tpk // KEY_STEP + 1, KEY_STEP, BLOCK), F32),
                        pltpu.VMEM((KEY_STEP, DSA_HEADS * BLOCK), F32), pltpu.VMEM((KEY_STEP, DSA_HEADS * BLOCK), F32),
                        pltpu.VMEM((KEY_STEP, IDX_HEADS * BLOCK), F32), pltpu.VMEM((KEY_STEP, IDX_HEADS * BLOCK), F32)],
        compiler_params=_params("parallel", "arbitrary"),
        name="dsa",
    )(iqt, wht, qlt, ki, c, ct, wuvt)


def _outproj_ffn_kernel(h_ref, yf_ref, yc_ref, yl_ref, yd_ref, w_ref, g_ref, gpre_ref, gpost_ref, win_ref, wout_ref,
                        o_ref, *, tm, tp, d_ff, chunk):
    mix = _dot(yf_ref[...], w_ref[0:GROUP_W, :])
    mix = mix + _dot(yc_ref[...], w_ref[GROUP_W:2 * GROUP_W, :])
    mix = mix + _dot(yl_ref[...], w_ref[2 * GROUP_W:3 * GROUP_W, :])
    mix = mix + _dot(yd_ref[...], w_ref[3 * GROUP_W:4 * GROUP_W, :])
    row = (pl.program_id(0) * tm) % tp + lax.broadcasted_iota(jnp.int32, (tm, 1), 0)
    x = jnp.where(row >= PAD, h_ref[...] + _rms(mix, g_ref[...]), 0.0)
    xn = _rms(x, gpre_ref[...]).astype(MXU_DTYPE)
    acc = jnp.zeros(x.shape, F32)
    for c in range(d_ff // chunk):
        gate = _dot(xn, win_ref[:, c * chunk:(c + 1) * chunk])
        up = _dot(xn, win_ref[:, d_ff + c * chunk:d_ff + (c + 1) * chunk])
        a = (gate * jax.nn.sigmoid(gate) * up).astype(MXU_DTYPE)
        acc = acc + _dot(a, wout_ref[c * chunk:(c + 1) * chunk, :])
    o_ref[...] = x + 0.5 * _rms(acc, gpost_ref[...])


def _outproj_ffn(h, yf, yc, yl, yd, w, g, g_pre, g_post, w_in, w_out, tp):
    rows, d = h.shape
    d_ff = w_out.shape[0]
    tm = _row_tile(rows)
    chunk = 512 if d_ff % 512 == 0 else d_ff
    row = lambda i: (i, 0)
    const = lambda i: (0, 0)
    return pl.pallas_call(
        functools.partial(_outproj_ffn_kernel, tm=tm, tp=tp, d_ff=d_ff, chunk=chunk),
        grid=(rows // tm,),
        in_specs=[pl.BlockSpec((tm, d), row)] + [pl.BlockSpec((tm, GROUP_W), row)] * 4
                 + [pl.BlockSpec(w.shape, const, pipeline_mode=pl.Buffered(1)),
                    pl.BlockSpec((1, d), const), pl.BlockSpec((1, d), const), pl.BlockSpec((1, d), const),
                    pl.BlockSpec(w_in.shape, const, pipeline_mode=pl.Buffered(1)),
                    pl.BlockSpec(w_out.shape, const, pipeline_mode=pl.Buffered(1))],
        out_specs=pl.BlockSpec((tm, d), row),
        out_shape=jax.ShapeDtypeStruct((rows, d), F32),
        compiler_params=_params("parallel"),
        name="out_proj_ffn",
    )(h, yf, yc, yl, yd, w, g, g_pre, g_post, w_in, w_out)


def _pack_w_in(w_in):
    offs = np.cumsum((0,) + SPLIT_SIZES)
    fq, fk, fv, ff, cu, lx, lg, dq, dkv, iq, ik, iw = (w_in[..., offs[i]:offs[i + 1]] for i in range(12))
    padc = lambda a, n: jnp.pad(a, ((0, 0), (0, 0), (0, n - a.shape[-1])))
    cols = [fq * (HEAD_DIM ** -0.5 * LOG2E), fk, fv, padc(ff, 128), cu, lx, lg, dq, dkv, iq,
            jnp.tile(ik, (1, 1, IDX_HEADS)), padc(iw, 128)]
    return jnp.concatenate(cols, axis=-1).astype(MXU_DTYPE)


def _block_diag(w):
    depth, n, a, b = w.shape
    eye = jnp.eye(n, dtype=w.dtype)
    return jnp.einsum("lnab,nm->lnamb", w, eye).reshape(depth, n * a, n * b)


def _lane_pad(a, n):
    return jnp.pad(a, [(0, 0)] * (a.ndim - 1) + [(0, n - a.shape[-1])])


def kernel(x, meta_tokens, norm_g, ffn_w_in, ffn_w_out, w_in, w_out, fox_b_f, conv_dw_w, conv_dw_b, conv_ln_g,
           conv_ln_b, lru_conv_w, lru_conv_b, lru_w_a, lru_b_a, lru_w_i, lru_b_i, lru_lambda, dsa_kv_norm_g,
           dsa_w_uk, dsa_w_uv, idx_k_ln_g, idx_k_ln_b):
    bsz, seq, d = x.shape
    depth = norm_g.shape[0]
    assert seq % BLOCK == 0 and d % 128 == 0
    topk = min(TOPK_MAX, seq // 4)
    tp = PAD + N_META + seq
    rows = bsz * tp

    ffn_w_in_m = ffn_w_in.astype(MXU_DTYPE)
    ffn_w_out_m = ffn_w_out.astype(MXU_DTYPE)
    w_in_m = _pack_w_in(w_in)
    w_out_m = w_out.astype(MXU_DTYPE)
    wuk_m = (_block_diag(dsa_w_uk.transpose(0, 1, 3, 2)) * (HEAD_DIM ** -0.5 * LOG2E)).astype(MXU_DTYPE)
    wuvt_m = _block_diag(dsa_w_uv).transpose(0, 2, 1).astype(MXU_DTYPE)
    wa_m = _block_diag(lru_w_a).astype(MXU_DTYPE)
    wi_m = _block_diag(lru_w_i).astype(MXU_DTYPE)
    row2 = lambda a: a[:, None, :]
    dww = jnp.pad(conv_dw_w, ((0, 0), (0, CONV_HALO - CONV_WIDTH), (0, 0)))
    lcw = jnp.pad(lru_conv_w, ((0, 0), (0, LRU_HALO - LRU_CONV_WIDTH), (0, 0)))
    ln_g8 = row2(jnp.tile(idx_k_ln_g, (1, IDX_HEADS)))
    ln_b8 = row2(jnp.tile(idx_k_ln_b, (1, IDX_HEADS)))

    meta = jnp.broadcast_to(meta_tokens[None].astype(x.dtype), (bsz, N_META, d))
    h = jnp.concatenate([jnp.zeros((bsz, PAD, d), x.dtype), meta, x], axis=1).reshape(rows, d)

    for l in range(depth):
        g = norm_g[l][:, None, :]
        h, z = _ffn_inproj(h, g[0], g[1], ffn_w_in_m[l, 0], ffn_w_out_m[l, 0],
                           g[2], w_in_m[l], wuk_m[l], row2(dsa_kv_norm_g)[l], ln_g8[l], ln_b8[l])
        seq_params = {
            "fox_b_f": _lane_pad(fox_b_f[l][None], 128), "conv_dw_w": dww[l], "conv_dw_b": row2(conv_dw_b)[l],
            "conv_ln_g": row2(conv_ln_g)[l], "conv_ln_b": row2(conv_ln_b)[l], "lru_conv_w": lcw[l],
            "lru_conv_b": row2(lru_conv_b)[l], "lru_w_a": wa_m[l], "lru_b_a": row2(lru_b_a)[l],
            "lru_w_i": wi_m[l], "lru_b_i": row2(lru_b_i)[l], "lru_lambda": row2(lru_lambda)[l]}
        y_conv, y_lru, cumt, kx = _seqmix(z, bsz, tp, seq_params)
        r3 = lambda a: a.reshape(bsz, tp, a.shape[-1])
        tpk = -(-tp // KEY_STEP) * KEY_STEP
        keys = lambda a: jnp.pad(r3(a), ((0, 0), (0, tpk - tp), (0, 0)))
        keys_t = lambda a: jnp.pad(a.reshape(a.shape[0], bsz, tp).transpose(1, 0, 2), ((0, 0), (0, 0), (0, tpk - tp)))
        y_fox = _fox(z["fqT"], keys(z["fk"]), keys(kx), keys_t(z["fvT"]), cumt)
        y_dsa = _dsa(z["iqT"], z["whT"], z["qlT"], keys(z["ki"]), keys(z["c"]), keys_t(z["cT"]), wuvt_m[l], topk)
        r2 = lambda a: a.reshape(rows, a.shape[-1])
        h = _outproj_ffn(h, r2(y_fox), r2(y_conv), r2(y_lru), r2(y_dsa), w_out_m[l], g[3], g[4], g[5],
                         ffn_w_in_m[l, 1], ffn_w_out_m[l, 1], tp)

    return h.reshape(bsz, tp, d)[:, PAD + N_META:]
```
